```python
import jax, jax.numpy as jnp
from jax import lax
import numpy as np

D_MODEL = 1024
BATCH = 8
SEQ = 2048
DEPTH = 2
DEC_BATCH = 128
DEC_SEQ = 4
PAST_LEN = 16384
PAGE_SIZE = 128

N_AB = (DEPTH + 1) // 2
N_C = DEPTH // 2
RMS_EPS = 1e-6
W_A = D_MODEL // 2
H_A = 8
BD_A = W_A // H_A
CONV_W = 4
LRU_C = 8.0
W_B = D_MODEL // 2
HD_B = 64
H_B = W_B // HD_B
LORA_W = 64
LORA_A = 64
LORA_G = 128
GN_EPS = 64e-5
P_A = 2 * W_A
P_B = 3 * W_B + LORA_W + LORA_A + LORA_G
P_AB = P_A + P_B
DK_C = 128
H_C = D_MODEL // DK_C
DV_C = D_MODEL // H_C
F_C = H_C * DK_C
P_C = 2 * F_C + 2 * D_MODEL
CHUNK = 64
D_FF = -(-(8 * D_MODEL) // (3 * 256)) * 256

kernel_name = 'hawk_rwkv7_hgrn2_hybrid_step'

F32 = jnp.float32


def rmsnorm(x, g):
    xf = x.astype(F32)
    y = xf * lax.rsqrt(jnp.mean(xf * xf, axis=-1, keepdims=True) + RMS_EPS)
    return (y * g.astype(F32)).astype(x.dtype)


def causal_conv(u, buf, w, b):
    T = u.shape[1]
    full = jnp.concatenate([buf.astype(u.dtype), u], axis=1)
    out = b + full[:, 0:T] * w[0]
    for j in range(1, CONV_W):
        out = out + full[:, j:j + T] * w[j]
    return out, full[:, full.shape[1] - (CONV_W - 1):]


def _lin_comb(e1, e2):
    a1, b1 = e1
    a2, b2 = e2
    return a1 * a2, a2 * b1 + b2


def rg_lru(u, h0, gr_w, gr_b, gi_w, gi_b, lam):
    B, T, _ = u.shape
    uf = u.astype(F32)
    ub = uf.reshape(B, T, H_A, BD_A)
    r = jax.nn.sigmoid(jnp.einsum('bthi,hij->bthj', ub, gr_w).reshape(B, T, W_A) + gr_b)
    ig = jax.nn.sigmoid(jnp.einsum('bthi,hij->bthj', ub, gi_w).reshape(B, T, W_A) + gi_b)
    log_a = -LRU_C * r * jax.nn.softplus(-lam.astype(F32))
    a = jnp.exp(log_a)
    bterm = jnp.sqrt(-jnp.expm1(2.0 * log_a)) * (ig * uf)
    bterm = bterm.at[:, 0].add(a[:, 0] * h0.astype(F32))
    _, h = lax.associative_scan(_lin_comb, (a, bterm), axis=1)
    return h, h[:, -1]


def _rwkv_step(S, inp):
    r_t, w_t, k_t, v_t, kk_t, a_t = inp
    sk = jnp.einsum('bhvk,bhk->bhv', S, kk_t)
    S = (S * w_t[:, :, None, :] - sk[..., None] * (kk_t * a_t)[:, :, None, :]
         + v_t[..., None] * k_t[:, :, None, :])
    return S, jnp.einsum('bhvk,bhk->bhv', S, r_t)


def rwkv7(pb, prev, S0, mu, w0, w2, a0, a2, g2, k_k, k_a, r_k, lnx_w, lnx_b):
    B, T, _ = pb.shape
    pf = pb.astype(F32)
    shifted = jnp.concatenate([prev[:, None].astype(F32), pf[:, :-1]], axis=1)
    m = pf + (shifted - pf) * mu
    o1 = 3 * W_B
    r = m[..., :W_B]
    k = m[..., W_B:2 * W_B]
    v = m[..., 2 * W_B:o1]
    xw = m[..., o1:o1 + LORA_W]
    xa = m[..., o1 + LORA_W:o1 + LORA_W + LORA_A]
    xg = m[..., o1 + LORA_W + LORA_A:]
    w_log = -jax.nn.softplus(-(w0 + jnp.tanh(xw) @ w2)) - 0.5
    decay = jnp.exp(-jnp.exp(w_log))
    a = jax.nn.sigmoid(a0 + xa @ a2)
    g = jax.nn.sigmoid(xg) @ g2
    hd = lambda z: z.reshape(B, T, H_B, HD_B)
    kk = hd(k * k_k)
    kk = kk * lax.rsqrt(jnp.maximum(jnp.sum(kk * kk, axis=-1, keepdims=True), 1e-24))
    k = k * (1.0 + (a - 1.0) * k_a)
    r4, k4, v4 = hd(r), hd(k), hd(v)
    tm = lambda z: jnp.swapaxes(z, 0, 1)
    S_T, o = lax.scan(_rwkv_step, S0.astype(F32),
                      (tm(r4), tm(hd(decay)), tm(k4), tm(v4), tm(kk), tm(hd(a))))
    o = tm(o)
    mean = jnp.mean(o, axis=-1, keepdims=True)
    var = jnp.mean(jnp.square(o - mean), axis=-1, keepdims=True)
    on = ((o - mean) * lax.rsqrt(var + GN_EPS)).reshape(B, T, W_B) * lnx_w + lnx_b
    bonus = (jnp.sum(r4 * k4 * r_k, axis=-1, keepdims=True) * v4).reshape(B, T, W_B)
    return (on + bonus) * g, S_T, pb[:, -1]


def hgrn2(pc, S0, lb, gn):
    B, T, _ = pc.shape
    pf = pc.astype(F32)
    q = jax.nn.silu(pf[..., :F_C])
    fr = pf[..., F_C:2 * F_C]
    iv = pf[..., 2 * F_C:2 * F_C + D_MODEL]
    gg = pf[..., 2 * F_C + D_MODEL:]
    lbf = lb.astype(F32)
    logf = jnp.logaddexp(jnp.log(lbf), jnp.log1p(-lbf) + jax.nn.log_sigmoid(fr))
    k = (1.0 - lbf) * jax.nn.sigmoid(-fr)
    c = min(CHUNK, T)
    n = -(-T // c)
    pad = n * c - T

    def blocks(z, d):
        z = jnp.pad(z, ((0, 0), (0, pad), (0, 0)))
        return z.reshape(B, n, c, H_C, d).transpose(1, 0, 3, 2, 4)

    mask = jnp.tril(jnp.ones((c, c), bool))[:, :, None]

    def step(S, inp):
        qc, kc, lc, ic = inp
        bc = jnp.cumsum(lc, axis=2)
        diff = bc[:, :, :, None, :] - bc[:, :, None, :, :]
        dec = jnp.exp(jnp.where(mask, diff, -jnp.inf))
        att = jnp.einsum('bhtsk,bhtk,bhsk->bhts', dec, qc, kc)
        o = (jnp.einsum('bhts,bhsv->bhtv', att, ic)
             + jnp.einsum('bhtk,bhkv->bhtv', qc * jnp.exp(bc), S))
        bl = bc[:, :, -1]
        S = (jnp.exp(bl)[..., None] * S
             + jnp.einsum('bhsk,bhsv->bhkv', kc * jnp.exp(bl[:, :, None] - bc), ic))
        return S, o

    S_T, o = lax.scan(step, S0.astype(F32),
                      (blocks(q, DK_C), blocks(k, DK_C), blocks(logf, DK_C), blocks(iv, DV_C)))
    o = o.transpose(1, 0, 3, 2, 4).reshape(B, n * c, H_C * DV_C)[:, :T]
    return rmsnorm(o, gn) * jax.nn.silu(gg), S_T


def trunk(x, conv_st, h_st, shift_st, rS_st, hS_st, P):
    lb_cum = jnp.cumsum(jax.nn.softmax(P['lb_c'].astype(F32), axis=0), axis=0)
    n_conv, n_h, n_shift, n_rS, n_hS = [], [], [], [], []
    for l in range(DEPTH):
        j = l // 2
        xn = rmsnorm(x, P['ln_mix'][l])
        if l % 2 == 0:
            p = xn @ P['w_in_ab'][j]
            u, conv_new = causal_conv(p[..., :W_A], conv_st[j], P['conv_w'][j], P['conv_b'][j])
            h, h_new = rg_lru(u, h_st[j], P['gr_w'][j], P['gr_b'][j], P['gi_w'][j],
                              P['gi_b'][j], P['lru_lambda'][j])
            out_a = jax.nn.gelu(p[..., W_A:P_A].astype(F32)) * h
            out_b, rS_new, shift_new = rwkv7(
                p[..., P_A:], shift_st[j], rS_st[j], P['mu_b'][j], P['w0_b'][j], P['w2_b'][j],
                P['a0_b'][j], P['a2_b'][j], P['g2_b'][j], P['kk_b'][j], P['ka_b'][j],
                P['rk_b'][j], P['lnx_w'][j], P['lnx_b'][j])
            y = jnp.concatenate([out_a, out_b], axis=-1).astype(x.dtype) @ P['w_out_ab'][j]
            n_conv.append(conv_new.astype(conv_st.dtype))
            n_h.append(h_new.astype(h_st.dtype))
            n_shift.append(shift_new.astype(shift_st.dtype))
            n_rS.append(rS_new.astype(rS_st.dtype))
        else:
            pc = xn @ P['w_in_c'][j]
            o, hS_new = hgrn2(pc, hS_st[j], lb_cum[l] - lb_cum[0], P['gn_c'][j])
            y = o.astype(x.dtype) @ P['w_out_c'][j]
            n_hS.append(hS_new.astype(hS_st.dtype))
        x = x + y
        xf = rmsnorm(x, P['ln_ffn'][l])
        x = x + (jax.nn.silu(xf @ P['ffn_gate'][l]) * (xf @ P['ffn_up'][l])) @ P['ffn_down'][l]
    return (rmsnorm(x, P['ln_final']), jnp.stack(n_conv), jnp.stack(n_h), jnp.stack(n_shift),
            jnp.stack(n_rS), jnp.stack(n_hS))


def setup_inputs(seed: int = 0) -> dict:
    key = jax.random.key(seed)
    ks = iter(jax.random.split(key, 48))
    nrm = lambda shape, s: jax.random.normal(next(ks), shape, F32) * s
    uni = lambda shape, lo, hi: jax.random.uniform(next(ks), shape, F32, lo, hi)
    d = {}
    d['x_prompt'] = nrm((BATCH, SEQ, D_MODEL), 1.0)
    d['x_sample'] = nrm((DEC_BATCH, DEC_SEQ, D_MODEL), 1.0)
    d['state_rglru_conv'] = nrm((N_AB, DEC_BATCH, CONV_W - 1, W_A), 1.0)
    d['state_rglru_h'] = nrm((N_AB, DEC_BATCH, W_A), 0.5)
    d['state_rwkv_shift'] = nrm((N_AB, DEC_BATCH, P_B), 1.0)
    d['state_rwkv_S'] = nrm((N_AB, DEC_BATCH, H_B, HD_B, HD_B), 0.5)
    d['state_hgrn_S'] = nrm((N_C, DEC_BATCH, H_C, DK_C, DV_C), 0.5)
    d['ln_mix'] = 1.0 + nrm((DEPTH, D_MODEL), 0.02)
    d['ln_ffn'] = 1.0 + nrm((DEPTH, D_MODEL), 0.02)
    d['ln_final'] = 1.0 + nrm((D_MODEL,), 0.02)
    d['w_in_ab'] = nrm((N_AB, D_MODEL, P_AB), D_MODEL ** -0.5)
    d['conv_w'] = nrm((N_AB, CONV_W, W_A), 0.5)
    d['conv_b'] = nrm((N_AB, W_A), 0.01)
    d['gr_w'] = nrm((N_AB, H_A, BD_A, BD_A), BD_A ** -0.5)
    d['gr_b'] = nrm((N_AB, W_A), 0.01)
    d['gi_w'] = nrm((N_AB, H_A, BD_A, BD_A), BD_A ** -0.5)
    d['gi_b'] = nrm((N_AB, W_A), 0.01)
    s = uni((N_AB, W_A), 0.9, 0.999) ** (1.0 / LRU_C)
    d['lru_lambda'] = jnp.log(s) - jnp.log1p(-s)
    d['mu_b'] = uni((N_AB, P_B), 0.0, 1.0)
    d['w0_b'] = uni((N_AB, W_B), -6.0, -1.0)
    d['w2_b'] = nrm((N_AB, LORA_W, W_B), 0.1 * LORA_W ** -0.5)
    d['a0_b'] = nrm((N_AB, W_B), 0.1)
    d['a2_b'] = nrm((N_AB, LORA_A, W_B), LORA_A ** -0.5)
    d['g2_b'] = nrm((N_AB, LORA_G, W_B), LORA_G ** -0.5)
    d['kk_b'] = 0.85 + nrm((N_AB, W_B), 0.02)
    d['ka_b'] = 1.0 + nrm((N_AB, W_B), 0.02)
    d['rk_b'] = nrm((N_AB, H_B, HD_B), 0.1)
    d['lnx_w'] = 1.0 + nrm((N_AB, W_B), 0.02)
    d['lnx_b'] = nrm((N_AB, W_B), 0.01)
    d['w_out_ab'] = nrm((N_AB, W_A + W_B, D_MODEL), (W_A + W_B) ** -0.5)
    d['w_in_c'] = nrm((N_C, D_MODEL, P_C), D_MODEL ** -0.5)
    d['lb_c'] = nrm((DEPTH, F_C), 1.0)
    d['gn_c'] = 1.0 + nrm((N_C, D_MODEL), 0.02)
    d['w_out_c'] = nrm((N_C, D_MODEL, D_MODEL), D_MODEL ** -0.5)
    d['ffn_gate'] = nrm((DEPTH, D_MODEL, D_FF), D_MODEL ** -0.5)
    d['ffn_up'] = nrm((DEPTH, D_MODEL, D_FF), D_MODEL ** -0.5)
    d['ffn_down'] = nrm((DEPTH, D_FF, D_MODEL), D_FF ** -0.5)
    return d


def reference(x_prompt, x_sample, state_rglru_conv, state_rglru_h, state_rwkv_shift, state_rwkv_S,
              state_hgrn_S, ln_mix, ln_ffn, ln_final, w_in_ab, conv_w, conv_b, gr_w, gr_b, gi_w,
              gi_b, lru_lambda, mu_b, w0_b, w2_b, a0_b, a2_b, g2_b, kk_b, ka_b, rk_b, lnx_w, lnx_b,
              w_out_ab, w_in_c, lb_c, gn_c, w_out_c, ffn_gate, ffn_up, ffn_down):
    P = dict(ln_mix=ln_mix, ln_ffn=ln_ffn, ln_final=ln_final, w_in_ab=w_in_ab, conv_w=conv_w,
             conv_b=conv_b, gr_w=gr_w, gr_b=gr_b, gi_w=gi_w, gi_b=gi_b, lru_lambda=lru_lambda,
             mu_b=mu_b, w0_b=w0_b, w2_b=w2_b, a0_b=a0_b, a2_b=a2_b, g2_b=g2_b, kk_b=kk_b,
             ka_b=ka_b, rk_b=rk_b, lnx_w=lnx_w, lnx_b=lnx_b, w_out_ab=w_out_ab, w_in_c=w_in_c,
             lb_c=lb_c, gn_c=gn_c, w_out_c=w_out_c, ffn_gate=ffn_gate, ffn_up=ffn_up,
             ffn_down=ffn_down)
    bp = x_prompt.shape[0]
    dt = x_prompt.dtype
    z_conv = jnp.zeros((N_AB, bp, CONV_W - 1, W_A), dt)
    z_h = jnp.zeros((N_AB, bp, W_A), dt)
    z_shift = jnp.zeros((N_AB, bp, P_B), dt)
    z_rS = jnp.zeros((N_AB, bp, H_B, HD_B, HD_B), dt)
    z_hS = jnp.zeros((N_C, bp, H_C, DK_C, DV_C), dt)
    y_prompt, p_conv, p_h, p_shift, p_rS, p_hS = trunk(x_prompt, z_conv, z_h, z_shift, z_rS, z_hS, P)
    y_sample, s_conv, s_h, s_shift, s_rS, s_hS = trunk(
        x_sample, state_rglru_conv, state_rglru_h, state_rwkv_shift, state_rwkv_S, state_hgrn_S, P)
    return (y_prompt, y_sample, p_conv, p_h, p_shift, p_rS, p_hS,
            s_conv, s_h, s_shift, s_rS, s_hS)
```

```python
import functools

import jax
import jax.numpy as jnp
from jax import lax
from jax.experimental import pallas as pl
from jax.experimental.pallas import tpu as pltpu

F32 = jnp.float32
BF16 = jnp.bfloat16

D_MODEL = 1024
W_A = 512
H_A = 8
CONV_W = 4
LRU_C = 8.0
W_B = 512
HD_B = 64
H_B = 8
LORA_W = 64
LORA_A = 64
LORA_G = 128
P_A = 2 * W_A
P_B = 3 * W_B + LORA_W + LORA_A + LORA_G
DK_C = 128
H_C = 8
F_C = 1024
D_FF = 2816
RMS_EPS = 1e-6
GN_EPS = 64e-5

LANES = 128
SUBLANES = 8
VMEM_LIMIT = 56 * 1024 * 1024


def _cparams(sem):
    return pltpu.CompilerParams(dimension_semantics=sem, vmem_limit_bytes=VMEM_LIMIT)


def _softplus(x):
    return jnp.maximum(x, 0.0) + jnp.log1p(jnp.exp(-jnp.abs(x)))


def _sigmoid(x):
    return 1.0 / (1.0 + jnp.exp(-x))


def _rms(x, g):
    ms = jnp.mean(x * x, axis=-1, keepdims=True)
    return x * lax.rsqrt(ms + RMS_EPS) * g


def _dot(a, b):
    return jnp.dot(a.astype(BF16), b.astype(BF16), preferred_element_type=F32)


def _norm_matmul_kernel(x_ref, g_ref, w_ref, *o_refs):
    xn = _rms(x_ref[...], g_ref[...]).astype(BF16)
    off = 0
    for o_ref in o_refs:
        n = o_ref.shape[-1]
        o_ref[...] = jnp.dot(xn, w_ref[:, off:off + n], preferred_element_type=F32)
        off += n


def _norm_matmul(x, g, w, splits, tm):
    n, d = x.shape
    p = w.shape[1]
    assert sum(splits) == p and n % tm == 0
    return pl.pallas_call(
        _norm_matmul_kernel,
        grid=(n // tm,),
        in_specs=[pl.BlockSpec((tm, d), lambda i: (i, 0)),
                  pl.BlockSpec((1, d), lambda i: (0, 0)),
                  pl.BlockSpec((d, p), lambda i: (0, 0))],
        out_specs=[pl.BlockSpec((tm, s), lambda i: (i, 0)) for s in splits],
        out_shape=[jax.ShapeDtypeStruct((n, s), F32) for s in splits],
        compiler_params=_cparams(("parallel",)),
        name="norm_matmul",
    )(x, g.reshape(1, d), w)


def _rglru_kernel(u_ref, gate_ref, cst_ref, h0_ref, cw_ref, cb_ref, wr_ref, wi_ref, br_ref,
                  bi_ref, lam_ref, out_ref, cnew_ref, hnew_ref, xbuf, hcar, bbuf,
                  *, step, rows, pad):
    t = pl.program_id(1)
    hist = (CONV_W - 1) * step

    @pl.when(t == 0)
    def _():
        xbuf[pad - hist:pad, :] = cst_ref[0]
        hcar[...] = h0_ref[0]

    xbuf[pad:pad + rows, :] = u_ref[...]
    conv = cb_ref[...] + xbuf[pad - hist:pad - hist + rows, :] * cw_ref[0:1, :]
    for j in range(1, CONV_W):
        o = pad - hist + j * step
        conv = conv + xbuf[o:o + rows, :] * cw_ref[j:j + 1, :]
    tail = xbuf[pad + rows - hist:pad + rows, :]
    cnew_ref[0] = tail
    xbuf[pad - hist:pad, :] = tail

    ub = conv.astype(BF16)
    r = _sigmoid(jnp.dot(ub, wr_ref[...], preferred_element_type=F32) + br_ref[...])
    ig = _sigmoid(jnp.dot(ub, wi_ref[...], preferred_element_type=F32) + bi_ref[...])
    log_a = (-LRU_C) * r * _softplus(-lam_ref[...])
    a = jnp.exp(log_a)
    bterm = jnp.sqrt(-jnp.tanh(log_a) * (a * a + 1.0)) * (ig * conv)
    bbuf[...] = bterm
    bbuf[0:step, :] = bterm[0:step, :] + a[0:step, :] * hcar[...]
    bv = bbuf[...]

    row = lax.broadcasted_iota(jnp.int32, (rows, W_A), 0)
    d = step
    while d < rows:
        a_sh = pltpu.roll(a, d, 0)
        b_sh = pltpu.roll(bv, d, 0)
        m = row >= d
        bv = jnp.where(m, a * b_sh + bv, bv)
        a = jnp.where(m, a * a_sh, a)
        d *= 2
    hcar[...] = bv[rows - step:rows, :]
    hnew_ref[0] = bv[rows - step:rows, :]
    out_ref[...] = jax.nn.gelu(gate_ref[...]) * bv


def _rglru(pa, conv_st, h_st, cw, cb, wr, wi, br, bi, lam, *, nseq, step, rows):
    n = pa.shape[0]
    nt = n // (nseq * rows)
    hist = (CONV_W - 1) * step
    pad = max(SUBLANES, hist)
    row_map = lambda b, t: (b * nt + t, 0)
    vec = lambda: pl.BlockSpec((1, W_A), lambda b, t: (0, 0))
    kern = functools.partial(_rglru_kernel, step=step, rows=rows, pad=pad)
    return pl.pallas_call(
        kern,
        grid=(nseq, nt),
        in_specs=[pl.BlockSpec((rows, W_A), row_map),
                  pl.BlockSpec((rows, W_A), lambda b, t: (b * nt + t, 1)),
                  pl.BlockSpec((1, hist, W_A), lambda b, t: (b, 0, 0)),
                  pl.BlockSpec((1, step, W_A), lambda b, t: (b, 0, 0)),
                  pl.BlockSpec((CONV_W, W_A), lambda b, t: (0, 0)),
                  vec(),
                  pl.BlockSpec((W_A, W_A), lambda b, t: (0, 0)),
                  pl.BlockSpec((W_A, W_A), lambda b, t: (0, 0)),
                  vec(), vec(), vec()],
        out_specs=[pl.BlockSpec((rows, W_A), row_map),
                   pl.BlockSpec((1, hist, W_A), lambda b, t: (b, 0, 0)),
                   pl.BlockSpec((1, step, W_A), lambda b, t: (b, 0, 0))],
        out_shape=[jax.ShapeDtypeStruct((n, W_A), F32),
                   jax.ShapeDtypeStruct((nseq, hist, W_A), F32),
                   jax.ShapeDtypeStruct((nseq, step, W_A), F32)],
        scratch_shapes=[pltpu.VMEM((pad + rows, W_A), F32),
                        pltpu.VMEM((step, W_A), F32),
                        pltpu.VMEM((rows, W_A), F32)],
        compiler_params=_cparams(("parallel", "arbitrary")),
        name="rglru",
    )(pa, pa, conv_st, h_st, cw, cb.reshape(1, W_A), wr, wi, br.reshape(1, W_A),
      bi.reshape(1, W_A), lam.reshape(1, W_A))


def _rwkv_prep_kernel(pb_ref, prev_ref, mu_ref, w0_ref, a0_ref, w2_ref, a2_ref, g2_ref,
                      r_ref, k_ref, v_ref, dec_ref, a_ref, g_ref, snew_ref, pbuf,
                      *, step, rows, pad):
    t = pl.program_id(1)

    @pl.when(t == 0)
    def _():
        pbuf[pad - step:pad, :] = prev_ref[0]

    pf = pb_ref[...]
    pbuf[pad:pad + rows, :] = pf
    shifted = pbuf[pad - step:pad - step + rows, :]
    last = pf[rows - step:rows, :]
    pbuf[pad - step:pad, :] = last
    snew_ref[0] = last

    m = pf + (shifted - pf) * mu_ref[...]
    o1 = 3 * W_B
    r_ref[...] = m[:, 0:W_B]
    k_ref[...] = m[:, W_B:2 * W_B]
    v_ref[...] = m[:, 2 * W_B:o1]
    xwa = m[:, o1:o1 + LORA_W + LORA_A]
    xg = m[:, o1 + LORA_W + LORA_A:]
    lane = lax.broadcasted_iota(jnp.int32, xwa.shape, 1)
    lhs = jnp.where(lane < LORA_W, jnp.tanh(xwa), xwa).astype(BF16)
    lw = jnp.dot(lhs, w2_ref[...], preferred_element_type=F32)
    la = jnp.dot(lhs, a2_ref[...], preferred_element_type=F32)
    w_log = -_softplus(-(w0_ref[...] + lw)) - 0.5
    dec_ref[...] = jnp.exp(-jnp.exp(w_log))
    a_ref[...] = _sigmoid(a0_ref[...] + la)
    g_ref[...] = jnp.dot(_sigmoid(xg).astype(BF16), g2_ref[...], preferred_element_type=F32)


def _rwkv_prep(pb, prev, mu, w0, a0, w2p, a2p, g2, *, nseq, step, rows):
    n = pb.shape[0]
    nt = n // (nseq * rows)
    pad = max(SUBLANES, step)
    row_map = lambda b, t: (b * nt + t, 0)
    const = lambda shape: pl.BlockSpec(shape, lambda b, t: (0,) * len(shape))
    kern = functools.partial(_rwkv_prep_kernel, step=step, rows=rows, pad=pad)
    outs = pl.pallas_call(
        kern,
        grid=(nseq, nt),
        in_specs=[pl.BlockSpec((rows, P_B), row_map),
                  pl.BlockSpec((1, step, P_B), lambda b, t: (b, 0, 0)),
                  const((1, P_B)), const((1, W_B)), const((1, W_B)),
                  const((LORA_W + LORA_A, W_B)), const((LORA_W + LORA_A, W_B)),
                  const((LORA_G, W_B))],
        out_specs=[pl.BlockSpec((rows, W_B), row_map)] * 6
                  + [pl.BlockSpec((1, step, P_B), lambda b, t: (b, 0, 0))],
        out_shape=[jax.ShapeDtypeStruct((n, W_B), F32)] * 6
                  + [jax.ShapeDtypeStruct((nseq, step, P_B), F32)],
        scratch_shapes=[pltpu.VMEM((pad + rows, P_B), F32)],
        compiler_params=_cparams(("parallel", "arbitrary")),
        name="rwkv_prep",
    )(pb, prev, mu.reshape(1, P_B), w0.reshape(1, W_B), a0.reshape(1, W_B), w2p, a2p, g2)
    return outs


def _rwkv_scan_kernel(r_ref, k_ref, v_ref, dec_ref, a_ref, s0_ref, kkp_ref, kap_ref, rkp_ref,
                      lw_ref, lb_ref, y_ref, sout_ref, s_scr, o_scr, *, tb, fold):
    tblk = pl.program_id(1)

    @pl.when(tblk == 0)
    def _():
        s_scr[...] = s0_ref[...]

    def colsum(x):
        s = jnp.sum(x, axis=0, keepdims=True)
        if fold:
            s = s + pltpu.roll(s, LANES // 2, 1)
        return s

    kkp = kkp_ref[...]
    kap = kap_ref[...]
    rkp = rkp_ref[...]
    sub = lax.broadcasted_iota(jnp.int32, (SUBLANES, LANES), 0)

    def step_fn(t, carry):
        kraw = k_ref[t]
        a = a_ref[t]
        r = r_ref[t]
        w = dec_ref[t]
        kk = kraw * kkp
        kk = kk * lax.rsqrt(jnp.maximum(colsum(kk * kk), 1e-24))
        kmod = kraw * (1.0 + (a - 1.0) * kap)
        bvec = kk * a

        def vblock(vb, c):
            base = pl.multiple_of(vb * SUBLANES, SUBLANES)
            vt = v_ref[t, pl.ds(base, SUBLANES), :]
            otile = jnp.zeros((SUBLANES, LANES), F32)
            for j in range(SUBLANES):
                sv = s_scr[base + j]
                sk = colsum(sv * kk)
                sv = sv * w - sk * bvec + vt[j:j + 1, :] * kmod
                s_scr[base + j] = sv
                otile = jnp.where(sub == j, colsum(sv * r), otile)
            o_scr[pl.ds(base, SUBLANES), :] = otile
            return c

        lax.fori_loop(0, HD_B // SUBLANES, vblock, 0)
        o = o_scr[...]
        mean = jnp.mean(o, axis=0, keepdims=True)
        var = jnp.mean(jnp.square(o - mean), axis=0, keepdims=True)
        on = (o - mean) * lax.rsqrt(var + GN_EPS) * lw_ref[...] + lb_ref[...]
        bonus = colsum(r * kmod * rkp) * v_ref[t]
        y_ref[t] = on + bonus
        return carry

    lax.fori_loop(0, tb, step_fn, 0)
    sout_ref[...] = s_scr[...]


def _rwkv_scan(r, k, v, dec, a, s0, kkp, kap, rkp, lw, lb, *, tb, fold):
    nt, kr, nl = r.shape
    lg = nl // LANES
    seq = lambda rows: pl.BlockSpec((tb, rows, LANES), lambda g, t: (t, 0, g))
    par = lambda rows: pl.BlockSpec((rows, LANES), lambda g, t: (0, g))
    st = pl.BlockSpec((HD_B, kr, LANES), lambda g, t: (0, 0, g))
    kern = functools.partial(_rwkv_scan_kernel, tb=tb, fold=fold)
    return pl.pallas_call(
        kern,
        grid=(lg, nt // tb),
        in_specs=[seq(kr), seq(kr), seq(HD_B), seq(kr), seq(kr), st,
                  par(kr), par(kr), par(kr), par(HD_B), par(HD_B)],
        out_specs=[seq(HD_B), st],
        out_shape=[jax.ShapeDtypeStruct((nt, HD_B, nl), F32),
                   jax.ShapeDtypeStruct((HD_B, kr, nl), F32)],
        scratch_shapes=[pltpu.VMEM((HD_B, kr, LANES), F32),
                        pltpu.VMEM((HD_B, LANES), F32)],
        compiler_params=_cparams(("parallel", "arbitrary")),
        name="rwkv_scan",
    )(r, k, v, dec, a, s0, kkp, kap, rkp, lw, lb)


def _mix_out_kernel(x_ref, oa_ref, yb_ref, g_ref, wa_ref, wb_ref, o_ref):
    ob = yb_ref[...] * g_ref[...]
    o_ref[...] = (x_ref[...]
                  + jnp.dot(oa_ref[...].astype(BF16), wa_ref[...], preferred_element_type=F32)
                  + jnp.dot(ob.astype(BF16), wb_ref[...], preferred_element_type=F32))


def _mix_out(x, oa, yb, g, wa, wb, tm):
    n, d = x.shape
    row = lambda w: pl.BlockSpec((tm, w), lambda i: (i, 0))
    return pl.pallas_call(
        _mix_out_kernel,
        grid=(n // tm,),
        in_specs=[row(d), row(W_A), row(W_B), row(W_B),
                  pl.BlockSpec((W_A, d), lambda i: (0, 0)),
                  pl.BlockSpec((W_B, d), lambda i: (0, 0))],
        out_specs=row(d),
        out_shape=jax.ShapeDtypeStruct((n, d), F32),
        compiler_params=_cparams(("parallel",)),
        name="mix_out",
    )(x, oa, yb, g, wa, wb)


def _proj_res_kernel(x_ref, o_ref, w_ref, y_ref):
    y_ref[...] = x_ref[...] + jnp.dot(o_ref[...].astype(BF16), w_ref[...],
                                      preferred_element_type=F32)


def _proj_res(x, o, w, tm):
    n, d = x.shape
    row = pl.BlockSpec((tm, d), lambda i: (i, 0))
    return pl.pallas_call(
        _proj_res_kernel,
        grid=(n // tm,),
        in_specs=[row, row, pl.BlockSpec((d, d), lambda i: (0, 0))],
        out_specs=row,
        out_shape=jax.ShapeDtypeStruct((n, d), F32),
        compiler_params=_cparams(("parallel",)),
        name="proj_res",
    )(x, o, w)


def _ffn_kernel(x_ref, g_ref, wg_ref, wu_ref, wd_ref, gf_ref, o_ref, xf_scr, acc_scr,
                *, final_norm):
    j = pl.program_id(1)

    @pl.when(j == 0)
    def _():
        x = x_ref[...]
        xf_scr[...] = _rms(x, g_ref[...]).astype(BF16)
        acc_scr[...] = x

    xf = xf_scr[...]
    hg = jnp.dot(xf, wg_ref[...], preferred_element_type=F32)
    hu = jnp.dot(xf, wu_ref[...], preferred_element_type=F32)
    h = (hg * _sigmoid(hg)) * hu
    acc_scr[...] += jnp.dot(h.astype(BF16), wd_ref[...], preferred_element_type=F32)

    @pl.when(j == pl.num_programs(1) - 1)
    def _():
        y = acc_scr[...]
        if final_norm:
            y = _rms(y, gf_ref[...])
        o_ref[...] = y


def _ffn(x, g, wg, wu, wd, gf, *, tm, tf, final_norm):
    n, d = x.shape
    ff = wg.shape[1]
    kern = functools.partial(_ffn_kernel, final_norm=final_norm)
    return pl.pallas_call(
        kern,
        grid=(n // tm, ff // tf),
        in_specs=[pl.BlockSpec((tm, d), lambda i, j: (i, 0)),
                  pl.BlockSpec((1, d), lambda i, j: (0, 0)),
                  pl.BlockSpec((d, tf), lambda i, j: (0, j)),
                  pl.BlockSpec((d, tf), lambda i, j: (0, j)),
                  pl.BlockSpec((tf, d), lambda i, j: (j, 0)),
                  pl.BlockSpec((1, d), lambda i, j: (0, 0))],
        out_specs=pl.BlockSpec((tm, d), lambda i, j: (i, 0)),
        out_shape=jax.ShapeDtypeStruct((n, d), F32),
        scratch_shapes=[pltpu.VMEM((tm, d), BF16), pltpu.VMEM((tm, d), F32)],
        compiler_params=_cparams(("parallel", "arbitrary")),
        name="ffn",
    )(x, g.reshape(1, d), wg, wu, wd, gf.reshape(1, d))


def _hgrn_kernel(q_ref, f_ref, i_ref, g_ref, s0_ref, lbc_ref, gn_ref, o_ref, sout_ref,
                 s_scr, q_scr, k_scr, bc_scr, i_scr, oh_scr, *, layer, chunk, valid):
    c = pl.program_id(1)

    @pl.when(c == 0)
    def _():
        s_scr[...] = s0_ref[0]

    lbc = lbc_ref[...]
    e = jnp.exp(lbc - jnp.max(lbc, axis=0, keepdims=True))
    sm = e / jnp.sum(e, axis=0, keepdims=True)
    lbf = sm[1:2, :]
    for l in range(2, layer + 1):
        lbf = lbf + sm[l:l + 1, :]

    fr = f_ref[0]
    x1 = jnp.log(lbf)
    x2 = jnp.log1p(-lbf) - _softplus(-fr)
    logf = jnp.maximum(x1, x2) + jnp.log1p(jnp.exp(-jnp.abs(x1 - x2)))
    kfull = (1.0 - lbf) * _sigmoid(-fr)
    if valid < chunk:
        rowv = lax.broadcasted_iota(jnp.int32, logf.shape, 0)
        logf = jnp.where(rowv < valid, logf, 0.0)
        kfull = jnp.where(rowv < valid, kfull, 0.0)
    qfull = jax.nn.silu(q_ref[0])

    bc = logf
    rowc = lax.broadcasted_iota(jnp.int32, bc.shape, 0)
    d = 1
    while d < chunk:
        bc = bc + jnp.where(rowc >= d, pltpu.roll(bc, d, 0), 0.0)
        d *= 2

    ivfull = i_ref[0]
    for h in range(H_C):
        sl = slice(h * DK_C, (h + 1) * DK_C)
        q_scr[h] = qfull[:, sl]
        k_scr[h] = kfull[:, sl]
        bc_scr[h] = bc[:, sl]
        i_scr[h] = ivfull[:, sl]

    nblk = chunk // SUBLANES
    sub = lax.broadcasted_iota(jnp.int32, (SUBLANES, DK_C), 0)
    lane_c = lax.broadcasted_iota(jnp.int32, (SUBLANES, chunk), 1)

    def head(h, carry):
        q = q_scr[h]
        k = k_scr[h]
        b = bc_scr[h]
        iv = i_scr[h].astype(BF16)
        s_h = s_scr[h]
        ablk = [jnp.zeros((SUBLANES, chunk), F32) for _ in range(nblk)]
        for s in range(min(chunk, valid)):
            b_s = b[s:s + 1, :]
            k_s = k[s:s + 1, :]
            for tb in range(s // SUBLANES, nblk):
                rs = slice(tb * SUBLANES, (tb + 1) * SUBLANES)
                diff = b[rs, :] - b_s
                if tb == s // SUBLANES:
                    diff = jnp.where(sub >= (s % SUBLANES), diff, -jnp.inf)
                ev = jnp.exp(diff) * (q[rs, :] * k_s)
                col = jnp.sum(ev, axis=1, keepdims=True)
                ablk[tb] = jnp.where(lane_c == s, col, ablk[tb])
        att = jnp.concatenate(ablk, axis=0) if nblk > 1 else ablk[0]
        o_h = (jnp.dot(att.astype(BF16), iv, preferred_element_type=F32)
               + jnp.dot((q * jnp.exp(b)).astype(BF16), s_h.astype(BF16),
                         preferred_element_type=F32))
        oh_scr[h] = o_h
        bl = b[chunk - 1:chunk, :]
        kd = k * jnp.exp(bl - b)
        pieces = [kd, jnp.broadcast_to(jnp.exp(bl), (SUBLANES, DK_C))]
        fill = DK_C - chunk - SUBLANES
        if fill > 0:
            pieces.append(jnp.zeros((fill, DK_C), F32))
        xt = jnp.transpose(jnp.concatenate(pieces, axis=0))
        kdt = xt[:, 0:chunk]
        ebl_col = xt[:, chunk:chunk + 1]
        s_scr[h] = ebl_col * s_h + jnp.dot(kdt.astype(BF16), iv, preferred_element_type=F32)
        return carry

    lax.fori_loop(0, H_C, head, 0)
    o = jnp.concatenate([oh_scr[h] for h in range(H_C)], axis=1)
    o_ref[0] = _rms(o, gn_ref[...]) * jax.nn.silu(g_ref[0])
    sout_ref[0] = s_scr[...]


def _hgrn(pc3, s0, lbc, gn, *, layer, chunk, valid):
    nb, t, _ = pc3.shape
    col = lambda j: pl.BlockSpec((1, chunk, F_C), lambda b, c, j=j: (b, c, j))
    st = pl.BlockSpec((1, H_C, DK_C, DK_C), lambda b, c: (b, 0, 0, 0))
    kern = functools.partial(_hgrn_kernel, layer=layer, chunk=chunk, valid=valid)
    hs = lambda: pltpu.VMEM((H_C, chunk, DK_C), F32)
    return pl.pallas_call(
        kern,
        grid=(nb, t // chunk),
        in_specs=[col(0), col(1), col(2), col(3), st,
                  pl.BlockSpec(lbc.shape, lambda b, c: (0, 0)),
                  pl.BlockSpec((1, D_MODEL), lambda b, c: (0, 0))],
        out_specs=[pl.BlockSpec((1, chunk, D_MODEL), lambda b, c: (b, c, 0)), st],
        out_shape=[jax.ShapeDtypeStruct((nb, t, D_MODEL), F32),
                   jax.ShapeDtypeStruct(s0.shape, F32)],
        scratch_shapes=[pltpu.VMEM((H_C, DK_C, DK_C), F32), hs(), hs(), hs(), hs(), hs()],
        compiler_params=_cparams(("parallel", "arbitrary")),
        name="hgrn",
    )(pc3, pc3, pc3, pc3, s0, lbc, gn.reshape(1, D_MODEL))


def _block_diag(w):
    h, n, _ = w.shape
    eye = jnp.eye(h, dtype=w.dtype)
    return (eye[:, None, :, None] * w[:, :, None, :]).reshape(h * n, h * n)


def _fold(x):
    half = x.shape[-2] // 2
    return jnp.concatenate([x[..., :half, :], x[..., half:, :]], axis=-1)


def _unfold(x):
    half = x.shape[-1] // 2
    return jnp.concatenate([x[..., :half], x[..., half:]], axis=-2)


def _trunk(x2d, conv_st, h_st, shift_st, rs_st, hs_st, P, *, nbatch, nsteps, time_major):
    n = x2d.shape[0]
    if time_major:
        nseq, step, rows = 1, nbatch, n
        tm = n
    else:
        nseq, step, rows = nbatch, 1, 256
        tm = 512
    fold = not time_major
    nchain = nbatch * H_B

    pa, pb = _norm_matmul(x2d, P['ln_mix'][0], P['w_in_ab'], (P_A, P_B), tm)
    out_a, conv_new, h_new = _rglru(
        pa, conv_st, h_st, P['conv_w'], P['conv_b'], P['wr'], P['wi'], P['gr_b'], P['gi_b'],
        P['lru_lambda'], nseq=nseq, step=step, rows=rows)
    r, k, v, dec, a, g, shift_new = _rwkv_prep(
        pb, shift_st, P['mu_b'], P['w0_b'], P['a0_b'], P['w2p'], P['a2p'], P['g2_b'],
        nseq=nseq, step=step, rows=rows)

    def to_lanes(z):
        if time_major:
            z = z.reshape(nsteps, nbatch, H_B, HD_B).transpose(0, 3, 1, 2)
        else:
            z = z.reshape(nbatch, nsteps, H_B, HD_B).transpose(1, 3, 0, 2)
        return z.reshape(nsteps, HD_B, nchain)

    def params_to_lanes(p):
        return jnp.tile(p.reshape(H_B, HD_B).T, (1, nbatch))

    dup = (lambda z: jnp.concatenate([z, z], axis=-1)) if fold else (lambda z: z)
    kfold = _fold if fold else (lambda z: z)
    s0 = rs_st.transpose(2, 3, 0, 1).reshape(HD_B, HD_B, nchain)
    yb, s_new = _rwkv_scan(
        kfold(to_lanes(r)), kfold(to_lanes(k)), dup(to_lanes(v)), kfold(to_lanes(dec)),
        kfold(to_lanes(a)), kfold(s0),
        kfold(params_to_lanes(P['kk_b'])), kfold(params_to_lanes(P['ka_b'])),
        kfold(params_to_lanes(P['rk_b'])), dup(params_to_lanes(P['lnx_w'])),
        dup(params_to_lanes(P['lnx_b'])),
        tb=min(nsteps, 64), fold=fold)
    if fold:
        yb = yb[..., :nchain]
        s_new = _unfold(s_new)
    yb = yb.reshape(nsteps, HD_B, nbatch, H_B)
    yb = yb.transpose(0, 2, 3, 1) if time_major else yb.transpose(2, 0, 3, 1)
    yb = yb.reshape(n, W_B)
    rs_new = s_new.reshape(HD_B, HD_B, nbatch, H_B).transpose(2, 3, 0, 1)

    x1 = _mix_out(x2d, out_a, yb, g, P['w_out_a'], P['w_out_b'], tm)
    x2 = _ffn(x1, P['ln_ffn'][0], P['ffn_gate'][0], P['ffn_up'][0], P['ffn_down'][0],
              P['ln_final'], tm=tm, tf=D_FF // 2, final_norm=False)

    (pc,) = _norm_matmul(x2, P['ln_mix'][1], P['w_in_c'], (4 * F_C,), tm)
    if time_major:
        pc3 = pc.reshape(nsteps, nbatch, 4 * F_C).transpose(1, 0, 2)
        chunk = SUBLANES
        pc3 = jnp.pad(pc3, ((0, 0), (0, chunk - nsteps), (0, 0)))
    else:
        pc3 = pc.reshape(nbatch, nsteps, 4 * F_C)
        chunk = 64
    o3, hs_new = _hgrn(pc3, hs_st, P['lb_c'], P['gn_c'], layer=1, chunk=chunk,
                       valid=min(chunk, nsteps))
    if time_major:
        o = o3[:, :nsteps].transpose(1, 0, 2).reshape(n, D_MODEL)
    else:
        o = o3.reshape(n, D_MODEL)
    x3 = _proj_res(x2, o, P['w_out_c'], tm)
    y = _ffn(x3, P['ln_ffn'][1], P['ffn_gate'][1], P['ffn_up'][1], P['ffn_down'][1],
             P['ln_final'], tm=tm, tf=D_FF // 2, final_norm=True)
    return y, conv_new, h_new, shift_new, rs_new, hs_new


def kernel(x_prompt, x_sample, state_rglru_conv, state_rglru_h, state_rwkv_shift, state_rwkv_S,
           state_hgrn_S, ln_mix, ln_ffn, ln_final, w_in_ab, conv_w, conv_b, gr_w, gr_b, gi_w,
           gi_b, lru_lambda, mu_b, w0_b, w2_b, a0_b, a2_b, g2_b, kk_b, ka_b, rk_b, lnx_w, lnx_b,
           w_out_ab, w_in_c, lb_c, gn_c, w_out_c, ffn_gate, ffn_up, ffn_down):
    bp, tp, _ = x_prompt.shape
    bs, ts, _ = x_sample.shape
    zpad_w = jnp.zeros((LORA_A, W_B), F32)
    zpad_a = jnp.zeros((LORA_W, W_B), F32)
    P = dict(
        ln_mix=ln_mix, ln_ffn=ln_ffn, ln_final=ln_final,
        w_in_ab=w_in_ab[0].astype(BF16), conv_w=conv_w[0], conv_b=conv_b[0],
        wr=_block_diag(gr_w[0]).astype(BF16), wi=_block_diag(gi_w[0]).astype(BF16),
        gr_b=gr_b[0], gi_b=gi_b[0], lru_lambda=lru_lambda[0], mu_b=mu_b[0], w0_b=w0_b[0],
        a0_b=a0_b[0],
        w2p=jnp.concatenate([w2_b[0], zpad_w], axis=0).astype(BF16),
        a2p=jnp.concatenate([zpad_a, a2_b[0]], axis=0).astype(BF16),
        g2_b=g2_b[0].astype(BF16), kk_b=kk_b[0], ka_b=ka_b[0], rk_b=rk_b[0],
        lnx_w=lnx_w[0], lnx_b=lnx_b[0],
        w_out_a=w_out_ab[0, :W_A].astype(BF16), w_out_b=w_out_ab[0, W_A:].astype(BF16),
        w_in_c=w_in_c[0].astype(BF16), lb_c=lb_c, gn_c=gn_c[0],
        w_out_c=w_out_c[0].astype(BF16), ffn_gate=ffn_gate.astype(BF16),
        ffn_up=ffn_up.astype(BF16), ffn_down=ffn_down.astype(BF16))

    yp, p_conv, p_h, p_shift, p_rs, p_hs = _trunk(
        x_prompt.reshape(bp * tp, D_MODEL),
        jnp.zeros((bp, CONV_W - 1, W_A), F32), jnp.zeros((bp, 1, W_A), F32),
        jnp.zeros((bp, 1, P_B), F32), jnp.zeros((bp, H_B, HD_B, HD_B), F32),
        jnp.zeros((bp, H_C, DK_C, DK_C), F32), P, nbatch=bp, nsteps=tp, time_major=False)

    ys, s_conv, s_h, s_shift, s_rs, s_hs = _trunk(
        x_sample.transpose(1, 0, 2).reshape(ts * bs, D_MODEL),
        state_rglru_conv[0].transpose(1, 0, 2).reshape(1, (CONV_W - 1) * bs, W_A),
        state_rglru_h[0].reshape(1, bs, W_A), state_rwkv_shift[0].reshape(1, bs, P_B),
        state_rwkv_S[0], state_hgrn_S[0], P, nbatch=bs, nsteps=ts, time_major=True)

    return (yp.reshape(bp, tp, D_MODEL),
            ys.reshape(ts, bs, D_MODEL).transpose(1, 0, 2),
            p_conv[None], p_h.reshape(1, bp, W_A), p_shift.reshape(1, bp, P_B), p_rs[None],
            p_hs[None],
            s_conv.reshape(CONV_W - 1, bs, W_A).transpose(1, 0, 2)[None],
            s_h.reshape(1, bs, W_A), s_shift.reshape(1, bs, P_B), s_rs[None], s_hs[None])
```

```python
import functools

import jax
import jax.numpy as jnp
from jax import lax
from jax.experimental import pallas as pl
from jax.experimental.pallas import tpu as pltpu

F32 = jnp.float32
BF16 = jnp.bfloat16

D_MODEL = 1024
W_A = 512
H_A = 8
CONV_W = 4
LRU_C = 8.0
W_B = 512
HD_B = 64
H_B = 8
LORA_W = 64
LORA_A = 64
LORA_G = 128
P_A = 2 * W_A
P_B = 3 * W_B + LORA_W + LORA_A + LORA_G
DK_C = 128
H_C = 8
F_C = 1024
D_FF = 2816
RMS_EPS = 1e-6
GN_EPS = 64e-5

LANES = 128
SUBLANES = 8
VMEM_LIMIT = 56 * 1024 * 1024


def _cparams(sem):
    return pltpu.CompilerParams(dimension_semantics=sem, vmem_limit_bytes=VMEM_LIMIT)


def _softplus(x):
    return jnp.maximum(x, 0.0) + jnp.log1p(jnp.exp(-jnp.abs(x)))


def _sigmoid(x):
    return 1.0 / (1.0 + jnp.exp(-x))


def _rms(x, g):
    ms = jnp.mean(x * x, axis=-1, keepdims=True)
    return x * lax.rsqrt(ms + RMS_EPS) * g


def _dot(a, b):
    return jnp.dot(a.astype(BF16), b.astype(BF16), preferred_element_type=F32)


def _norm_matmul_kernel(x_ref, g_ref, w_ref, *o_refs):
    xn = _rms(x_ref[...], g_ref[...]).astype(BF16)
    off = 0
    for o_ref in o_refs:
        n = o_ref.shape[-1]
        o_ref[...] = jnp.dot(xn, w_ref[:, off:off + n], preferred_element_type=F32)
        off += n


def _norm_matmul(x, g, w, splits, tm):
    n, d = x.shape
    p = w.shape[1]
    assert sum(splits) == p and n % tm == 0
    return pl.pallas_call(
        _norm_matmul_kernel,
        grid=(n // tm,),
        in_specs=[pl.BlockSpec((tm, d), lambda i: (i, 0)),
                  pl.BlockSpec((1, d), lambda i: (0, 0)),
                  pl.BlockSpec((d, p), lambda i: (0, 0))],
        out_specs=[pl.BlockSpec((tm, s), lambda i: (i, 0)) for s in splits],
        out_shape=[jax.ShapeDtypeStruct((n, s), F32) for s in splits],
        compiler_params=_cparams(("parallel",)),
        name="norm_matmul",
    )(x, g.reshape(1, d), w)


def _rglru_kernel(u_ref, gate_ref, cst_ref, h0_ref, cw_ref, cb_ref, wr_ref, wi_ref, br_ref,
                  bi_ref, lam_ref, out_ref, cnew_ref, hnew_ref, xbuf, hcar, bbuf,
                  *, step, rows, pad):
    t = pl.program_id(1)
    hist = (CONV_W - 1) * step

    @pl.when(t == 0)
    def _():
        xbuf[pad - hist:pad, :] = cst_ref[0]
        hcar[...] = h0_ref[0]

    xbuf[pad:pad + rows, :] = u_ref[...]
    conv = cb_ref[...] + xbuf[pad - hist:pad - hist + rows, :] * cw_ref[0:1, :]
    for j in range(1, CONV_W):
        o = pad - hist + j * step
        conv = conv + xbuf[o:o + rows, :] * cw_ref[j:j + 1, :]
    tail = xbuf[pad + rows - hist:pad + rows, :]
    cnew_ref[0] = tail
    xbuf[pad - hist:pad, :] = tail

    ub = conv.astype(BF16)
    r = _sigmoid(jnp.dot(ub, wr_ref[...], preferred_element_type=F32) + br_ref[...])
    ig = _sigmoid(jnp.dot(ub, wi_ref[...], preferred_element_type=F32) + bi_ref[...])
    log_a = (-LRU_C) * r * _softplus(-lam_ref[...])
    a = jnp.exp(log_a)
    bterm = jnp.sqrt(-jnp.tanh(log_a) * (a * a + 1.0)) * (ig * conv)
    bbuf[...] = bterm
    bbuf[0:step, :] = bterm[0:step, :] + a[0:step, :] * hcar[...]
    bv = bbuf[...]

    row = lax.broadcasted_iota(jnp.int32, (rows, W_A), 0)
    d = step
    while d < rows:
        a_sh = pltpu.roll(a, d, 0)
        b_sh = pltpu.roll(bv, d, 0)
        m = row >= d
        bv = jnp.where(m, a * b_sh + bv, bv)
        a = jnp.where(m, a * a_sh, a)
        d *= 2
    hcar[...] = bv[rows - step:rows, :]
    hnew_ref[0] = bv[rows - step:rows, :]
    out_ref[...] = jax.nn.gelu(gate_ref[...]) * bv


def _rglru(pa, conv_st, h_st, cw, cb, wr, wi, br, bi, lam, *, nseq, step, rows):
    n = pa.shape[0]
    nt = n // (nseq * rows)
    hist = (CONV_W - 1) * step
    pad = max(SUBLANES, hist)
    row_map = lambda b, t: (b * nt + t, 0)
    vec = lambda: pl.BlockSpec((1, W_A), lambda b, t: (0, 0))
    kern = functools.partial(_rglru_kernel, step=step, rows=rows, pad=pad)
    return pl.pallas_call(
        kern,
        grid=(nseq, nt),
        in_specs=[pl.BlockSpec((rows, W_A), row_map),
                  pl.BlockSpec((rows, W_A), lambda b, t: (b * nt + t, 1)),
                  pl.BlockSpec((1, hist, W_A), lambda b, t: (b, 0, 0)),
                  pl.BlockSpec((1, step, W_A), lambda b, t: (b, 0, 0)),
                  pl.BlockSpec((CONV_W, W_A), lambda b, t: (0, 0)),
                  vec(),
                  pl.BlockSpec((W_A, W_A), lambda b, t: (0, 0)),
                  pl.BlockSpec((W_A, W_A), lambda b, t: (0, 0)),
                  vec(), vec(), vec()],
        out_specs=[pl.BlockSpec((rows, W_A), row_map),
                   pl.BlockSpec((1, hist, W_A), lambda b, t: (b, 0, 0)),
                   pl.BlockSpec((1, step, W_A), lambda b, t: (b, 0, 0))],
        out_shape=[jax.ShapeDtypeStruct((n, W_A), F32),
                   jax.ShapeDtypeStruct((nseq, hist, W_A), F32),
                   jax.ShapeDtypeStruct((nseq, step, W_A), F32)],
        scratch_shapes=[pltpu.VMEM((pad + rows, W_A), F32),
                        pltpu.VMEM((step, W_A), F32),
                        pltpu.VMEM((rows, W_A), F32)],
        compiler_params=_cparams(("parallel", "arbitrary")),
        name="rglru",
    )(pa, pa, conv_st, h_st, cw, cb.reshape(1, W_A), wr, wi, br.reshape(1, W_A),
      bi.reshape(1, W_A), lam.reshape(1, W_A))


def _rwkv_prep_kernel(pb_ref, prev_ref, mu_ref, w0_ref, a0_ref, w2_ref, a2_ref, g2_ref,
                      r_ref, k_ref, v_ref, dec_ref, a_ref, g_ref, snew_ref, pbuf,
                      *, step, rows, pad):
    t = pl.program_id(1)

    @pl.when(t == 0)
    def _():
        pbuf[pad - step:pad, :] = prev_ref[0]

    pf = pb_ref[...]
    pbuf[pad:pad + rows, :] = pf
    shifted = pbuf[pad - step:pad - step + rows, :]
    last = pf[rows - step:rows, :]
    pbuf[pad - step:pad, :] = last
    snew_ref[0] = last

    m = pf + (shifted - pf) * mu_ref[...]
    o1 = 3 * W_B
    r_ref[...] = m[:, 0:W_B]
    k_ref[...] = m[:, W_B:2 * W_B]
    v_ref[...] = m[:, 2 * W_B:o1]
    xwa = m[:, o1:o1 + LORA_W + LORA_A]
    xg = m[:, o1 + LORA_W + LORA_A:]
    lane = lax.broadcasted_iota(jnp.int32, xwa.shape, 1)
    lhs = jnp.where(lane < LORA_W, jnp.tanh(xwa), xwa).astype(BF16)
    lw = jnp.dot(lhs, w2_ref[...], preferred_element_type=F32)
    la = jnp.dot(lhs, a2_ref[...], preferred_element_type=F32)
    w_log = -_softplus(-(w0_ref[...] + lw)) - 0.5
    dec_ref[...] = jnp.exp(-jnp.exp(w_log))
    a_ref[...] = _sigmoid(a0_ref[...] + la)
    g_ref[...] = jnp.dot(_sigmoid(xg).astype(BF16), g2_ref[...], preferred_element_type=F32)


def _rwkv_prep(pb, prev, mu, w0, a0, w2p, a2p, g2, *, nseq, step, rows):
    n = pb.shape[0]
    nt = n // (nseq * rows)
    pad = max(SUBLANES, step)
    row_map = lambda b, t: (b * nt + t, 0)
    const = lambda shape: pl.BlockSpec(shape, lambda b, t: (0,) * len(shape))
    kern = functools.partial(_rwkv_prep_kernel, step=step, rows=rows, pad=pad)
    outs = pl.pallas_call(
        kern,
        grid=(nseq, nt),
        in_specs=[pl.BlockSpec((rows, P_B), row_map),
                  pl.BlockSpec((1, step, P_B), lambda b, t: (b, 0, 0)),
                  const((1, P_B)), const((1, W_B)), const((1, W_B)),
                  const((LORA_W + LORA_A, W_B)), const((LORA_W + LORA_A, W_B)),
                  const((LORA_G, W_B))],
        out_specs=[pl.BlockSpec((rows, W_B), row_map)] * 6
                  + [pl.BlockSpec((1, step, P_B), lambda b, t: (b, 0, 0))],
        out_shape=[jax.ShapeDtypeStruct((n, W_B), F32)] * 6
                  + [jax.ShapeDtypeStruct((nseq, step, P_B), F32)],
        scratch_shapes=[pltpu.VMEM((pad + rows, P_B), F32)],
        compiler_params=_cparams(("parallel", "arbitrary")),
        name="rwkv_prep",
    )(pb, prev, mu.reshape(1, P_B), w0.reshape(1, W_B), a0.reshape(1, W_B), w2p, a2p, g2)
    return outs


VGROUP = 4


def _rwkv_scan_kernel(r_ref, k_ref, v_ref, dec_ref, a_ref, s0_ref, kkp_ref, kap_ref, rkp_ref,
                      lw_ref, lb_ref, y_ref, sout_ref, s_scr, kk_scr, km_scr, b_scr, o_scr,
                      *, tb, nvr, fold):
    tblk = pl.program_id(1)

    @pl.when(tblk == 0)
    def _():
        s_scr[...] = s0_ref[...]

    def colsum(x):
        return jnp.sum(x, axis=0, keepdims=True)

    def vsum(x):
        s = colsum(x)
        if fold:
            s = s + pltpu.roll(s, LANES // 2, 1)
        return s

    kkp = kkp_ref[...]
    kap = kap_ref[...]

    def prep(t, c):
        kraw = k_ref[t]
        a = a_ref[t]
        kk = kraw * kkp
        kk = kk * lax.rsqrt(jnp.maximum(colsum(kk * kk), 1e-24))
        kk_scr[t] = kk
        km_scr[t] = kraw * (1.0 + (a - 1.0) * kap)
        b_scr[t] = kk * a
        return c

    lax.fori_loop(0, tb, prep, 0, unroll=2)

    sub = lax.broadcasted_iota(jnp.int32, (VGROUP, LANES), 0)
    for g in range(nvr // VGROUP):
        vbase = g * VGROUP

        def step(t, state, vbase=vbase):
            kk = kk_scr[t]
            w = dec_ref[t]
            bvec = b_scr[t]
            km = km_scr[t]
            r = r_ref[t]
            vt = v_ref[t, vbase:vbase + VGROUP, :]
            new = []
            otile = jnp.zeros((VGROUP, LANES), F32)
            for i in range(VGROUP):
                sk = colsum(state[i] * kk)
                sv = state[i] * w - sk * bvec + vt[i:i + 1, :] * km
                new.append(sv)
                otile = jnp.where(sub == i, colsum(sv * r), otile)
            o_scr[t, vbase:vbase + VGROUP, :] = otile
            return tuple(new)

        state = tuple(s_scr[vbase + i] for i in range(VGROUP))
        state = lax.fori_loop(0, tb, step, state, unroll=2)
        for i in range(VGROUP):
            s_scr[vbase + i] = state[i]

    rkp = rkp_ref[...]
    lw = lw_ref[...]
    lb = lb_ref[...]

    def post(t, c):
        o = o_scr[t]
        mean = vsum(o) * (1.0 / HD_B)
        cen = o - mean
        var = vsum(cen * cen) * (1.0 / HD_B)
        on = cen * lax.rsqrt(var + GN_EPS) * lw + lb
        bonus = colsum(r_ref[t] * km_scr[t] * rkp) * v_ref[t]
        y_ref[t] = on + bonus
        return c

    lax.fori_loop(0, tb, post, 0, unroll=4)
    sout_ref[...] = s_scr[...]


def _rwkv_scan(r, k, v, dec, a, s0, kkp, kap, rkp, lw, lb, *, tb, fold):
    nt, _, nl = r.shape
    nvr = v.shape[1]
    lg = nl // LANES
    seq = lambda rows: pl.BlockSpec((tb, rows, LANES), lambda g, t: (t, 0, g))
    par = lambda rows: pl.BlockSpec((rows, LANES), lambda g, t: (0, g))
    st = pl.BlockSpec((nvr, HD_B, LANES), lambda g, t: (0, 0, g))
    kern = functools.partial(_rwkv_scan_kernel, tb=tb, nvr=nvr, fold=fold)
    tile = lambda rows: pltpu.VMEM((tb, rows, LANES), F32)
    return pl.pallas_call(
        kern,
        grid=(lg, nt // tb),
        in_specs=[seq(HD_B), seq(HD_B), seq(nvr), seq(HD_B), seq(HD_B), st,
                  par(HD_B), par(HD_B), par(HD_B), par(nvr), par(nvr)],
        out_specs=[seq(nvr), st],
        out_shape=[jax.ShapeDtypeStruct((nt, nvr, nl), F32),
                   jax.ShapeDtypeStruct((nvr, HD_B, nl), F32)],
        scratch_shapes=[pltpu.VMEM((nvr, HD_B, LANES), F32),
                        tile(HD_B), tile(HD_B), tile(HD_B), tile(nvr)],
        compiler_params=_cparams(("parallel", "arbitrary")),
        name="rwkv_scan",
    )(r, k, v, dec, a, s0, kkp, kap, rkp, lw, lb)


def _mix_out_kernel(x_ref, oa_ref, yb_ref, g_ref, wa_ref, wb_ref, o_ref):
    ob = yb_ref[...] * g_ref[...]
    o_ref[...] = (x_ref[...]
                  + jnp.dot(oa_ref[...].astype(BF16), wa_ref[...], preferred_element_type=F32)
                  + jnp.dot(ob.astype(BF16), wb_ref[...], preferred_element_type=F32))


def _mix_out(x, oa, yb, g, wa, wb, tm):
    n, d = x.shape
    row = lambda w: pl.BlockSpec((tm, w), lambda i: (i, 0))
    return pl.pallas_call(
        _mix_out_kernel,
        grid=(n // tm,),
        in_specs=[row(d), row(W_A), row(W_B), row(W_B),
                  pl.BlockSpec((W_A, d), lambda i: (0, 0)),
                  pl.BlockSpec((W_B, d), lambda i: (0, 0))],
        out_specs=row(d),
        out_shape=jax.ShapeDtypeStruct((n, d), F32),
        compiler_params=_cparams(("parallel",)),
        name="mix_out",
    )(x, oa, yb, g, wa, wb)


def _proj_res_kernel(x_ref, o_ref, w_ref, y_ref):
    y_ref[...] = x_ref[...] + jnp.dot(o_ref[...].astype(BF16), w_ref[...],
                                      preferred_element_type=F32)


def _proj_res(x, o, w, tm):
    n, d = x.shape
    row = pl.BlockSpec((tm, d), lambda i: (i, 0))
    return pl.pallas_call(
        _proj_res_kernel,
        grid=(n // tm,),
        in_specs=[row, row, pl.BlockSpec((d, d), lambda i: (0, 0))],
        out_specs=row,
        out_shape=jax.ShapeDtypeStruct((n, d), F32),
        compiler_params=_cparams(("parallel",)),
        name="proj_res",
    )(x, o, w)


def _ffn_kernel(x_ref, g_ref, wg_ref, wu_ref, wd_ref, gf_ref, o_ref, xf_scr, acc_scr,
                *, final_norm):
    j = pl.program_id(1)

    @pl.when(j == 0)
    def _():
        x = x_ref[...]
        xf_scr[...] = _rms(x, g_ref[...]).astype(BF16)
        acc_scr[...] = x

    xf = xf_scr[...]
    hg = jnp.dot(xf, wg_ref[...], preferred_element_type=F32)
    hu = jnp.dot(xf, wu_ref[...], preferred_element_type=F32)
    h = (hg * _sigmoid(hg)) * hu
    acc_scr[...] += jnp.dot(h.astype(BF16), wd_ref[...], preferred_element_type=F32)

    @pl.when(j == pl.num_programs(1) - 1)
    def _():
        y = acc_scr[...]
        if final_norm:
            y = _rms(y, gf_ref[...])
        o_ref[...] = y


def _ffn(x, g, wg, wu, wd, gf, *, tm, tf, final_norm):
    n, d = x.shape
    ff = wg.shape[1]
    kern = functools.partial(_ffn_kernel, final_norm=final_norm)
    return pl.pallas_call(
        kern,
        grid=(n // tm, ff // tf),
        in_specs=[pl.BlockSpec((tm, d), lambda i, j: (i, 0)),
                  pl.BlockSpec((1, d), lambda i, j: (0, 0)),
                  pl.BlockSpec((d, tf), lambda i, j: (0, j)),
                  pl.BlockSpec((d, tf), lambda i, j: (0, j)),
                  pl.BlockSpec((tf, d), lambda i, j: (j, 0)),
                  pl.BlockSpec((1, d), lambda i, j: (0, 0))],
        out_specs=pl.BlockSpec((tm, d), lambda i, j: (i, 0)),
        out_shape=jax.ShapeDtypeStruct((n, d), F32),
        scratch_shapes=[pltpu.VMEM((tm, d), BF16), pltpu.VMEM((tm, d), F32)],
        compiler_params=_cparams(("parallel", "arbitrary")),
        name="ffn",
    )(x, g.reshape(1, d), wg, wu, wd, gf.reshape(1, d))


def _hgrn_kernel(q_ref, f_ref, i_ref, g_ref, s0_ref, lbc_ref, gn_ref, o_ref, sout_ref,
                 s_scr, q_scr, k_scr, bc_scr, i_scr, oh_scr, *, layer, chunk, valid):
    c = pl.program_id(1)

    @pl.when(c == 0)
    def _():
        s_scr[...] = s0_ref[0]

    lbc = lbc_ref[...]
    e = jnp.exp(lbc - jnp.max(lbc, axis=0, keepdims=True))
    sm = e / jnp.sum(e, axis=0, keepdims=True)
    lbf = sm[1:2, :]
    for l in range(2, layer + 1):
        lbf = lbf + sm[l:l + 1, :]

    fr = f_ref[0]
    ez = jnp.exp(-jnp.abs(fr))
    rz = 1.0 / (1.0 + ez)
    sig = jnp.where(fr >= 0.0, rz, ez * rz)
    nsig = jnp.where(fr >= 0.0, ez * rz, rz)
    logf = jnp.log(lbf + (1.0 - lbf) * sig)
    kfull = (1.0 - lbf) * nsig
    if valid < chunk:
        rowv = lax.broadcasted_iota(jnp.int32, logf.shape, 0)
        logf = jnp.where(rowv < valid, logf, 0.0)
        kfull = jnp.where(rowv < valid, kfull, 0.0)
    qfull = jax.nn.silu(q_ref[0])

    bc = logf
    rowc = lax.broadcasted_iota(jnp.int32, bc.shape, 0)
    d = 1
    while d < chunk:
        bc = bc + jnp.where(rowc >= d, pltpu.roll(bc, d, 0), 0.0)
        d *= 2

    ivfull = i_ref[0]
    for h in range(H_C):
        sl = slice(h * DK_C, (h + 1) * DK_C)
        q_scr[h] = qfull[:, sl]
        k_scr[h] = kfull[:, sl]
        bc_scr[h] = bc[:, sl]
        i_scr[h] = ivfull[:, sl]

    nblk = chunk // SUBLANES
    sub = lax.broadcasted_iota(jnp.int32, (SUBLANES, DK_C), 0)
    lane_c = lax.broadcasted_iota(jnp.int32, (SUBLANES, chunk), 1)

    def head(h, carry):
        q = q_scr[h]
        k = k_scr[h]
        b = bc_scr[h]
        iv = i_scr[h].astype(BF16)
        s_h = s_scr[h]
        ablk = []
        for tb in range(nblk):
            rs = slice(tb * SUBLANES, (tb + 1) * SUBLANES)
            blk = jnp.zeros((SUBLANES, chunk), F32)
            for j in range(SUBLANES):
                s = tb * SUBLANES + j
                if s >= valid:
                    break
                diff = jnp.where(sub >= j, b[rs, :] - b[s:s + 1, :], -jnp.inf)
                ev = jnp.exp(diff) * (q[rs, :] * k[s:s + 1, :])
                blk = jnp.where(lane_c == s, jnp.sum(ev, axis=1, keepdims=True), blk)
            if tb > 0:
                lo = tb * SUBLANES
                edge = b[lo - 1:lo, :]
                qi = q[rs, :] * jnp.exp(b[rs, :] - edge)
                ki = k[0:lo, :] * jnp.exp(edge - b[0:lo, :])
                if lo < chunk:
                    ki = jnp.concatenate([ki, jnp.zeros((chunk - lo, DK_C), F32)], axis=0)
                blk = blk + lax.dot_general(qi.astype(BF16), ki.astype(BF16),
                                            (((1,), (1,)), ((), ())),
                                            preferred_element_type=F32)
            ablk.append(blk)
        att = jnp.concatenate(ablk, axis=0) if nblk > 1 else ablk[0]
        o_h = (jnp.dot(att.astype(BF16), iv, preferred_element_type=F32)
               + jnp.dot((q * jnp.exp(b)).astype(BF16), s_h.astype(BF16),
                         preferred_element_type=F32))
        oh_scr[h] = o_h
        bl = b[chunk - 1:chunk, :]
        kd = k * jnp.exp(bl - b)
        pieces = [kd, jnp.broadcast_to(jnp.exp(bl), (SUBLANES, DK_C))]
        fill = DK_C - chunk - SUBLANES
        if fill > 0:
            pieces.append(jnp.zeros((fill, DK_C), F32))
        xt = jnp.transpose(jnp.concatenate(pieces, axis=0))
        kdt = xt[:, 0:chunk]
        ebl_col = xt[:, chunk:chunk + 1]
        s_scr[h] = ebl_col * s_h + jnp.dot(kdt.astype(BF16), iv, preferred_element_type=F32)
        return carry

    lax.fori_loop(0, H_C, head, 0, unroll=True if chunk <= SUBLANES else 4)
    o = jnp.concatenate([oh_scr[h] for h in range(H_C)], axis=1)
    o_ref[0] = _rms(o, gn_ref[...]) * jax.nn.silu(g_ref[0])
    sout_ref[0] = s_scr[...]


def _hgrn(pc3, s0, lbc, gn, *, layer, chunk, valid):
    nb, t, _ = pc3.shape
    col = lambda j: pl.BlockSpec((1, chunk, F_C), lambda b, c, j=j: (b, c, j))
    st = pl.BlockSpec((1, H_C, DK_C, DK_C), lambda b, c: (b, 0, 0, 0))
    kern = functools.partial(_hgrn_kernel, layer=layer, chunk=chunk, valid=valid)
    hs = lambda: pltpu.VMEM((H_C, chunk, DK_C), F32)
    return pl.pallas_call(
        kern,
        grid=(nb, t // chunk),
        in_specs=[col(0), col(1), col(2), col(3), st,
                  pl.BlockSpec(lbc.shape, lambda b, c: (0, 0)),
                  pl.BlockSpec((1, D_MODEL), lambda b, c: (0, 0))],
        out_specs=[pl.BlockSpec((1, chunk, D_MODEL), lambda b, c: (b, c, 0)), st],
        out_shape=[jax.ShapeDtypeStruct((nb, t, D_MODEL), F32),
                   jax.ShapeDtypeStruct(s0.shape, F32)],
        scratch_shapes=[pltpu.VMEM((H_C, DK_C, DK_C), F32), hs(), hs(), hs(), hs(), hs()],
        compiler_params=_cparams(("parallel", "arbitrary")),
        name="hgrn",
    )(pc3, pc3, pc3, pc3, s0, lbc, gn.reshape(1, D_MODEL))


def _block_diag(w):
    h, n, _ = w.shape
    eye = jnp.eye(h, dtype=w.dtype)
    return (eye[:, None, :, None] * w[:, :, None, :]).reshape(h * n, h * n)


def _fold(x):
    half = x.shape[-2] // 2
    return jnp.concatenate([x[..., :half, :], x[..., half:, :]], axis=-1)


def _unfold(x):
    half = x.shape[-1] // 2
    return jnp.concatenate([x[..., :half], x[..., half:]], axis=-2)


def _trunk(x2d, conv_st, h_st, shift_st, rs_st, hs_st, P, *, nbatch, nsteps, time_major):
    n = x2d.shape[0]
    if time_major:
        nseq, step, rows = 1, nbatch, n
        tm = n
    else:
        nseq, step, rows = nbatch, 1, 256
        tm = 512
    fold = not time_major
    nchain = nbatch * H_B

    pa, pb = _norm_matmul(x2d, P['ln_mix'][0], P['w_in_ab'], (P_A, P_B), tm)
    out_a, conv_new, h_new = _rglru(
        pa, conv_st, h_st, P['conv_w'], P['conv_b'], P['wr'], P['wi'], P['gr_b'], P['gi_b'],
        P['lru_lambda'], nseq=nseq, step=step, rows=rows)
    r, k, v, dec, a, g, shift_new = _rwkv_prep(
        pb, shift_st, P['mu_b'], P['w0_b'], P['a0_b'], P['w2p'], P['a2p'], P['g2_b'],
        nseq=nseq, step=step, rows=rows)

    def to_lanes(z):
        if time_major:
            z = z.reshape(nsteps, nbatch, H_B, HD_B).transpose(0, 3, 1, 2)
        else:
            z = z.reshape(nbatch, nsteps, H_B, HD_B).transpose(1, 3, 0, 2)
        return z.reshape(nsteps, HD_B, nchain)

    def params_to_lanes(p):
        return jnp.tile(p.reshape(H_B, HD_B).T, (1, nbatch))

    dup = (lambda z: jnp.concatenate([z, z], axis=-1)) if fold else (lambda z: z)
    vfold = _fold if fold else (lambda z: z)
    s0 = rs_st.transpose(2, 3, 0, 1).reshape(HD_B, HD_B, nchain)
    if fold:
        s0 = jnp.concatenate([s0[:HD_B // 2], s0[HD_B // 2:]], axis=-1)
    yb, s_new = _rwkv_scan(
        dup(to_lanes(r)), dup(to_lanes(k)), vfold(to_lanes(v)), dup(to_lanes(dec)),
        dup(to_lanes(a)), s0,
        dup(params_to_lanes(P['kk_b'])), dup(params_to_lanes(P['ka_b'])),
        dup(params_to_lanes(P['rk_b'])), vfold(params_to_lanes(P['lnx_w'])),
        vfold(params_to_lanes(P['lnx_b'])),
        tb=min(nsteps, 64), fold=fold)
    if fold:
        yb = _unfold(yb)
        s_new = jnp.concatenate([s_new[..., :nchain], s_new[..., nchain:]], axis=0)
    yb = yb.reshape(nsteps, HD_B, nbatch, H_B)
    yb = yb.transpose(0, 2, 3, 1) if time_major else yb.transpose(2, 0, 3, 1)
    yb = yb.reshape(n, W_B)
    rs_new = s_new.reshape(HD_B, HD_B, nbatch, H_B).transpose(2, 3, 0, 1)

    x1 = _mix_out(x2d, out_a, yb, g, P['w_out_a'], P['w_out_b'], tm)
    x2 = _ffn(x1, P['ln_ffn'][0], P['ffn_gate'][0], P['ffn_up'][0], P['ffn_down'][0],
              P['ln_final'], tm=tm, tf=D_FF // 2, final_norm=False)

    (pc,) = _norm_matmul(x2, P['ln_mix'][1], P['w_in_c'], (4 * F_C,), tm)
    if time_major:
        pc3 = pc.reshape(nsteps, nbatch, 4 * F_C).transpose(1, 0, 2)
        chunk = SUBLANES
        pc3 = jnp.pad(pc3, ((0, 0), (0, chunk - nsteps), (0, 0)))
    else:
        pc3 = pc.reshape(nbatch, nsteps, 4 * F_C)
        chunk = 64
    o3, hs_new = _hgrn(pc3, hs_st, P['lb_c'], P['gn_c'], layer=1, chunk=chunk,
                       valid=min(chunk, nsteps))
    if time_major:
        o = o3[:, :nsteps].transpose(1, 0, 2).reshape(n, D_MODEL)
    else:
        o = o3.reshape(n, D_MODEL)
    x3 = _proj_res(x2, o, P['w_out_c'], tm)
    y = _ffn(x3, P['ln_ffn'][1], P['ffn_gate'][1], P['ffn_up'][1], P['ffn_down'][1],
             P['ln_final'], tm=tm, tf=D_FF // 2, final_norm=True)
    return y, conv_new, h_new, shift_new, rs_new, hs_new


def kernel(x_prompt, x_sample, state_rglru_conv, state_rglru_h, state_rwkv_shift, state_rwkv_S,
           state_hgrn_S, ln_mix, ln_ffn, ln_final, w_in_ab, conv_w, conv_b, gr_w, gr_b, gi_w,
           gi_b, lru_lambda, mu_b, w0_b, w2_b, a0_b, a2_b, g2_b, kk_b, ka_b, rk_b, lnx_w, lnx_b,
           w_out_ab, w_in_c, lb_c, gn_c, w_out_c, ffn_gate, ffn_up, ffn_down):
    bp, tp, _ = x_prompt.shape
    bs, ts, _ = x_sample.shape
    zpad_w = jnp.zeros((LORA_A, W_B), F32)
    zpad_a = jnp.zeros((LORA_W, W_B), F32)
    P = dict(
        ln_mix=ln_mix, ln_ffn=ln_ffn, ln_final=ln_final,
        w_in_ab=w_in_ab[0].astype(BF16), conv_w=conv_w[0], conv_b=conv_b[0],
        wr=_block_diag(gr_w[0]).astype(BF16), wi=_block_diag(gi_w[0]).astype(BF16),
        gr_b=gr_b[0], gi_b=gi_b[0], lru_lambda=lru_lambda[0], mu_b=mu_b[0], w0_b=w0_b[0],
        a0_b=a0_b[0],
        w2p=jnp.concatenate([w2_b[0], zpad_w], axis=0).astype(BF16),
        a2p=jnp.concatenate([zpad_a, a2_b[0]], axis=0).astype(BF16),
        g2_b=g2_b[0].astype(BF16), kk_b=kk_b[0], ka_b=ka_b[0], rk_b=rk_b[0],
        lnx_w=lnx_w[0], lnx_b=lnx_b[0],
        w_out_a=w_out_ab[0, :W_A].astype(BF16), w_out_b=w_out_ab[0, W_A:].astype(BF16),
        w_in_c=w_in_c[0].astype(BF16), lb_c=lb_c, gn_c=gn_c[0],
        w_out_c=w_out_c[0].astype(BF16), ffn_gate=ffn_gate.astype(BF16),
        ffn_up=ffn_up.astype(BF16), ffn_down=ffn_down.astype(BF16))

    yp, p_conv, p_h, p_shift, p_rs, p_hs = _trunk(
        x_prompt.reshape(bp * tp, D_MODEL),
        jnp.zeros((bp, CONV_W - 1, W_A), F32), jnp.zeros((bp, 1, W_A), F32),
        jnp.zeros((bp, 1, P_B), F32), jnp.zeros((bp, H_B, HD_B, HD_B), F32),
        jnp.zeros((bp, H_C, DK_C, DK_C), F32), P, nbatch=bp, nsteps=tp, time_major=False)

    ys, s_conv, s_h, s_shift, s_rs, s_hs = _trunk(
        x_sample.transpose(1, 0, 2).reshape(ts * bs, D_MODEL),
        state_rglru_conv[0].transpose(1, 0, 2).reshape(1, (CONV_W - 1) * bs, W_A),
        state_rglru_h[0].reshape(1, bs, W_A), state_rwkv_shift[0].reshape(1, bs, P_B),
        state_rwkv_S[0], state_hgrn_S[0], P, nbatch=bs, nsteps=ts, time_major=True)

    return (yp.reshape(bp, tp, D_MODEL),
            ys.reshape(ts, bs, D_MODEL).transpose(1, 0, 2),
            p_conv[None], p_h.reshape(1, bp, W_A), p_shift.reshape(1, bp, P_B), p_rs[None],
            p_hs[None],
            s_conv.reshape(CONV_W - 1, bs, W_A).transpose(1, 0, 2)[None],
            s_h.reshape(1, bs, W_A), s_shift.reshape(1, bs, P_B), s_rs[None], s_hs[None])
```

```python
import functools

import jax
import jax.numpy as jnp
from jax import lax
from jax.experimental import pallas as pl
from jax.experimental.pallas import tpu as pltpu

F32 = jnp.float32
BF16 = jnp.bfloat16

D_MODEL = 1024
W_A = 512
H_A = 8
CONV_W = 4
LRU_C = 8.0
W_B = 512
HD_B = 64
H_B = 8
LORA_W = 64
LORA_A = 64
LORA_G = 128
P_A = 2 * W_A
P_B = 3 * W_B + LORA_W + LORA_A + LORA_G
DK_C = 128
H_C = 8
F_C = 1024
D_FF = 2816
RMS_EPS = 1e-6
GN_EPS = 64e-5

LANES = 128
SUBLANES = 8
VMEM_LIMIT = 56 * 1024 * 1024


def _cparams(sem):
    return pltpu.CompilerParams(dimension_semantics=sem, vmem_limit_bytes=VMEM_LIMIT)


def _softplus(x):
    return jnp.maximum(x, 0.0) + jnp.log1p(jnp.exp(-jnp.abs(x)))


def _sigmoid(x):
    return 1.0 / (1.0 + jnp.exp(-x))


def _rms(x, g):
    ms = jnp.mean(x * x, axis=-1, keepdims=True)
    return x * lax.rsqrt(ms + RMS_EPS) * g


def _dot(a, b):
    return jnp.dot(a.astype(BF16), b.astype(BF16), preferred_element_type=F32)


def _split(x):
    hi = x.astype(BF16)
    lo = (x - hi.astype(F32)).astype(BF16)
    return hi, lo


def _mm(a, b, nt=False):
    ah, al = a
    bh, bl = b
    lhs = jnp.concatenate([ah, al, ah], axis=1)
    if nt:
        rhs = jnp.concatenate([bh, bh, bl], axis=1)
        return lax.dot_general(lhs, rhs, (((1,), (1,)), ((), ())), preferred_element_type=F32)
    rhs = jnp.concatenate([bh, bh, bl], axis=0)
    return jnp.dot(lhs, rhs, preferred_element_type=F32)


def _head_ones(width):
    r = lax.broadcasted_iota(jnp.int32, (width, width), 0) // HD_B
    c = lax.broadcasted_iota(jnp.int32, (width, width), 1) // HD_B
    return jnp.where(r == c, 1.0, 0.0).astype(BF16)


def _head_sum(x, ones):
    hi, lo = _split(x)
    return jnp.dot(jnp.concatenate([hi, lo], axis=1), jnp.concatenate([ones, ones], axis=0),
                   preferred_element_type=F32)


def _norm_matmul_kernel(x_ref, g_ref, w_ref, *o_refs):
    xn = _rms(x_ref[...], g_ref[...]).astype(BF16)
    off = 0
    for o_ref in o_refs:
        n = o_ref.shape[-1]
        o_ref[...] = jnp.dot(xn, w_ref[:, off:off + n], preferred_element_type=F32)
        off += n


def _norm_matmul(x, g, w, splits, tm):
    n, d = x.shape
    p = w.shape[1]
    assert sum(splits) == p and n % tm == 0
    return pl.pallas_call(
        _norm_matmul_kernel,
        grid=(n // tm,),
        in_specs=[pl.BlockSpec((tm, d), lambda i: (i, 0)),
                  pl.BlockSpec((1, d), lambda i: (0, 0)),
                  pl.BlockSpec((d, p), lambda i: (0, 0))],
        out_specs=[pl.BlockSpec((tm, s), lambda i: (i, 0)) for s in splits],
        out_shape=[jax.ShapeDtypeStruct((n, s), F32) for s in splits],
        compiler_params=_cparams(("parallel",)),
        name="norm_matmul",
    )(x, g.reshape(1, d), w)


def _rglru_kernel(u_ref, gate_ref, cst_ref, h0_ref, cw_ref, cb_ref, wr_ref, wi_ref, br_ref,
                  bi_ref, lam_ref, out_ref, cnew_ref, hnew_ref, xbuf, hcar, bbuf,
                  *, step, rows, pad):
    t = pl.program_id(1)
    hist = (CONV_W - 1) * step

    @pl.when(t == 0)
    def _():
        xbuf[pad - hist:pad, :] = cst_ref[0]
        hcar[...] = h0_ref[0]

    xbuf[pad:pad + rows, :] = u_ref[...]
    conv = cb_ref[...] + xbuf[pad - hist:pad - hist + rows, :] * cw_ref[0:1, :]
    for j in range(1, CONV_W):
        o = pad - hist + j * step
        conv = conv + xbuf[o:o + rows, :] * cw_ref[j:j + 1, :]
    tail = xbuf[pad + rows - hist:pad + rows, :]
    cnew_ref[0] = tail
    xbuf[pad - hist:pad, :] = tail

    ub = conv.astype(BF16)
    r = _sigmoid(jnp.dot(ub, wr_ref[...], preferred_element_type=F32) + br_ref[...])
    ig = _sigmoid(jnp.dot(ub, wi_ref[...], preferred_element_type=F32) + bi_ref[...])
    log_a = (-LRU_C) * r * _softplus(-lam_ref[...])
    a = jnp.exp(log_a)
    bterm = jnp.sqrt(-jnp.tanh(log_a) * (a * a + 1.0)) * (ig * conv)
    bbuf[...] = bterm
    bbuf[0:step, :] = bterm[0:step, :] + a[0:step, :] * hcar[...]
    bv = bbuf[...]

    row = lax.broadcasted_iota(jnp.int32, (rows, W_A), 0)
    d = step
    while d < rows:
        a_sh = pltpu.roll(a, d, 0)
        b_sh = pltpu.roll(bv, d, 0)
        m = row >= d
        bv = jnp.where(m, a * b_sh + bv, bv)
        a = jnp.where(m, a * a_sh, a)
        d *= 2
    hcar[...] = bv[rows - step:rows, :]
    hnew_ref[0] = bv[rows - step:rows, :]
    out_ref[...] = jax.nn.gelu(gate_ref[...]) * bv


def _rglru(pa, conv_st, h_st, cw, cb, wr, wi, br, bi, lam, *, nseq, step, rows):
    n = pa.shape[0]
    nt = n // (nseq * rows)
    hist = (CONV_W - 1) * step
    pad = max(SUBLANES, hist)
    row_map = lambda b, t: (b * nt + t, 0)
    vec = lambda: pl.BlockSpec((1, W_A), lambda b, t: (0, 0))
    kern = functools.partial(_rglru_kernel, step=step, rows=rows, pad=pad)
    return pl.pallas_call(
        kern,
        grid=(nseq, nt),
        in_specs=[pl.BlockSpec((rows, W_A), row_map),
                  pl.BlockSpec((rows, W_A), lambda b, t: (b * nt + t, 1)),
                  pl.BlockSpec((1, hist, W_A), lambda b, t: (b, 0, 0)),
                  pl.BlockSpec((1, step, W_A), lambda b, t: (b, 0, 0)),
                  pl.BlockSpec((CONV_W, W_A), lambda b, t: (0, 0)),
                  vec(),
                  pl.BlockSpec((W_A, W_A), lambda b, t: (0, 0)),
                  pl.BlockSpec((W_A, W_A), lambda b, t: (0, 0)),
                  vec(), vec(), vec()],
        out_specs=[pl.BlockSpec((rows, W_A), row_map),
                   pl.BlockSpec((1, hist, W_A), lambda b, t: (b, 0, 0)),
                   pl.BlockSpec((1, step, W_A), lambda b, t: (b, 0, 0))],
        out_shape=[jax.ShapeDtypeStruct((n, W_A), F32),
                   jax.ShapeDtypeStruct((nseq, hist, W_A), F32),
                   jax.ShapeDtypeStruct((nseq, step, W_A), F32)],
        scratch_shapes=[pltpu.VMEM((pad + rows, W_A), F32),
                        pltpu.VMEM((step, W_A), F32),
                        pltpu.VMEM((rows, W_A), F32)],
        compiler_params=_cparams(("parallel", "arbitrary")),
        name="rglru",
    )(pa, pa, conv_st, h_st, cw, cb.reshape(1, W_A), wr, wi, br.reshape(1, W_A),
      bi.reshape(1, W_A), lam.reshape(1, W_A))


def _rwkv_prep_kernel(pb_ref, prev_ref, mu_ref, w0_ref, a0_ref, w2_ref, a2_ref, g2_ref,
                      kkp_ref, kap_ref,
                      r_ref, km_ref, v_ref, lw_ref, kk_ref, bb_ref, g_ref, snew_ref, pbuf,
                      *, step, rows, pad):
    t = pl.program_id(1)

    @pl.when(t == 0)
    def _():
        pbuf[pad - step:pad, :] = prev_ref[0]

    pf = pb_ref[...]
    pbuf[pad:pad + rows, :] = pf
    shifted = pbuf[pad - step:pad - step + rows, :]
    last = pf[rows - step:rows, :]
    pbuf[pad - step:pad, :] = last
    snew_ref[0] = last

    m = pf + (shifted - pf) * mu_ref[...]
    o1 = 3 * W_B
    r_ref[...] = m[:, 0:W_B]
    kraw = m[:, W_B:2 * W_B]
    v_ref[...] = m[:, 2 * W_B:o1]
    xwa = m[:, o1:o1 + LORA_W + LORA_A]
    xg = m[:, o1 + LORA_W + LORA_A:]
    lane = lax.broadcasted_iota(jnp.int32, xwa.shape, 1)
    lhs = jnp.where(lane < LORA_W, jnp.tanh(xwa), xwa).astype(BF16)
    lw = jnp.dot(lhs, w2_ref[...], preferred_element_type=F32)
    la = jnp.dot(lhs, a2_ref[...], preferred_element_type=F32)
    w_log = -_softplus(-(w0_ref[...] + lw)) - 0.5
    lw_ref[...] = -jnp.exp(w_log)
    a = _sigmoid(a0_ref[...] + la)
    g_ref[...] = jnp.dot(_sigmoid(xg).astype(BF16), g2_ref[...], preferred_element_type=F32)
    kk = kraw * kkp_ref[...]
    kk = kk * lax.rsqrt(jnp.maximum(_head_sum(kk * kk, _head_ones(W_B)), 1e-24))
    kk_ref[...] = kk
    bb_ref[...] = kk * a
    km_ref[...] = kraw * (1.0 + (a - 1.0) * kap_ref[...])


def _rwkv_prep(pb, prev, mu, w0, a0, w2p, a2p, g2, kkp, kap, *, nseq, step, rows):
    n = pb.shape[0]
    nt = n // (nseq * rows)
    pad = max(SUBLANES, step)
    row_map = lambda b, t: (b * nt + t, 0)
    const = lambda shape: pl.BlockSpec(shape, lambda b, t: (0,) * len(shape))
    kern = functools.partial(_rwkv_prep_kernel, step=step, rows=rows, pad=pad)
    outs = pl.pallas_call(
        kern,
        grid=(nseq, nt),
        in_specs=[pl.BlockSpec((rows, P_B), row_map),
                  pl.BlockSpec((1, step, P_B), lambda b, t: (b, 0, 0)),
                  const((1, P_B)), const((1, W_B)), const((1, W_B)),
                  const((LORA_W + LORA_A, W_B)), const((LORA_W + LORA_A, W_B)),
                  const((LORA_G, W_B)), const((1, W_B)), const((1, W_B))],
        out_specs=[pl.BlockSpec((rows, W_B), row_map)] * 7
                  + [pl.BlockSpec((1, step, P_B), lambda b, t: (b, 0, 0))],
        out_shape=[jax.ShapeDtypeStruct((n, W_B), F32)] * 7
                  + [jax.ShapeDtypeStruct((nseq, step, P_B), F32)],
        scratch_shapes=[pltpu.VMEM((pad + rows, P_B), F32)],
        compiler_params=_cparams(("parallel", "arbitrary")),
        name="rwkv_prep",
    )(pb, prev, mu.reshape(1, P_B), w0.reshape(1, W_B), a0.reshape(1, W_B), w2p, a2p, g2,
      kkp.reshape(1, W_B), kap.reshape(1, W_B))
    return outs


VGROUP = 4


def _rwkv_scan_kernel(r_ref, km_ref, v_ref, lw_ref, kk_ref, bb_ref, s0_ref, o_ref, sout_ref,
                      s_scr, *, tb):
    tblk = pl.program_id(1)

    @pl.when(tblk == 0)
    def _():
        s_scr[...] = s0_ref[...]

    def colsum(x):
        return jnp.sum(x, axis=0, keepdims=True)

    sub = lax.broadcasted_iota(jnp.int32, (VGROUP, LANES), 0)
    for g in range(HD_B // VGROUP):
        vbase = g * VGROUP

        def step(t, state, vbase=vbase):
            kk = kk_ref[t]
            w = jnp.exp(lw_ref[t])
            bvec = bb_ref[t]
            km = km_ref[t]
            r = r_ref[t]
            vt = v_ref[t, vbase:vbase + VGROUP, :]
            new = []
            otile = jnp.zeros((VGROUP, LANES), F32)
            for i in range(VGROUP):
                sk = colsum(state[i] * kk)
                sv = state[i] * w - sk * bvec + vt[i:i + 1, :] * km
                new.append(sv)
                otile = jnp.where(sub == i, colsum(sv * r), otile)
            o_ref[t, vbase:vbase + VGROUP, :] = otile
            return tuple(new)

        state = tuple(s_scr[vbase + i] for i in range(VGROUP))
        state = lax.fori_loop(0, tb, step, state, unroll=2)
        for i in range(VGROUP):
            s_scr[vbase + i] = state[i]

    sout_ref[...] = s_scr[...]


def _rwkv_scan(r, km, v, lw, kk, bb, s0, *, tb):
    nt, _, nl = r.shape
    seq = pl.BlockSpec((tb, HD_B, LANES), lambda g, t: (t, 0, g))
    st = pl.BlockSpec((HD_B, HD_B, LANES), lambda g, t: (0, 0, g))
    kern = functools.partial(_rwkv_scan_kernel, tb=tb)
    return pl.pallas_call(
        kern,
        grid=(nl // LANES, nt // tb),
        in_specs=[seq] * 6 + [st],
        out_specs=[seq, st],
        out_shape=[jax.ShapeDtypeStruct((nt, HD_B, nl), F32),
                   jax.ShapeDtypeStruct((HD_B, HD_B, nl), F32)],
        scratch_shapes=[pltpu.VMEM((HD_B, HD_B, LANES), F32)],
        compiler_params=_cparams(("parallel", "arbitrary")),
        name="rwkv_scan",
    )(r, km, v, lw, kk, bb, s0)


PAIR = 2 * HD_B
RWKV_SEQS_PER_STEP = 2


def _rwkv_chunk_kernel(r_ref, km_ref, v_ref, lw_ref, kk_ref, bb_ref, s0_ref, o_ref, sout_ref,
                       s_scr, *, chunk, nb):
    c = pl.program_id(1)

    @pl.when(c == 0)
    def _():
        s_scr[...] = s0_ref[...]

    lane = lax.broadcasted_iota(jnp.int32, (chunk, PAIR), 1)
    row = lax.broadcasted_iota(jnp.int32, (chunk, PAIR), 0)
    low_half = lane < HD_B
    li = jnp.where(low_half, lane, lane - HD_B)
    strict = li < row
    incl = li <= row
    eye = jnp.where(li == row, 1.0, 0.0)
    same = lambda n: (li // n) == (row // n)
    m8 = same(8)
    levels = []
    n = 16
    prev = m8
    while n <= chunk:
        cur = same(n)
        levels.append(jnp.logical_and(cur, jnp.logical_not(prev)))
        prev = cur
        n *= 2
    sq_r = lax.broadcasted_iota(jnp.int32, (PAIR, PAIR), 0)
    sq_c = lax.broadcasted_iota(jnp.int32, (PAIR, PAIR), 1)
    same_head = (sq_r < HD_B) == (sq_c < HD_B)
    low_bf = lax.broadcasted_iota(jnp.int32, (chunk, PAIR), 1) < HD_B

    def bd(s):
        return tuple(jnp.concatenate([jnp.where(low_bf, x, jnp.zeros_like(x)),
                                      jnp.where(low_bf, jnp.zeros_like(x), x)], axis=0)
                     for x in s)

    pm = lambda x, y: _mm(_split(x), bd(_split(y)))

    chains = [(b, p) for b in range(nb) for p in range(H_B // 2)]
    each = lambda f: [f(i) for i in range(len(chains))]

    vv, lhs2, kg, bg, kdbd, etot, s_p = [], [], [], [], [], [], []
    for b, p in chains:
        sl = slice(p * PAIR, (p + 1) * PAIR)
        r = r_ref[b, :, sl]
        km = km_ref[b, :, sl]
        lw = lw_ref[b, :, sl]
        kk = kk_ref[b, :, sl]
        bb = bb_ref[b, :, sl]
        cum = lw
        d = 1
        while d < chunk:
            cum = cum + jnp.where(row >= d, pltpu.roll(cum, d, 0), 0.0)
            d *= 2
        tot = cum[chunk - 1:chunk, :]
        g_end = jnp.exp(tot - cum)
        g_inv = jnp.exp(-cum)
        vv.append(v_ref[b, :, sl])
        lhs2.append(_split(jnp.concatenate([kk * jnp.exp(cum - lw), r * jnp.exp(cum)], axis=0)))
        kg.append(bd(_split(km * g_inv)))
        bg.append(bd(_split(bb * g_inv)))
        kdbd.append(_split(jnp.concatenate([km * g_end, bb * g_end], axis=0)))
        etot.append(jnp.exp(tot))
        s_p.append(s_scr[b, p])

    ss = each(lambda i: _mm(lhs2[i], _split(s_p[i]), nt=True))
    akbk = each(lambda i: _mm(lhs2[i], kg[i], nt=True))
    abbb = each(lambda i: _mm(lhs2[i], bg[i], nt=True))
    a_b = each(lambda i: jnp.where(strict, abbb[i][:chunk], 0.0))
    b_b = each(lambda i: jnp.where(incl, abbb[i][chunk:], 0.0))
    av = each(lambda i: _mm(_split(jnp.concatenate(
        [jnp.where(strict, akbk[i][:chunk], 0.0), jnp.where(incl, akbk[i][chunk:], 0.0)],
        axis=0)), bd(_split(vv[i]))))

    dg = each(lambda i: jnp.where(m8, a_b[i], 0.0))
    dg_s = each(lambda i: _split(dg[i]))
    d2 = each(lambda i: _mm(dg_s[i], bd(dg_s[i])))
    d2_s = each(lambda i: _split(d2[i]))
    d2_b = each(lambda i: bd(d2_s[i]))
    d3 = each(lambda i: _mm(dg_s[i], d2_b[i]))
    d4 = each(lambda i: _mm(d2_s[i], d2_b[i]))
    t1 = each(lambda i: eye - dg[i] + d2[i] - d3[i])
    inv = each(lambda i: t1[i] + pm(t1[i], d4[i]))
    for m in levels:
        mlev = each(lambda i: pm(inv[i], jnp.where(m, a_b[i], 0.0)))
        inv = each(lambda i: inv[i] - pm(mlev[i], inv[i]))

    u = each(lambda i: pm(inv[i], ss[i][:chunk] + av[i][:chunk]))
    upd = each(lambda i: _mm(_split(jnp.transpose(jnp.concatenate([vv[i], -u[i]], axis=0))),
                             kdbd[i]))
    bu = each(lambda i: pm(b_b[i], u[i]))
    for i, (b, p) in enumerate(chains):
        sl = slice(p * PAIR, (p + 1) * PAIR)
        o_ref[b, :, sl] = ss[i][chunk:] + av[i][chunk:] - bu[i]
        s_scr[b, p] = s_p[i] * etot[i] + jnp.where(same_head, upd[i], 0.0)

    sout_ref[...] = s_scr[...]


def _rwkv_chunk(r, km, v, lw, kk, bb, s0, *, chunk, nb):
    nseq, t, _ = r.shape
    assert chunk == HD_B and nseq % nb == 0
    npair = H_B // 2
    row = pl.BlockSpec((nb, chunk, W_B), lambda b, c: (b, c, 0))
    st = pl.BlockSpec((nb, npair, PAIR, PAIR), lambda b, c: (b, 0, 0, 0))
    kern = functools.partial(_rwkv_chunk_kernel, chunk=chunk, nb=nb)
    return pl.pallas_call(
        kern,
        grid=(nseq // nb, t // chunk),
        in_specs=[row] * 6 + [st],
        out_specs=[row, st],
        out_shape=[jax.ShapeDtypeStruct((nseq, t, W_B), F32),
                   jax.ShapeDtypeStruct((nseq, npair, PAIR, PAIR), F32)],
        scratch_shapes=[pltpu.VMEM((nb, npair, PAIR, PAIR), F32)],
        compiler_params=_cparams(("parallel", "arbitrary")),
        name="rwkv_chunk",
    )(r, km, v, lw, kk, bb, s0)


def _mix_out_kernel(x_ref, oa_ref, o_ref, r_ref, km_ref, v_ref, g_ref, lnw_ref, lnb_ref, rk_ref,
                    wa_ref, wb_ref, y_ref):
    ones = _head_ones(W_B)
    o = o_ref[...]
    cen = o - _head_sum(o, ones) * (1.0 / HD_B)
    var = _head_sum(cen * cen, ones) * (1.0 / HD_B)
    on = cen * lax.rsqrt(var + GN_EPS) * lnw_ref[...] + lnb_ref[...]
    bonus = _head_sum(r_ref[...] * km_ref[...] * rk_ref[...], ones) * v_ref[...]
    ob = (on + bonus) * g_ref[...]
    y_ref[...] = (x_ref[...]
                  + jnp.dot(oa_ref[...].astype(BF16), wa_ref[...], preferred_element_type=F32)
                  + jnp.dot(ob.astype(BF16), wb_ref[...], preferred_element_type=F32))


def _mix_out(x, oa, o, r, km, v, g, lnw, lnb, rk, wa, wb, tm):
    n, d = x.shape
    row = lambda w: pl.BlockSpec((tm, w), lambda i: (i, 0))
    par = pl.BlockSpec((1, W_B), lambda i: (0, 0))
    return pl.pallas_call(
        _mix_out_kernel,
        grid=(n // tm,),
        in_specs=[row(d), row(W_A), row(W_B), row(W_B), row(W_B), row(W_B), row(W_B),
                  par, par, par,
                  pl.BlockSpec((W_A, d), lambda i: (0, 0)),
                  pl.BlockSpec((W_B, d), lambda i: (0, 0))],
        out_specs=row(d),
        out_shape=jax.ShapeDtypeStruct((n, d), F32),
        compiler_params=_cparams(("parallel",)),
        name="mix_out",
    )(x, oa, o, r, km, v, g, lnw.reshape(1, W_B), lnb.reshape(1, W_B), rk.reshape(1, W_B),
      wa, wb)


def _proj_res_kernel(x_ref, o_ref, w_ref, y_ref):
    y_ref[...] = x_ref[...] + jnp.dot(o_ref[...].astype(BF16), w_ref[...],
                                      preferred_element_type=F32)


def _proj_res(x, o, w, tm):
    n, d = x.shape
    row = pl.BlockSpec((tm, d), lambda i: (i, 0))
    return pl.pallas_call(
        _proj_res_kernel,
        grid=(n // tm,),
        in_specs=[row, row, pl.BlockSpec((d, d), lambda i: (0, 0))],
        out_specs=row,
        out_shape=jax.ShapeDtypeStruct((n, d), F32),
        compiler_params=_cparams(("parallel",)),
        name="proj_res",
    )(x, o, w)


def _ffn_kernel(x_ref, g_ref, wg_ref, wu_ref, wd_ref, gf_ref, o_ref, xf_scr, acc_scr,
                *, final_norm):
    j = pl.program_id(1)

    @pl.when(j == 0)
    def _():
        x = x_ref[...]
        xf_scr[...] = _rms(x, g_ref[...]).astype(BF16)
        acc_scr[...] = x

    xf = xf_scr[...]
    hg = jnp.dot(xf, wg_ref[...], preferred_element_type=F32)
    hu = jnp.dot(xf, wu_ref[...], preferred_element_type=F32)
    h = (hg * _sigmoid(hg)) * hu
    acc_scr[...] += jnp.dot(h.astype(BF16), wd_ref[...], preferred_element_type=F32)

    @pl.when(j == pl.num_programs(1) - 1)
    def _():
        y = acc_scr[...]
        if final_norm:
            y = _rms(y, gf_ref[...])
        o_ref[...] = y


def _ffn(x, g, wg, wu, wd, gf, *, tm, tf, final_norm):
    n, d = x.shape
    ff = wg.shape[1]
    kern = functools.partial(_ffn_kernel, final_norm=final_norm)
    return pl.pallas_call(
        kern,
        grid=(n // tm, ff // tf),
        in_specs=[pl.BlockSpec((tm, d), lambda i, j: (i, 0)),
                  pl.BlockSpec((1, d), lambda i, j: (0, 0)),
                  pl.BlockSpec((d, tf), lambda i, j: (0, j)),
                  pl.BlockSpec((d, tf), lambda i, j: (0, j)),
                  pl.BlockSpec((tf, d), lambda i, j: (j, 0)),
                  pl.BlockSpec((1, d), lambda i, j: (0, 0))],
        out_specs=pl.BlockSpec((tm, d), lambda i, j: (i, 0)),
        out_shape=jax.ShapeDtypeStruct((n, d), F32),
        scratch_shapes=[pltpu.VMEM((tm, d), BF16), pltpu.VMEM((tm, d), F32)],
        compiler_params=_cparams(("parallel", "arbitrary")),
        name="ffn",
    )(x, g.reshape(1, d), wg, wu, wd, gf.reshape(1, d))


def _hgrn_kernel(q_ref, f_ref, i_ref, g_ref, s0_ref, lbc_ref, gn_ref, o_ref, sout_ref,
                 s_scr, q_scr, k_scr, bc_scr, i_scr, oh_scr, *, layer, chunk, valid):
    c = pl.program_id(1)

    @pl.when(c == 0)
    def _():
        s_scr[...] = s0_ref[0]

    lbc = lbc_ref[...]
    e = jnp.exp(lbc - jnp.max(lbc, axis=0, keepdims=True))
    sm = e / jnp.sum(e, axis=0, keepdims=True)
    lbf = sm[1:2, :]
    for l in range(2, layer + 1):
        lbf = lbf + sm[l:l + 1, :]

    fr = f_ref[0]
    ez = jnp.exp(-jnp.abs(fr))
    rz = 1.0 / (1.0 + ez)
    sig = jnp.where(fr >= 0.0, rz, ez * rz)
    nsig = jnp.where(fr >= 0.0, ez * rz, rz)
    logf = jnp.log(lbf + (1.0 - lbf) * sig)
    kfull = (1.0 - lbf) * nsig
    if valid < chunk:
        rowv = lax.broadcasted_iota(jnp.int32, logf.shape, 0)
        logf = jnp.where(rowv < valid, logf, 0.0)
        kfull = jnp.where(rowv < valid, kfull, 0.0)
    qfull = jax.nn.silu(q_ref[0])

    bc = logf
    rowc = lax.broadcasted_iota(jnp.int32, bc.shape, 0)
    d = 1
    while d < chunk:
        bc = bc + jnp.where(rowc >= d, pltpu.roll(bc, d, 0), 0.0)
        d *= 2

    ivfull = i_ref[0]
    for h in range(H_C):
        sl = slice(h * DK_C, (h + 1) * DK_C)
        q_scr[h] = qfull[:, sl]
        k_scr[h] = kfull[:, sl]
        bc_scr[h] = bc[:, sl]
        i_scr[h] = ivfull[:, sl]

    nblk = chunk // SUBLANES
    sub = lax.broadcasted_iota(jnp.int32, (SUBLANES, DK_C), 0)
    lane_c = lax.broadcasted_iota(jnp.int32, (SUBLANES, chunk), 1)

    def head(h, carry):
        q = q_scr[h]
        k = k_scr[h]
        b = bc_scr[h]
        iv = i_scr[h].astype(BF16)
        s_h = s_scr[h]
        ablk = []
        for tb in range(nblk):
            rs = slice(tb * SUBLANES, (tb + 1) * SUBLANES)
            blk = jnp.zeros((SUBLANES, chunk), F32)
            for j in range(SUBLANES):
                s = tb * SUBLANES + j
                if s >= valid:
                    break
                diff = jnp.where(sub >= j, b[rs, :] - b[s:s + 1, :], -jnp.inf)
                ev = jnp.exp(diff) * (q[rs, :] * k[s:s + 1, :])
                blk = jnp.where(lane_c == s, jnp.sum(ev, axis=1, keepdims=True), blk)
            if tb > 0:
                lo = tb * SUBLANES
                edge = b[lo - 1:lo, :]
                qi = q[rs, :] * jnp.exp(b[rs, :] - edge)
                ki = k[0:lo, :] * jnp.exp(edge - b[0:lo, :])
                if lo < chunk:
                    ki = jnp.concatenate([ki, jnp.zeros((chunk - lo, DK_C), F32)], axis=0)
                blk = blk + lax.dot_general(qi.astype(BF16), ki.astype(BF16),
                                            (((1,), (1,)), ((), ())),
                                            preferred_element_type=F32)
            ablk.append(blk)
        att = jnp.concatenate(ablk, axis=0) if nblk > 1 else ablk[0]
        o_h = (jnp.dot(att.astype(BF16), iv, preferred_element_type=F32)
               + jnp.dot((q * jnp.exp(b)).astype(BF16), s_h.astype(BF16),
                         preferred_element_type=F32))
        oh_scr[h] = o_h
        bl = b[chunk - 1:chunk, :]
        kd = k * jnp.exp(bl - b)
        pieces = [kd, jnp.broadcast_to(jnp.exp(bl), (SUBLANES, DK_C))]
        fill = DK_C - chunk - SUBLANES
        if fill > 0:
            pieces.append(jnp.zeros((fill, DK_C), F32))
        xt = jnp.transpose(jnp.concatenate(pieces, axis=0))
        kdt = xt[:, 0:chunk]
        ebl_col = xt[:, chunk:chunk + 1]
        s_scr[h] = ebl_col * s_h + jnp.dot(kdt.astype(BF16), iv, preferred_element_type=F32)
        return carry

    lax.fori_loop(0, H_C, head, 0, unroll=True if chunk <= SUBLANES else 4)
    o = jnp.concatenate([oh_scr[h] for h in range(H_C)], axis=1)
    o_ref[0] = _rms(o, gn_ref[...]) * jax.nn.silu(g_ref[0])
    sout_ref[0] = s_scr[...]


def _hgrn(pc3, s0, lbc, gn, *, layer, chunk, valid):
    nb, t, _ = pc3.shape
    col = lambda j: pl.BlockSpec((1, chunk, F_C), lambda b, c, j=j: (b, c, j))
    st = pl.BlockSpec((1, H_C, DK_C, DK_C), lambda b, c: (b, 0, 0, 0))
    kern = functools.partial(_hgrn_kernel, layer=layer, chunk=chunk, valid=valid)
    hs = lambda: pltpu.VMEM((H_C, chunk, DK_C), F32)
    return pl.pallas_call(
        kern,
        grid=(nb, t // chunk),
        in_specs=[col(0), col(1), col(2), col(3), st,
                  pl.BlockSpec(lbc.shape, lambda b, c: (0, 0)),
                  pl.BlockSpec((1, D_MODEL), lambda b, c: (0, 0))],
        out_specs=[pl.BlockSpec((1, chunk, D_MODEL), lambda b, c: (b, c, 0)), st],
        out_shape=[jax.ShapeDtypeStruct((nb, t, D_MODEL), F32),
                   jax.ShapeDtypeStruct(s0.shape, F32)],
        scratch_shapes=[pltpu.VMEM((H_C, DK_C, DK_C), F32), hs(), hs(), hs(), hs(), hs()],
        compiler_params=_cparams(("parallel", "arbitrary")),
        name="hgrn",
    )(pc3, pc3, pc3, pc3, s0, lbc, gn.reshape(1, D_MODEL))


def _block_diag(w):
    h, n, _ = w.shape
    eye = jnp.eye(h, dtype=w.dtype)
    return (eye[:, None, :, None] * w[:, :, None, :]).reshape(h * n, h * n)


def _fold(x):
    half = x.shape[-2] // 2
    return jnp.concatenate([x[..., :half, :], x[..., half:, :]], axis=-1)


def _unfold(x):
    half = x.shape[-1] // 2
    return jnp.concatenate([x[..., :half], x[..., half:]], axis=-2)


def _trunk(x2d, conv_st, h_st, shift_st, rs_st, hs_st, P, *, nbatch, nsteps, time_major):
    n = x2d.shape[0]
    if time_major:
        nseq, step, rows = 1, nbatch, n
        tm = n
    else:
        nseq, step, rows = nbatch, 1, 256
        tm = 512
    fold = not time_major
    nchain = nbatch * H_B

    pa, pb = _norm_matmul(x2d, P['ln_mix'][0], P['w_in_ab'], (P_A, P_B), tm)
    out_a, conv_new, h_new = _rglru(
        pa, conv_st, h_st, P['conv_w'], P['conv_b'], P['wr'], P['wi'], P['gr_b'], P['gi_b'],
        P['lru_lambda'], nseq=nseq, step=step, rows=rows)
    r, km, v, lw, kk, bb, g, shift_new = _rwkv_prep(
        pb, shift_st, P['mu_b'], P['w0_b'], P['a0_b'], P['w2p'], P['a2p'], P['g2_b'],
        P['kk_b'], P['ka_b'], nseq=nseq, step=step, rows=rows)

    if time_major:
        def to_lanes(z):
            z = z.reshape(nsteps, nbatch, H_B, HD_B).transpose(0, 3, 1, 2)
            return z.reshape(nsteps, HD_B, nchain)

        s0 = rs_st.transpose(2, 3, 0, 1).reshape(HD_B, HD_B, nchain)
        o, s_new = _rwkv_scan(to_lanes(r), to_lanes(km), to_lanes(v), to_lanes(lw), to_lanes(kk),
                              to_lanes(bb), s0, tb=nsteps)
        o = o.reshape(nsteps, HD_B, nbatch, H_B).transpose(0, 2, 3, 1).reshape(n, W_B)
        rs_new = s_new.reshape(HD_B, HD_B, nbatch, H_B).transpose(2, 3, 0, 1)
    else:
        z = jnp.zeros_like(rs_st[:, 0::2])
        s0 = jnp.concatenate([jnp.concatenate([rs_st[:, 0::2], z], axis=-1),
                              jnp.concatenate([z, rs_st[:, 1::2]], axis=-1)], axis=-2)
        seq3 = lambda y: y.reshape(nbatch, nsteps, W_B)
        o, s_new = _rwkv_chunk(seq3(r), seq3(km), seq3(v), seq3(lw), seq3(kk), seq3(bb), s0,
                               chunk=HD_B, nb=RWKV_SEQS_PER_STEP)
        o = o.reshape(n, W_B)
        rs_new = jnp.stack([s_new[:, :, :HD_B, :HD_B], s_new[:, :, HD_B:, HD_B:]],
                           axis=2).reshape(nbatch, H_B, HD_B, HD_B)

    x1 = _mix_out(x2d, out_a, o, r, km, v, g, P['lnx_w'], P['lnx_b'], P['rk_b'],
                  P['w_out_a'], P['w_out_b'], tm)
    x2 = _ffn(x1, P['ln_ffn'][0], P['ffn_gate'][0], P['ffn_up'][0], P['ffn_down'][0],
              P['ln_final'], tm=tm, tf=D_FF // 2, final_norm=False)

    (pc,) = _norm_matmul(x2, P['ln_mix'][1], P['w_in_c'], (4 * F_C,), tm)
    if time_major:
        pc3 = pc.reshape(nsteps, nbatch, 4 * F_C).transpose(1, 0, 2)
        chunk = SUBLANES
        pc3 = jnp.pad(pc3, ((0, 0), (0, chunk - nsteps), (0, 0)))
    else:
        pc3 = pc.reshape(nbatch, nsteps, 4 * F_C)
        chunk = 64
    o3, hs_new = _hgrn(pc3, hs_st, P['lb_c'], P['gn_c'], layer=1, chunk=chunk,
                       valid=min(chunk, nsteps))
    if time_major:
        o = o3[:, :nsteps].transpose(1, 0, 2).reshape(n, D_MODEL)
    else:
        o = o3.reshape(n, D_MODEL)
    x3 = _proj_res(x2, o, P['w_out_c'], tm)
    y = _ffn(x3, P['ln_ffn'][1], P['ffn_gate'][1], P['ffn_up'][1], P['ffn_down'][1],
             P['ln_final'], tm=tm, tf=D_FF // 2, final_norm=True)
    return y, conv_new, h_new, shift_new, rs_new, hs_new


def kernel(x_prompt, x_sample, state_rglru_conv, state_rglru_h, state_rwkv_shift, state_rwkv_S,
           state_hgrn_S, ln_mix, ln_ffn, ln_final, w_in_ab, conv_w, conv_b, gr_w, gr_b, gi_w,
           gi_b, lru_lambda, mu_b, w0_b, w2_b, a0_b, a2_b, g2_b, kk_b, ka_b, rk_b, lnx_w, lnx_b,
           w_out_ab, w_in_c, lb_c, gn_c, w_out_c, ffn_gate, ffn_up, ffn_down):
    bp, tp, _ = x_prompt.shape
    bs, ts, _ = x_sample.shape
    zpad_w = jnp.zeros((LORA_A, W_B), F32)
    zpad_a = jnp.zeros((LORA_W, W_B), F32)
    P = dict(
        ln_mix=ln_mix, ln_ffn=ln_ffn, ln_final=ln_final,
        w_in_ab=w_in_ab[0].astype(BF16), conv_w=conv_w[0], conv_b=conv_b[0],
        wr=_block_diag(gr_w[0]).astype(BF16), wi=_block_diag(gi_w[0]).astype(BF16),
        gr_b=gr_b[0], gi_b=gi_b[0], lru_lambda=lru_lambda[0], mu_b=mu_b[0], w0_b=w0_b[0],
        a0_b=a0_b[0],
        w2p=jnp.concatenate([w2_b[0], zpad_w], axis=0).astype(BF16),
        a2p=jnp.concatenate([zpad_a, a2_b[0]], axis=0).astype(BF16),
        g2_b=g2_b[0].astype(BF16), kk_b=kk_b[0], ka_b=ka_b[0], rk_b=rk_b[0],
        lnx_w=lnx_w[0], lnx_b=lnx_b[0],
        w_out_a=w_out_ab[0, :W_A].astype(BF16), w_out_b=w_out_ab[0, W_A:].astype(BF16),
        w_in_c=w_in_c[0].astype(BF16), lb_c=lb_c, gn_c=gn_c[0],
        w_out_c=w_out_c[0].astype(BF16), ffn_gate=ffn_gate.astype(BF16),
        ffn_up=ffn_up.astype(BF16), ffn_down=ffn_down.astype(BF16))

    yp, p_conv, p_h, p_shift, p_rs, p_hs = _trunk(
        x_prompt.reshape(bp * tp, D_MODEL),
        jnp.zeros((bp, CONV_W - 1, W_A), F32), jnp.zeros((bp, 1, W_A), F32),
        jnp.zeros((bp, 1, P_B), F32), jnp.zeros((bp, H_B, HD_B, HD_B), F32),
        jnp.zeros((bp, H_C, DK_C, DK_C), F32), P, nbatch=bp, nsteps=tp, time_major=False)

    ys, s_conv, s_h, s_shift, s_rs, s_hs = _trunk(
        x_sample.transpose(1, 0, 2).reshape(ts * bs, D_MODEL),
        state_rglru_conv[0].transpose(1, 0, 2).reshape(1, (CONV_W - 1) * bs, W_A),
        state_rglru_h[0].reshape(1, bs, W_A), state_rwkv_shift[0].reshape(1, bs, P_B),
        state_rwkv_S[0], state_hgrn_S[0], P, nbatch=bs, nsteps=ts, time_major=True)

    return (yp.reshape(bp, tp, D_MODEL),
            ys.reshape(ts, bs, D_MODEL).transpose(1, 0, 2),
            p_conv[None], p_h.reshape(1, bp, W_A), p_shift.reshape(1, bp, P_B), p_rs[None],
            p_hs[None],
            s_conv.reshape(CONV_W - 1, bs, W_A).transpose(1, 0, 2)[None],
            s_h.reshape(1, bs, W_A), s_shift.reshape(1, bs, P_B), s_rs[None], s_hs[None])
```

```python
import functools

import jax
import jax.numpy as jnp
from jax import lax
from jax.experimental import pallas as pl
from jax.experimental.pallas import tpu as pltpu

F32 = jnp.float32
BF16 = jnp.bfloat16

D_MODEL = 1024
W_A = 512
H_A = 8
CONV_W = 4
LRU_C = 8.0
W_B = 512
HD_B = 64
H_B = 8
LORA_W = 64
LORA_A = 64
LORA_G = 128
P_A = 2 * W_A
P_B = 3 * W_B + LORA_W + LORA_A + LORA_G
DK_C = 128
H_C = 8
F_C = 1024
D_FF = 2816
RMS_EPS = 1e-6
GN_EPS = 64e-5

LANES = 128
SUBLANES = 8
VMEM_LIMIT = 56 * 1024 * 1024


def _cparams(sem):
    return pltpu.CompilerParams(dimension_semantics=sem, vmem_limit_bytes=VMEM_LIMIT)


def _softplus(x):
    return jnp.maximum(x, 0.0) + jnp.log1p(jnp.exp(-jnp.abs(x)))


def _sigmoid(x):
    return 1.0 / (1.0 + jnp.exp(-x))


def _rms(x, g):
    ms = jnp.mean(x * x, axis=-1, keepdims=True)
    return x * lax.rsqrt(ms + RMS_EPS) * g


def _dot(a, b):
    return jnp.dot(a.astype(BF16), b.astype(BF16), preferred_element_type=F32)


def _split(x):
    hi = x.astype(BF16)
    lo = (x - hi.astype(F32)).astype(BF16)
    return hi, lo


def _mm(a, b, nt=False):
    ah, al = a
    bh, bl = b
    lhs = jnp.concatenate([ah, al, ah], axis=1)
    if nt:
        rhs = jnp.concatenate([bh, bh, bl], axis=1)
        return lax.dot_general(lhs, rhs, (((1,), (1,)), ((), ())), preferred_element_type=F32)
    rhs = jnp.concatenate([bh, bh, bl], axis=0)
    return jnp.dot(lhs, rhs, preferred_element_type=F32)


def _head_ones(width):
    r = lax.broadcasted_iota(jnp.int32, (width, width), 0) // HD_B
    c = lax.broadcasted_iota(jnp.int32, (width, width), 1) // HD_B
    return jnp.where(r == c, 1.0, 0.0).astype(BF16)


def _head_sum(x, ones):
    hi, lo = _split(x)
    return jnp.dot(jnp.concatenate([hi, lo], axis=1), jnp.concatenate([ones, ones], axis=0),
                   preferred_element_type=F32)


def _norm_matmul_kernel(x_ref, g_ref, w_ref, *o_refs):
    xn = _rms(x_ref[...], g_ref[...]).astype(BF16)
    off = 0
    for o_ref in o_refs:
        n = o_ref.shape[-1]
        o_ref[...] = jnp.dot(xn, w_ref[:, off:off + n], preferred_element_type=F32)
        off += n


def _norm_matmul(x, g, w, splits, tm):
    n, d = x.shape
    p = w.shape[1]
    assert sum(splits) == p and n % tm == 0
    return pl.pallas_call(
        _norm_matmul_kernel,
        grid=(n // tm,),
        in_specs=[pl.BlockSpec((tm, d), lambda i: (i, 0)),
                  pl.BlockSpec((1, d), lambda i: (0, 0)),
                  pl.BlockSpec((d, p), lambda i: (0, 0))],
        out_specs=[pl.BlockSpec((tm, s), lambda i: (i, 0)) for s in splits],
        out_shape=[jax.ShapeDtypeStruct((n, s), F32) for s in splits],
        compiler_params=_cparams(("parallel",)),
        name="norm_matmul",
    )(x, g.reshape(1, d), w)


def _rglru_kernel(u_ref, gate_ref, cst_ref, h0_ref, cw_ref, cb_ref, wr_ref, wi_ref, br_ref,
                  bi_ref, lam_ref, out_ref, cnew_ref, hnew_ref, xbuf, hcar, bbuf,
                  *, step, rows, pad):
    t = pl.program_id(1)
    hist = (CONV_W - 1) * step

    @pl.when(t == 0)
    def _():
        xbuf[pad - hist:pad, :] = cst_ref[0]
        hcar[...] = h0_ref[0]

    xbuf[pad:pad + rows, :] = u_ref[...]
    conv = cb_ref[...] + xbuf[pad - hist:pad - hist + rows, :] * cw_ref[0:1, :]
    for j in range(1, CONV_W):
        o = pad - hist + j * step
        conv = conv + xbuf[o:o + rows, :] * cw_ref[j:j + 1, :]
    tail = xbuf[pad + rows - hist:pad + rows, :]
    cnew_ref[0] = tail
    xbuf[pad - hist:pad, :] = tail

    ub = conv.astype(BF16)
    r = _sigmoid(jnp.dot(ub, wr_ref[...], preferred_element_type=F32) + br_ref[...])
    ig = _sigmoid(jnp.dot(ub, wi_ref[...], preferred_element_type=F32) + bi_ref[...])
    log_a = (-LRU_C) * r * _softplus(-lam_ref[...])
    a = jnp.exp(log_a)
    bterm = jnp.sqrt(-jnp.tanh(log_a) * (a * a + 1.0)) * (ig * conv)
    bbuf[...] = bterm
    bbuf[0:step, :] = bterm[0:step, :] + a[0:step, :] * hcar[...]
    bv = bbuf[...]

    row = lax.broadcasted_iota(jnp.int32, (rows, W_A), 0)
    d = step
    while d < rows:
        a_sh = pltpu.roll(a, d, 0)
        b_sh = pltpu.roll(bv, d, 0)
        m = row >= d
        bv = jnp.where(m, a * b_sh + bv, bv)
        a = jnp.where(m, a * a_sh, a)
        d *= 2
    hcar[...] = bv[rows - step:rows, :]
    hnew_ref[0] = bv[rows - step:rows, :]
    out_ref[...] = jax.nn.gelu(gate_ref[...]) * bv


def _rglru(pa, conv_st, h_st, cw, cb, wr, wi, br, bi, lam, *, nseq, step, rows):
    n = pa.shape[0]
    nt = n // (nseq * rows)
    hist = (CONV_W - 1) * step
    pad = max(SUBLANES, hist)
    row_map = lambda b, t: (b * nt + t, 0)
    vec = lambda: pl.BlockSpec((1, W_A), lambda b, t: (0, 0))
    kern = functools.partial(_rglru_kernel, step=step, rows=rows, pad=pad)
    return pl.pallas_call(
        kern,
        grid=(nseq, nt),
        in_specs=[pl.BlockSpec((rows, W_A), row_map),
                  pl.BlockSpec((rows, W_A), lambda b, t: (b * nt + t, 1)),
                  pl.BlockSpec((1, hist, W_A), lambda b, t: (b, 0, 0)),
                  pl.BlockSpec((1, step, W_A), lambda b, t: (b, 0, 0)),
                  pl.BlockSpec((CONV_W, W_A), lambda b, t: (0, 0)),
                  vec(),
                  pl.BlockSpec((W_A, W_A), lambda b, t: (0, 0)),
                  pl.BlockSpec((W_A, W_A), lambda b, t: (0, 0)),
                  vec(), vec(), vec()],
        out_specs=[pl.BlockSpec((rows, W_A), row_map),
                   pl.BlockSpec((1, hist, W_A), lambda b, t: (b, 0, 0)),
                   pl.BlockSpec((1, step, W_A), lambda b, t: (b, 0, 0))],
        out_shape=[jax.ShapeDtypeStruct((n, W_A), F32),
                   jax.ShapeDtypeStruct((nseq, hist, W_A), F32),
                   jax.ShapeDtypeStruct((nseq, step, W_A), F32)],
        scratch_shapes=[pltpu.VMEM((pad + rows, W_A), F32),
                        pltpu.VMEM((step, W_A), F32),
                        pltpu.VMEM((rows, W_A), F32)],
        compiler_params=_cparams(("parallel", "arbitrary")),
        name="rglru",
    )(pa, pa, conv_st, h_st, cw, cb.reshape(1, W_A), wr, wi, br.reshape(1, W_A),
      bi.reshape(1, W_A), lam.reshape(1, W_A))


def _rwkv_prep_kernel(pb_ref, prev_ref, mu_ref, w0_ref, a0_ref, w2_ref, a2_ref, g2_ref,
                      kkp_ref, kap_ref,
                      r_ref, km_ref, v_ref, lw_ref, kk_ref, bb_ref, g_ref, snew_ref, pbuf,
                      *, step, rows, pad):
    t = pl.program_id(1)

    @pl.when(t == 0)
    def _():
        pbuf[pad - step:pad, :] = prev_ref[0]

    pf = pb_ref[...]
    pbuf[pad:pad + rows, :] = pf
    shifted = pbuf[pad - step:pad - step + rows, :]
    last = pf[rows - step:rows, :]
    pbuf[pad - step:pad, :] = last
    snew_ref[0] = last

    m = pf + (shifted - pf) * mu_ref[...]
    o1 = 3 * W_B
    r_ref[...] = m[:, 0:W_B]
    kraw = m[:, W_B:2 * W_B]
    v_ref[...] = m[:, 2 * W_B:o1]
    xwa = m[:, o1:o1 + LORA_W + LORA_A]
    xg = m[:, o1 + LORA_W + LORA_A:]
    lane = lax.broadcasted_iota(jnp.int32, xwa.shape, 1)
    lhs = jnp.where(lane < LORA_W, jnp.tanh(xwa), xwa).astype(BF16)
    lw = jnp.dot(lhs, w2_ref[...], preferred_element_type=F32)
    la = jnp.dot(lhs, a2_ref[...], preferred_element_type=F32)
    w_log = -_softplus(-(w0_ref[...] + lw)) - 0.5
    lw_ref[...] = -jnp.exp(w_log)
    a = _sigmoid(a0_ref[...] + la)
    g_ref[...] = jnp.dot(_sigmoid(xg).astype(BF16), g2_ref[...], preferred_element_type=F32)
    kk = kraw * kkp_ref[...]
    kk = kk * lax.rsqrt(jnp.maximum(_head_sum(kk * kk, _head_ones(W_B)), 1e-24))
    kk_ref[...] = kk
    bb_ref[...] = kk * a
    km_ref[...] = kraw * (1.0 + (a - 1.0) * kap_ref[...])


def _rwkv_prep(pb, prev, mu, w0, a0, w2p, a2p, g2, kkp, kap, *, nseq, step, rows):
    n = pb.shape[0]
    nt = n // (nseq * rows)
    pad = max(SUBLANES, step)
    row_map = lambda b, t: (b * nt + t, 0)
    const = lambda shape: pl.BlockSpec(shape, lambda b, t: (0,) * len(shape))
    kern = functools.partial(_rwkv_prep_kernel, step=step, rows=rows, pad=pad)
    outs = pl.pallas_call(
        kern,
        grid=(nseq, nt),
        in_specs=[pl.BlockSpec((rows, P_B), row_map),
                  pl.BlockSpec((1, step, P_B), lambda b, t: (b, 0, 0)),
                  const((1, P_B)), const((1, W_B)), const((1, W_B)),
                  const((LORA_W + LORA_A, W_B)), const((LORA_W + LORA_A, W_B)),
                  const((LORA_G, W_B)), const((1, W_B)), const((1, W_B))],
        out_specs=[pl.BlockSpec((rows, W_B), row_map)] * 7
                  + [pl.BlockSpec((1, step, P_B), lambda b, t: (b, 0, 0))],
        out_shape=[jax.ShapeDtypeStruct((n, W_B), F32)] * 7
                  + [jax.ShapeDtypeStruct((nseq, step, P_B), F32)],
        scratch_shapes=[pltpu.VMEM((pad + rows, P_B), F32)],
        compiler_params=_cparams(("parallel", "arbitrary")),
        name="rwkv_prep",
    )(pb, prev, mu.reshape(1, P_B), w0.reshape(1, W_B), a0.reshape(1, W_B), w2p, a2p, g2,
      kkp.reshape(1, W_B), kap.reshape(1, W_B))
    return outs


VGROUP = 4


def _rwkv_scan_kernel(r_ref, km_ref, v_ref, lw_ref, kk_ref, bb_ref, s0_ref, o_ref, sout_ref,
                      s_scr, *, tb):
    tblk = pl.program_id(1)

    @pl.when(tblk == 0)
    def _():
        s_scr[...] = s0_ref[...]

    def colsum(x):
        return jnp.sum(x, axis=0, keepdims=True)

    sub = lax.broadcasted_iota(jnp.int32, (VGROUP, LANES), 0)
    for g in range(HD_B // VGROUP):
        vbase = g * VGROUP

        def step(t, state, vbase=vbase):
            kk = kk_ref[t]
            w = jnp.exp(lw_ref[t])
            bvec = bb_ref[t]
            km = km_ref[t]
            r = r_ref[t]
            vt = v_ref[t, vbase:vbase + VGROUP, :]
            new = []
            otile = jnp.zeros((VGROUP, LANES), F32)
            for i in range(VGROUP):
                sk = colsum(state[i] * kk)
                sv = state[i] * w - sk * bvec + vt[i:i + 1, :] * km
                new.append(sv)
                otile = jnp.where(sub == i, colsum(sv * r), otile)
            o_ref[t, vbase:vbase + VGROUP, :] = otile
            return tuple(new)

        state = tuple(s_scr[vbase + i] for i in range(VGROUP))
        state = lax.fori_loop(0, tb, step, state, unroll=2)
        for i in range(VGROUP):
            s_scr[vbase + i] = state[i]

    sout_ref[...] = s_scr[...]


def _rwkv_scan(r, km, v, lw, kk, bb, s0, *, tb):
    nt, _, nl = r.shape
    seq = pl.BlockSpec((tb, HD_B, LANES), lambda g, t: (t, 0, g))
    st = pl.BlockSpec((HD_B, HD_B, LANES), lambda g, t: (0, 0, g))
    kern = functools.partial(_rwkv_scan_kernel, tb=tb)
    return pl.pallas_call(
        kern,
        grid=(nl // LANES, nt // tb),
        in_specs=[seq] * 6 + [st],
        out_specs=[seq, st],
        out_shape=[jax.ShapeDtypeStruct((nt, HD_B, nl), F32),
                   jax.ShapeDtypeStruct((HD_B, HD_B, nl), F32)],
        scratch_shapes=[pltpu.VMEM((HD_B, HD_B, LANES), F32)],
        compiler_params=_cparams(("parallel", "arbitrary")),
        name="rwkv_scan",
    )(r, km, v, lw, kk, bb, s0)


PAIR = 2 * HD_B
RWKV_SEQS_PER_STEP = 4


def _rwkv_chunk_kernel(r_ref, km_ref, v_ref, lw_ref, kk_ref, bb_ref, s0_ref, o_ref, sout_ref,
                       s_scr, *, chunk, nb):
    c = pl.program_id(1)

    @pl.when(c == 0)
    def _():
        s_scr[...] = s0_ref[...]

    lane = lax.broadcasted_iota(jnp.int32, (chunk, PAIR), 1)
    row = lax.broadcasted_iota(jnp.int32, (chunk, PAIR), 0)
    low_half = lane < HD_B
    li = jnp.where(low_half, lane, lane - HD_B)
    strict = li < row
    incl = li <= row
    eye = jnp.where(li == row, 1.0, 0.0)
    same = lambda n: (li // n) == (row // n)
    m8 = same(8)
    levels = []
    n = 16
    prev = m8
    while n <= chunk:
        cur = same(n)
        levels.append(jnp.logical_and(cur, jnp.logical_not(prev)))
        prev = cur
        n *= 2
    sq_r = lax.broadcasted_iota(jnp.int32, (PAIR, PAIR), 0)
    sq_c = lax.broadcasted_iota(jnp.int32, (PAIR, PAIR), 1)
    same_head = (sq_r < HD_B) == (sq_c < HD_B)
    low_bf = lax.broadcasted_iota(jnp.int32, (chunk, PAIR), 1) < HD_B

    def bd(s):
        return tuple(jnp.concatenate([jnp.where(low_bf, x, jnp.zeros_like(x)),
                                      jnp.where(low_bf, jnp.zeros_like(x), x)], axis=0)
                     for x in s)

    pm = lambda x, y: _mm(_split(x), bd(_split(y)))

    chains = [(b, p) for b in range(nb) for p in range(H_B // 2)]
    each = lambda f: [f(i) for i in range(len(chains))]

    vv, lhs2, kg, bg, kdbd, etot, s_p = [], [], [], [], [], [], []
    for b, p in chains:
        sl = slice(p * PAIR, (p + 1) * PAIR)
        r = r_ref[b, :, sl]
        km = km_ref[b, :, sl]
        lw = lw_ref[b, :, sl]
        kk = kk_ref[b, :, sl]
        bb = bb_ref[b, :, sl]
        cum = lw
        d = 1
        while d < chunk:
            cum = cum + jnp.where(row >= d, pltpu.roll(cum, d, 0), 0.0)
            d *= 2
        tot = cum[chunk - 1:chunk, :]
        g_end = jnp.exp(tot - cum)
        g_inv = jnp.exp(-cum)
        vv.append(v_ref[b, :, sl])
        lhs2.append(_split(jnp.concatenate([kk * jnp.exp(cum - lw), r * jnp.exp(cum)], axis=0)))
        kg.append(bd(_split(km * g_inv)))
        bg.append(bd(_split(bb * g_inv)))
        kdbd.append(_split(jnp.concatenate([km * g_end, bb * g_end], axis=0)))
        etot.append(jnp.exp(tot))
        s_p.append(s_scr[b, p])

    ss = each(lambda i: _mm(lhs2[i], _split(s_p[i]), nt=True))
    kb = each(lambda i: _mm(lhs2[i], tuple(jnp.concatenate([x, y], axis=0)
                                           for x, y in zip(kg[i], bg[i])), nt=True))
    a_b = each(lambda i: jnp.where(strict, kb[i][:chunk, PAIR:], 0.0))
    b_b = each(lambda i: jnp.where(incl, kb[i][chunk:, PAIR:], 0.0))
    av = each(lambda i: _mm(_split(jnp.concatenate(
        [jnp.where(strict, kb[i][:chunk, :PAIR], 0.0), jnp.where(incl, kb[i][chunk:, :PAIR], 0.0)],
        axis=0)), bd(_split(vv[i]))))

    dg = each(lambda i: jnp.where(m8, a_b[i], 0.0))
    dg_s = each(lambda i: _split(dg[i]))
    d2 = each(lambda i: _mm(dg_s[i], bd(dg_s[i])))
    d2_s = each(lambda i: _split(d2[i]))
    d2_b = each(lambda i: bd(d2_s[i]))
    d3 = each(lambda i: _mm(dg_s[i], d2_b[i]))
    d4 = each(lambda i: _mm(d2_s[i], d2_b[i]))
    t1 = each(lambda i: eye - dg[i] + d2[i] - d3[i])
    inv = each(lambda i: t1[i] + pm(t1[i], d4[i]))
    for m in levels:
        mlev = each(lambda i: pm(inv[i], jnp.where(m, a_b[i], 0.0)))
        inv = each(lambda i: inv[i] - pm(mlev[i], inv[i]))

    u = each(lambda i: pm(inv[i], ss[i][:chunk] + av[i][:chunk]))
    upd = each(lambda i: _mm(_split(jnp.transpose(jnp.concatenate([vv[i], -u[i]], axis=0))),
                             kdbd[i]))
    bu = each(lambda i: pm(b_b[i], u[i]))
    for i, (b, p) in enumerate(chains):
        sl = slice(p * PAIR, (p + 1) * PAIR)
        o_ref[b, :, sl] = ss[i][chunk:] + av[i][chunk:] - bu[i]
        s_scr[b, p] = s_p[i] * etot[i] + jnp.where(same_head, upd[i], 0.0)

    sout_ref[...] = s_scr[...]


def _rwkv_chunk(r, km, v, lw, kk, bb, s0, *, chunk, nb):
    nseq, t, _ = r.shape
    assert chunk == HD_B and nseq % nb == 0
    npair = H_B // 2
    row = pl.BlockSpec((nb, chunk, W_B), lambda b, c: (b, c, 0))
    st = pl.BlockSpec((nb, npair, PAIR, PAIR), lambda b, c: (b, 0, 0, 0))
    kern = functools.partial(_rwkv_chunk_kernel, chunk=chunk, nb=nb)
    return pl.pallas_call(
        kern,
        grid=(nseq // nb, t // chunk),
        in_specs=[row] * 6 + [st],
        out_specs=[row, st],
        out_shape=[jax.ShapeDtypeStruct((nseq, t, W_B), F32),
                   jax.ShapeDtypeStruct((nseq, npair, PAIR, PAIR), F32)],
        scratch_shapes=[pltpu.VMEM((nb, npair, PAIR, PAIR), F32)],
        compiler_params=_cparams(("parallel", "arbitrary")),
        name="rwkv_chunk",
    )(r, km, v, lw, kk, bb, s0)


MXU_N = 256


def _mix_residual(x_ref, oa_ref, o_ref, r_ref, km_ref, v_ref, g_ref, lnw_ref, lnb_ref, rk_ref,
                  wa_ref, wb_ref):
    ones = _head_ones(W_B)
    o = o_ref[...]
    cen = o - _head_sum(o, ones) * (1.0 / HD_B)
    var = _head_sum(cen * cen, ones) * (1.0 / HD_B)
    on = cen * lax.rsqrt(var + GN_EPS) * lnw_ref[...] + lnb_ref[...]
    bonus = _head_sum(r_ref[...] * km_ref[...] * rk_ref[...], ones) * v_ref[...]
    ob = (on + bonus) * g_ref[...]
    return (x_ref[...]
            + jnp.dot(oa_ref[...].astype(BF16), wa_ref[...], preferred_element_type=F32)
            + jnp.dot(ob.astype(BF16), wb_ref[...], preferred_element_type=F32))


def _proj_residual(x_ref, o_ref, w_ref):
    return x_ref[...] + jnp.dot(o_ref[...].astype(BF16), w_ref[...],
                                preferred_element_type=F32)


def _res_ffn_kernel(*refs, residual, n_res, final_norm):
    g_ref, wg_ref, wu_ref, wd_ref, gf_ref, y_ref, xf_scr, x_scr, h_scr = refs[n_res:]
    x = residual(*refs[:n_res])
    xf_scr[...] = _rms(x, g_ref[...]).astype(BF16)
    x_scr[...] = x
    for j in range(wg_ref.shape[1] // MXU_N):
        cols = slice(j * MXU_N, (j + 1) * MXU_N)
        xf = xf_scr[...]
        hg = jnp.dot(xf, wg_ref[:, cols], preferred_element_type=F32)
        hu = jnp.dot(xf, wu_ref[:, cols], preferred_element_type=F32)
        h_scr[:, cols] = ((hg * _sigmoid(hg)) * hu).astype(BF16)
    y = x_scr[...] + jnp.dot(h_scr[...], wd_ref[...], preferred_element_type=F32)
    if final_norm:
        y = _rms(y, gf_ref[...])
    y_ref[...] = y


def _res_ffn(residual, row_ins, const_ins, g, wg, wu, wd, gf, *, tm, final_norm):
    n, d = row_ins[0].shape
    whole = lambda a: pl.BlockSpec(a.shape, lambda i: (0,) * a.ndim,
                                   pipeline_mode=pl.Buffered(1))
    consts = list(const_ins) + [g.reshape(1, d), wg, wu, wd, gf.reshape(1, d)]
    kern = functools.partial(_res_ffn_kernel, residual=residual,
                             n_res=len(row_ins) + len(const_ins), final_norm=final_norm)
    return pl.pallas_call(
        kern,
        grid=(n // tm,),
        in_specs=[pl.BlockSpec((tm, a.shape[1]), lambda i: (i, 0)) for a in row_ins]
                 + [whole(a) for a in consts],
        out_specs=pl.BlockSpec((tm, d), lambda i: (i, 0)),
        out_shape=jax.ShapeDtypeStruct((n, d), F32),
        scratch_shapes=[pltpu.VMEM((tm, d), BF16), pltpu.VMEM((tm, d), F32),
                        pltpu.VMEM((tm, wg.shape[1]), BF16)],
        compiler_params=_cparams(("parallel",)),
        name="res_ffn",
    )(*row_ins, *consts)


def _hgrn_kernel(q_ref, f_ref, i_ref, g_ref, s0_ref, lbc_ref, gn_ref, o_ref, sout_ref,
                 s_scr, q_scr, k_scr, bc_scr, i_scr, oh_scr, *, layer, chunk, valid):
    c = pl.program_id(1)

    @pl.when(c == 0)
    def _():
        s_scr[...] = s0_ref[0]

    lbc = lbc_ref[...]
    e = jnp.exp(lbc - jnp.max(lbc, axis=0, keepdims=True))
    sm = e / jnp.sum(e, axis=0, keepdims=True)
    lbf = sm[1:2, :]
    for l in range(2, layer + 1):
        lbf = lbf + sm[l:l + 1, :]

    fr = f_ref[0]
    ez = jnp.exp(-jnp.abs(fr))
    rz = 1.0 / (1.0 + ez)
    sig = jnp.where(fr >= 0.0, rz, ez * rz)
    nsig = jnp.where(fr >= 0.0, ez * rz, rz)
    logf = jnp.log(lbf + (1.0 - lbf) * sig)
    kfull = (1.0 - lbf) * nsig
    if valid < chunk:
        rowv = lax.broadcasted_iota(jnp.int32, logf.shape, 0)
        logf = jnp.where(rowv < valid, logf, 0.0)
        kfull = jnp.where(rowv < valid, kfull, 0.0)
    qfull = jax.nn.silu(q_ref[0])

    bc = logf
    rowc = lax.broadcasted_iota(jnp.int32, bc.shape, 0)
    d = 1
    while d < chunk:
        bc = bc + jnp.where(rowc >= d, pltpu.roll(bc, d, 0), 0.0)
        d *= 2

    ivfull = i_ref[0]
    for h in range(H_C):
        sl = slice(h * DK_C, (h + 1) * DK_C)
        q_scr[h] = qfull[:, sl]
        k_scr[h] = kfull[:, sl]
        bc_scr[h] = bc[:, sl]
        i_scr[h] = ivfull[:, sl]

    nblk = chunk // SUBLANES
    sub = lax.broadcasted_iota(jnp.int32, (SUBLANES, DK_C), 0)
    lane_c = lax.broadcasted_iota(jnp.int32, (SUBLANES, chunk), 1)

    def head(h, carry):
        q = q_scr[h]
        k = k_scr[h]
        b = bc_scr[h]
        iv = i_scr[h].astype(BF16)
        s_h = s_scr[h]
        ablk = []
        for tb in range(nblk):
            rs = slice(tb * SUBLANES, (tb + 1) * SUBLANES)
            blk = jnp.zeros((SUBLANES, chunk), F32)
            for j in range(SUBLANES):
                s = tb * SUBLANES + j
                if s >= valid:
                    break
                diff = jnp.where(sub >= j, b[rs, :] - b[s:s + 1, :], -jnp.inf)
                ev = jnp.exp(diff) * (q[rs, :] * k[s:s + 1, :])
                blk = jnp.where(lane_c == s, jnp.sum(ev, axis=1, keepdims=True), blk)
            if tb > 0:
                lo = tb * SUBLANES
                edge = b[lo - 1:lo, :]
                qi = q[rs, :] * jnp.exp(b[rs, :] - edge)
                ki = k[0:lo, :] * jnp.exp(edge - b[0:lo, :])
                if lo < chunk:
                    ki = jnp.concatenate([ki, jnp.zeros((chunk - lo, DK_C), F32)], axis=0)
                blk = blk + lax.dot_general(qi.astype(BF16), ki.astype(BF16),
                                            (((1,), (1,)), ((), ())),
                                            preferred_element_type=F32)
            ablk.append(blk)
        att = jnp.concatenate(ablk, axis=0) if nblk > 1 else ablk[0]
        o_h = (jnp.dot(att.astype(BF16), iv, preferred_element_type=F32)
               + jnp.dot((q * jnp.exp(b)).astype(BF16), s_h.astype(BF16),
                         preferred_element_type=F32))
        oh_scr[h] = o_h
        bl = b[chunk - 1:chunk, :]
        kd = k * jnp.exp(bl - b)
        pieces = [kd, jnp.broadcast_to(jnp.exp(bl), (SUBLANES, DK_C))]
        fill = DK_C - chunk - SUBLANES
        if fill > 0:
            pieces.append(jnp.zeros((fill, DK_C), F32))
        xt = jnp.transpose(jnp.concatenate(pieces, axis=0))
        kdt = xt[:, 0:chunk]
        ebl_col = xt[:, chunk:chunk + 1]
        s_scr[h] = ebl_col * s_h + jnp.dot(kdt.astype(BF16), iv, preferred_element_type=F32)
        return carry

    lax.fori_loop(0, H_C, head, 0, unroll=True if chunk <= SUBLANES else 4)
    o = jnp.concatenate([oh_scr[h] for h in range(H_C)], axis=1)
    o_ref[0] = _rms(o, gn_ref[...]) * jax.nn.silu(g_ref[0])
    sout_ref[0] = s_scr[...]


def _hgrn(pc3, s0, lbc, gn, *, layer, chunk, valid):
    nb, t, _ = pc3.shape
    col = lambda j: pl.BlockSpec((1, chunk, F_C), lambda b, c, j=j: (b, c, j))
    st = pl.BlockSpec((1, H_C, DK_C, DK_C), lambda b, c: (b, 0, 0, 0))
    kern = functools.partial(_hgrn_kernel, layer=layer, chunk=chunk, valid=valid)
    hs = lambda: pltpu.VMEM((H_C, chunk, DK_C), F32)
    return pl.pallas_call(
        kern,
        grid=(nb, t // chunk),
        in_specs=[col(0), col(1), col(2), col(3), st,
                  pl.BlockSpec(lbc.shape, lambda b, c: (0, 0)),
                  pl.BlockSpec((1, D_MODEL), lambda b, c: (0, 0))],
        out_specs=[pl.BlockSpec((1, chunk, D_MODEL), lambda b, c: (b, c, 0)), st],
        out_shape=[jax.ShapeDtypeStruct((nb, t, D_MODEL), F32),
                   jax.ShapeDtypeStruct(s0.shape, F32)],
        scratch_shapes=[pltpu.VMEM((H_C, DK_C, DK_C), F32), hs(), hs(), hs(), hs(), hs()],
        compiler_params=_cparams(("parallel", "arbitrary")),
        name="hgrn",
    )(pc3, pc3, pc3, pc3, s0, lbc, gn.reshape(1, D_MODEL))


def _block_diag(w):
    h, n, _ = w.shape
    eye = jnp.eye(h, dtype=w.dtype)
    return (eye[:, None, :, None] * w[:, :, None, :]).reshape(h * n, h * n)


def _fold(x):
    half = x.shape[-2] // 2
    return jnp.concatenate([x[..., :half, :], x[..., half:, :]], axis=-1)


def _unfold(x):
    half = x.shape[-1] // 2
    return jnp.concatenate([x[..., :half], x[..., half:]], axis=-2)


def _trunk(x2d, conv_st, h_st, shift_st, rs_st, hs_st, P, *, nbatch, nsteps, time_major):
    n = x2d.shape[0]
    if time_major:
        nseq, step, rows = 1, nbatch, n
        tm = n
    else:
        nseq, step, rows = nbatch, 1, 256
        tm = 512
    fold = not time_major
    nchain = nbatch * H_B

    pa, pb = _norm_matmul(x2d, P['ln_mix'][0], P['w_in_ab'], (P_A, P_B), tm)
    out_a, conv_new, h_new = _rglru(
        pa, conv_st, h_st, P['conv_w'], P['conv_b'], P['wr'], P['wi'], P['gr_b'], P['gi_b'],
        P['lru_lambda'], nseq=nseq, step=step, rows=rows)
    r, km, v, lw, kk, bb, g, shift_new = _rwkv_prep(
        pb, shift_st, P['mu_b'], P['w0_b'], P['a0_b'], P['w2p'], P['a2p'], P['g2_b'],
        P['kk_b'], P['ka_b'], nseq=nseq, step=step, rows=rows)

    if time_major:
        def to_lanes(z):
            z = z.reshape(nsteps, nbatch, H_B, HD_B).transpose(0, 3, 1, 2)
            return z.reshape(nsteps, HD_B, nchain)

        s0 = rs_st.transpose(2, 3, 0, 1).reshape(HD_B, HD_B, nchain)
        o, s_new = _rwkv_scan(to_lanes(r), to_lanes(km), to_lanes(v), to_lanes(lw), to_lanes(kk),
                              to_lanes(bb), s0, tb=nsteps)
        o = o.reshape(nsteps, HD_B, nbatch, H_B).transpose(0, 2, 3, 1).reshape(n, W_B)
        rs_new = s_new.reshape(HD_B, HD_B, nbatch, H_B).transpose(2, 3, 0, 1)
    else:
        z = jnp.zeros_like(rs_st[:, 0::2])
        s0 = jnp.concatenate([jnp.concatenate([rs_st[:, 0::2], z], axis=-1),
                              jnp.concatenate([z, rs_st[:, 1::2]], axis=-1)], axis=-2)
        seq3 = lambda y: y.reshape(nbatch, nsteps, W_B)
        o, s_new = _rwkv_chunk(seq3(r), seq3(km), seq3(v), seq3(lw), seq3(kk), seq3(bb), s0,
                               chunk=HD_B, nb=RWKV_SEQS_PER_STEP)
        o = o.reshape(n, W_B)
        rs_new = jnp.stack([s_new[:, :, :HD_B, :HD_B], s_new[:, :, HD_B:, HD_B:]],
                           axis=2).reshape(nbatch, H_B, HD_B, HD_B)

    vec = lambda p: p.reshape(1, W_B)
    x2 = _res_ffn(_mix_residual, [x2d, out_a, o, r, km, v, g],
                  [vec(P['lnx_w']), vec(P['lnx_b']), vec(P['rk_b']), P['w_out_a'], P['w_out_b']],
                  P['ln_ffn'][0], P['ffn_gate'][0], P['ffn_up'][0], P['ffn_down'][0],
                  P['ln_final'], tm=tm, final_norm=False)

    (pc,) = _norm_matmul(x2, P['ln_mix'][1], P['w_in_c'], (4 * F_C,), tm)
    if time_major:
        pc3 = pc.reshape(nsteps, nbatch, 4 * F_C).transpose(1, 0, 2)
        chunk = SUBLANES
        pc3 = jnp.pad(pc3, ((0, 0), (0, chunk - nsteps), (0, 0)))
    else:
        pc3 = pc.reshape(nbatch, nsteps, 4 * F_C)
        chunk = 64
    o3, hs_new = _hgrn(pc3, hs_st, P['lb_c'], P['gn_c'], layer=1, chunk=chunk,
                       valid=min(chunk, nsteps))
    if time_major:
        o = o3[:, :nsteps].transpose(1, 0, 2).reshape(n, D_MODEL)
    else:
        o = o3.reshape(n, D_MODEL)
    y = _res_ffn(_proj_residual, [x2, o], [P['w_out_c']],
                 P['ln_ffn'][1], P['ffn_gate'][1], P['ffn_up'][1], P['ffn_down'][1],
                 P['ln_final'], tm=tm, final_norm=True)
    return y, conv_new, h_new, shift_new, rs_new, hs_new


def kernel(x_prompt, x_sample, state_rglru_conv, state_rglru_h, state_rwkv_shift, state_rwkv_S,
           state_hgrn_S, ln_mix, ln_ffn, ln_final, w_in_ab, conv_w, conv_b, gr_w, gr_b, gi_w,
           gi_b, lru_lambda, mu_b, w0_b, w2_b, a0_b, a2_b, g2_b, kk_b, ka_b, rk_b, lnx_w, lnx_b,
           w_out_ab, w_in_c, lb_c, gn_c, w_out_c, ffn_gate, ffn_up, ffn_down):
    bp, tp, _ = x_prompt.shape
    bs, ts, _ = x_sample.shape
    zpad_w = jnp.zeros((LORA_A, W_B), F32)
    zpad_a = jnp.zeros((LORA_W, W_B), F32)
    P = dict(
        ln_mix=ln_mix, ln_ffn=ln_ffn, ln_final=ln_final,
        w_in_ab=w_in_ab[0].astype(BF16), conv_w=conv_w[0], conv_b=conv_b[0],
        wr=_block_diag(gr_w[0]).astype(BF16), wi=_block_diag(gi_w[0]).astype(BF16),
        gr_b=gr_b[0], gi_b=gi_b[0], lru_lambda=lru_lambda[0], mu_b=mu_b[0], w0_b=w0_b[0],
        a0_b=a0_b[0],
        w2p=jnp.concatenate([w2_b[0], zpad_w], axis=0).astype(BF16),
        a2p=jnp.concatenate([zpad_a, a2_b[0]], axis=0).astype(BF16),
        g2_b=g2_b[0].astype(BF16), kk_b=kk_b[0], ka_b=ka_b[0], rk_b=rk_b[0],
        lnx_w=lnx_w[0], lnx_b=lnx_b[0],
        w_out_a=w_out_ab[0, :W_A].astype(BF16), w_out_b=w_out_ab[0, W_A:].astype(BF16),
        w_in_c=w_in_c[0].astype(BF16), lb_c=lb_c, gn_c=gn_c[0],
        w_out_c=w_out_c[0].astype(BF16), ffn_gate=ffn_gate.astype(BF16),
        ffn_up=ffn_up.astype(BF16), ffn_down=ffn_down.astype(BF16))

    yp, p_conv, p_h, p_shift, p_rs, p_hs = _trunk(
        x_prompt.reshape(bp * tp, D_MODEL),
        jnp.zeros((bp, CONV_W - 1, W_A), F32), jnp.zeros((bp, 1, W_A), F32),
        jnp.zeros((bp, 1, P_B), F32), jnp.zeros((bp, H_B, HD_B, HD_B), F32),
        jnp.zeros((bp, H_C, DK_C, DK_C), F32), P, nbatch=bp, nsteps=tp, time_major=False)

    ys, s_conv, s_h, s_shift, s_rs, s_hs = _trunk(
        x_sample.transpose(1, 0, 2).reshape(ts * bs, D_MODEL),
        state_rglru_conv[0].transpose(1, 0, 2).reshape(1, (CONV_W - 1) * bs, W_A),
        state_rglru_h[0].reshape(1, bs, W_A), state_rwkv_shift[0].reshape(1, bs, P_B),
        state_rwkv_S[0], state_hgrn_S[0], P, nbatch=bs, nsteps=ts, time_major=True)

    return (yp.reshape(bp, tp, D_MODEL),
            ys.reshape(ts, bs, D_MODEL).transpose(1, 0, 2),
            p_conv[None], p_h.reshape(1, bp, W_A), p_shift.reshape(1, bp, P_B), p_rs[None],
            p_hs[None],
            s_conv.reshape(CONV_W - 1, bs, W_A).transpose(1, 0, 2)[None],
            s_h.reshape(1, bs, W_A), s_shift.reshape(1, bs, P_B), s_rs[None], s_hs[None])
```

```python
import functools

import jax
import jax.numpy as jnp
from jax import lax
from jax.experimental import pallas as pl
from jax.experimental.pallas import tpu as pltpu

F32 = jnp.float32
BF16 = jnp.bfloat16

D_MODEL = 1024
W_A = 512
H_A = 8
CONV_W = 4
LRU_C = 8.0
W_B = 512
HD_B = 64
H_B = 8
LORA_W = 64
LORA_A = 64
LORA_G = 128
P_A = 2 * W_A
P_B = 3 * W_B + LORA_W + LORA_A + LORA_G
DK_C = 128
H_C = 8
F_C = 1024
D_FF = 2816
RMS_EPS = 1e-6
GN_EPS = 64e-5

LANES = 128
SUBLANES = 8
VMEM_LIMIT = 56 * 1024 * 1024


def _cparams(sem):
    return pltpu.CompilerParams(dimension_semantics=sem, vmem_limit_bytes=VMEM_LIMIT)


def _softplus(x):
    return jnp.maximum(x, 0.0) + jnp.log1p(jnp.exp(-jnp.abs(x)))


def _sigmoid(x):
    return 1.0 / (1.0 + jnp.exp(-x))


def _rms(x, g):
    ms = jnp.mean(x * x, axis=-1, keepdims=True)
    return x * lax.rsqrt(ms + RMS_EPS) * g


def _dot(a, b):
    return jnp.dot(a.astype(BF16), b.astype(BF16), preferred_element_type=F32)


def _split(x):
    hi = x.astype(BF16)
    lo = (x - hi.astype(F32)).astype(BF16)
    return hi, lo


def _mm(a, b, nt=False):
    ah, al = a
    bh, bl = b
    lhs = jnp.concatenate([ah, al, ah], axis=1)
    if nt:
        rhs = jnp.concatenate([bh, bh, bl], axis=1)
        return lax.dot_general(lhs, rhs, (((1,), (1,)), ((), ())), preferred_element_type=F32)
    rhs = jnp.concatenate([bh, bh, bl], axis=0)
    return jnp.dot(lhs, rhs, preferred_element_type=F32)


def _head_ones(width):
    r = lax.broadcasted_iota(jnp.int32, (width, width), 0) // HD_B
    c = lax.broadcasted_iota(jnp.int32, (width, width), 1) // HD_B
    return jnp.where(r == c, 1.0, 0.0).astype(BF16)


def _head_sum(x, ones):
    hi, lo = _split(x)
    return jnp.dot(jnp.concatenate([hi, lo], axis=1), jnp.concatenate([ones, ones], axis=0),
                   preferred_element_type=F32)


def _norm_matmul_kernel(x_ref, g_ref, w_ref, *o_refs):
    xn = _rms(x_ref[...], g_ref[...]).astype(BF16)
    off = 0
    for o_ref in o_refs:
        n = o_ref.shape[-1]
        o_ref[...] = jnp.dot(xn, w_ref[:, off:off + n], preferred_element_type=F32)
        off += n


def _norm_matmul(x, g, w, splits, tm):
    n, d = x.shape
    p = w.shape[1]
    assert sum(splits) == p and n % tm == 0
    return pl.pallas_call(
        _norm_matmul_kernel,
        grid=(n // tm,),
        in_specs=[pl.BlockSpec((tm, d), lambda i: (i, 0)),
                  pl.BlockSpec((1, d), lambda i: (0, 0)),
                  pl.BlockSpec((d, p), lambda i: (0, 0))],
        out_specs=[pl.BlockSpec((tm, s), lambda i: (i, 0)) for s in splits],
        out_shape=[jax.ShapeDtypeStruct((n, s), F32) for s in splits],
        compiler_params=_cparams(("parallel",)),
        name="norm_matmul",
    )(x, g.reshape(1, d), w)


def _rglru_kernel(u_ref, gate_ref, cst_ref, h0_ref, cw_ref, cb_ref, wr_ref, wi_ref, br_ref,
                  bi_ref, lam_ref, out_ref, cnew_ref, hnew_ref, xbuf, hcar, bbuf,
                  *, step, rows, pad):
    t = pl.program_id(1)
    hist = (CONV_W - 1) * step

    @pl.when(t == 0)
    def _():
        xbuf[pad - hist:pad, :] = cst_ref[0]
        hcar[...] = h0_ref[0]

    if step == 1:
        u = u_ref[...]
        prev = xbuf[0:SUBLANES, :]
        sub8 = lax.broadcasted_iota(jnp.int32, (SUBLANES, W_A), 0)
        conv = cb_ref[...]
        for j in range(CONV_W):
            d = CONV_W - 1 - j
            if d == 0:
                ush = u
            else:
                rolled = pltpu.roll(u, d, 0)
                head = jnp.where(sub8 < d, pltpu.roll(prev, d, 0), rolled[0:SUBLANES, :])
                ush = jnp.concatenate([head, rolled[SUBLANES:, :]], axis=0)
            conv = conv + ush * cw_ref[j:j + 1, :]
        cnew_ref[0] = u[rows - hist:rows, :]
        xbuf[0:SUBLANES, :] = u[rows - SUBLANES:rows, :]
    else:
        xbuf[pad:pad + rows, :] = u_ref[...]
        conv = cb_ref[...] + xbuf[pad - hist:pad - hist + rows, :] * cw_ref[0:1, :]
        for j in range(1, CONV_W):
            o = pad - hist + j * step
            conv = conv + xbuf[o:o + rows, :] * cw_ref[j:j + 1, :]
        tail = xbuf[pad + rows - hist:pad + rows, :]
        cnew_ref[0] = tail
        xbuf[pad - hist:pad, :] = tail

    ub = conv.astype(BF16)
    r = _sigmoid(jnp.dot(ub, wr_ref[...], preferred_element_type=F32) + br_ref[...])
    ig = _sigmoid(jnp.dot(ub, wi_ref[...], preferred_element_type=F32) + bi_ref[...])
    log_a = (-LRU_C) * r * _softplus(-lam_ref[...])
    a = jnp.exp(log_a)
    bterm = jnp.sqrt(-jnp.tanh(log_a) * (a * a + 1.0)) * (ig * conv)
    bbuf[...] = bterm
    bbuf[0:step, :] = bterm[0:step, :] + a[0:step, :] * hcar[...]
    bv = bbuf[...]

    def scan_levels(a, bv, pos, first, count):
        d = first
        while d < first * count:
            m = pos >= d
            bv = jnp.where(m, a * pltpu.roll(bv, d, 0) + bv, bv)
            a = jnp.where(m, a * pltpu.roll(a, d, 0), a)
            d *= 2
        return a, bv

    row = lax.broadcasted_iota(jnp.int32, (rows, W_A), 0)
    a, bv = scan_levels(a, bv, row, step, rows // step)
    hcar[...] = bv[rows - step:rows, :]
    hnew_ref[0] = bv[rows - step:rows, :]
    out_ref[...] = jax.nn.gelu(gate_ref[...]) * bv


def _rglru(pa, conv_st, h_st, cw, cb, wr, wi, br, bi, lam, *, nseq, step, rows):
    n = pa.shape[0]
    nt = n // (nseq * rows)
    hist = (CONV_W - 1) * step
    pad = max(SUBLANES, hist)
    row_map = lambda b, t: (b * nt + t, 0)
    vec = lambda: pl.BlockSpec((1, W_A), lambda b, t: (0, 0))
    kern = functools.partial(_rglru_kernel, step=step, rows=rows, pad=pad)
    return pl.pallas_call(
        kern,
        grid=(nseq, nt),
        in_specs=[pl.BlockSpec((rows, W_A), row_map),
                  pl.BlockSpec((rows, W_A), lambda b, t: (b * nt + t, 1)),
                  pl.BlockSpec((1, hist, W_A), lambda b, t: (b, 0, 0)),
                  pl.BlockSpec((1, step, W_A), lambda b, t: (b, 0, 0)),
                  pl.BlockSpec((CONV_W, W_A), lambda b, t: (0, 0)),
                  vec(),
                  pl.BlockSpec((W_A, W_A), lambda b, t: (0, 0)),
                  pl.BlockSpec((W_A, W_A), lambda b, t: (0, 0)),
                  vec(), vec(), vec()],
        out_specs=[pl.BlockSpec((rows, W_A), row_map),
                   pl.BlockSpec((1, hist, W_A), lambda b, t: (b, 0, 0)),
                   pl.BlockSpec((1, step, W_A), lambda b, t: (b, 0, 0))],
        out_shape=[jax.ShapeDtypeStruct((n, W_A), F32),
                   jax.ShapeDtypeStruct((nseq, hist, W_A), F32),
                   jax.ShapeDtypeStruct((nseq, step, W_A), F32)],
        scratch_shapes=[pltpu.VMEM((pad + rows, W_A), F32),
                        pltpu.VMEM((step, W_A), F32),
                        pltpu.VMEM((rows, W_A), F32)],
        compiler_params=_cparams(("parallel", "arbitrary")),
        name="rglru",
    )(pa, pa, conv_st, h_st, cw, cb.reshape(1, W_A), wr, wi, br.reshape(1, W_A),
      bi.reshape(1, W_A), lam.reshape(1, W_A))


def _rwkv_prep_kernel(pb_ref, prev_ref, mu_ref, w0_ref, a0_ref, w2_ref, a2_ref, g2_ref,
                      kkp_ref, kap_ref,
                      r_ref, km_ref, v_ref, lw_ref, kk_ref, bb_ref, g_ref, snew_ref, pbuf,
                      *, step, rows, pad):
    t = pl.program_id(1)

    @pl.when(t == 0)
    def _():
        pbuf[pad - step:pad, :] = prev_ref[0]

    pf = pb_ref[...]
    if step == 1:
        rolled = pltpu.roll(pf, 1, 0)
        sub8 = lax.broadcasted_iota(jnp.int32, (SUBLANES, P_B), 0)
        head = jnp.where(sub8 < 1, pltpu.roll(pbuf[0:SUBLANES, :], 1, 0), rolled[0:SUBLANES, :])
        shifted = jnp.concatenate([head, rolled[SUBLANES:, :]], axis=0)
        pbuf[0:SUBLANES, :] = pf[rows - SUBLANES:rows, :]
    else:
        pbuf[pad:pad + rows, :] = pf
        shifted = pbuf[pad - step:pad - step + rows, :]
        pbuf[pad - step:pad, :] = pf[rows - step:rows, :]
    snew_ref[0] = pf[rows - step:rows, :]

    m = pf + (shifted - pf) * mu_ref[...]
    o1 = 3 * W_B
    r_ref[...] = m[:, 0:W_B]
    kraw = m[:, W_B:2 * W_B]
    v_ref[...] = m[:, 2 * W_B:o1]
    xwa = m[:, o1:o1 + LORA_W + LORA_A]
    xg = m[:, o1 + LORA_W + LORA_A:]
    lane = lax.broadcasted_iota(jnp.int32, xwa.shape, 1)
    lhs = jnp.where(lane < LORA_W, jnp.tanh(xwa), xwa).astype(BF16)
    lw = jnp.dot(lhs, w2_ref[...], preferred_element_type=F32)
    la = jnp.dot(lhs, a2_ref[...], preferred_element_type=F32)
    w_log = -_softplus(-(w0_ref[...] + lw)) - 0.5
    lw_ref[...] = -jnp.exp(w_log)
    a = _sigmoid(a0_ref[...] + la)
    g_ref[...] = jnp.dot(_sigmoid(xg).astype(BF16), g2_ref[...], preferred_element_type=F32)
    kk = kraw * kkp_ref[...]
    kk = kk * lax.rsqrt(jnp.maximum(_head_sum(kk * kk, _head_ones(W_B)), 1e-24))
    kk_ref[...] = kk
    bb_ref[...] = kk * a
    km_ref[...] = kraw * (1.0 + (a - 1.0) * kap_ref[...])


def _rwkv_prep(pb, prev, mu, w0, a0, w2p, a2p, g2, kkp, kap, *, nseq, step, rows):
    n = pb.shape[0]
    nt = n // (nseq * rows)
    pad = max(SUBLANES, step)
    row_map = lambda b, t: (b * nt + t, 0)
    const = lambda shape: pl.BlockSpec(shape, lambda b, t: (0,) * len(shape))
    kern = functools.partial(_rwkv_prep_kernel, step=step, rows=rows, pad=pad)
    outs = pl.pallas_call(
        kern,
        grid=(nseq, nt),
        in_specs=[pl.BlockSpec((rows, P_B), row_map),
                  pl.BlockSpec((1, step, P_B), lambda b, t: (b, 0, 0)),
                  const((1, P_B)), const((1, W_B)), const((1, W_B)),
                  const((LORA_W + LORA_A, W_B)), const((LORA_W + LORA_A, W_B)),
                  const((LORA_G, W_B)), const((1, W_B)), const((1, W_B))],
        out_specs=[pl.BlockSpec((rows, W_B), row_map)] * 7
                  + [pl.BlockSpec((1, step, P_B), lambda b, t: (b, 0, 0))],
        out_shape=[jax.ShapeDtypeStruct((n, W_B), F32)] * 7
                  + [jax.ShapeDtypeStruct((nseq, step, P_B), F32)],
        scratch_shapes=[pltpu.VMEM((pad + rows, P_B), F32)],
        compiler_params=_cparams(("parallel", "arbitrary")),
        name="rwkv_prep",
    )(pb, prev, mu.reshape(1, P_B), w0.reshape(1, W_B), a0.reshape(1, W_B), w2p, a2p, g2,
      kkp.reshape(1, W_B), kap.reshape(1, W_B))
    return outs


VGROUP = 4


def _rwkv_scan_kernel(r_ref, km_ref, v_ref, lw_ref, kk_ref, bb_ref, s0_ref, o_ref, sout_ref,
                      s_scr, *, tb):
    tblk = pl.program_id(1)

    @pl.when(tblk == 0)
    def _():
        s_scr[...] = s0_ref[...]

    def colsum(x):
        return jnp.sum(x, axis=0, keepdims=True)

    sub = lax.broadcasted_iota(jnp.int32, (VGROUP, LANES), 0)
    for g in range(HD_B // VGROUP):
        vbase = g * VGROUP

        def step(t, state, vbase=vbase):
            kk = kk_ref[t]
            w = jnp.exp(lw_ref[t])
            bvec = bb_ref[t]
            km = km_ref[t]
            r = r_ref[t]
            vt = v_ref[t, vbase:vbase + VGROUP, :]
            new = []
            otile = jnp.zeros((VGROUP, LANES), F32)
            for i in range(VGROUP):
                sk = colsum(state[i] * kk)
                sv = state[i] * w - sk * bvec + vt[i:i + 1, :] * km
                new.append(sv)
                otile = jnp.where(sub == i, colsum(sv * r), otile)
            o_ref[t, vbase:vbase + VGROUP, :] = otile
            return tuple(new)

        state = tuple(s_scr[vbase + i] for i in range(VGROUP))
        state = lax.fori_loop(0, tb, step, state, unroll=2)
        for i in range(VGROUP):
            s_scr[vbase + i] = state[i]

    sout_ref[...] = s_scr[...]


def _rwkv_scan(r, km, v, lw, kk, bb, s0, *, tb):
    nt, _, nl = r.shape
    seq = pl.BlockSpec((tb, HD_B, LANES), lambda g, t: (t, 0, g))
    st = pl.BlockSpec((HD_B, HD_B, LANES), lambda g, t: (0, 0, g))
    kern = functools.partial(_rwkv_scan_kernel, tb=tb)
    return pl.pallas_call(
        kern,
        grid=(nl // LANES, nt // tb),
        in_specs=[seq] * 6 + [st],
        out_specs=[seq, st],
        out_shape=[jax.ShapeDtypeStruct((nt, HD_B, nl), F32),
                   jax.ShapeDtypeStruct((HD_B, HD_B, nl), F32)],
        scratch_shapes=[pltpu.VMEM((HD_B, HD_B, LANES), F32)],
        compiler_params=_cparams(("parallel", "arbitrary")),
        name="rwkv_scan",
    )(r, km, v, lw, kk, bb, s0)


PAIR = 2 * HD_B
RWKV_SEQS_PER_STEP = 4


def _rwkv_chunk_kernel(r_ref, km_ref, v_ref, lw_ref, kk_ref, bb_ref, s0_ref, o_ref, sout_ref,
                       s_scr, *, chunk, nb):
    c = pl.program_id(1)

    @pl.when(c == 0)
    def _():
        s_scr[...] = s0_ref[...]

    lane = lax.broadcasted_iota(jnp.int32, (chunk, PAIR), 1)
    row = lax.broadcasted_iota(jnp.int32, (chunk, PAIR), 0)
    low_half = lane < HD_B
    li = jnp.where(low_half, lane, lane - HD_B)
    strict = li < row
    incl = li <= row
    eye = jnp.where(li == row, 1.0, 0.0)
    same = lambda n: (li // n) == (row // n)
    m8 = same(8)
    levels = []
    n = 16
    prev = m8
    while n <= chunk:
        cur = same(n)
        levels.append(jnp.logical_and(cur, jnp.logical_not(prev)))
        prev = cur
        n *= 2
    sq_r = lax.broadcasted_iota(jnp.int32, (PAIR, PAIR), 0)
    sq_c = lax.broadcasted_iota(jnp.int32, (PAIR, PAIR), 1)
    same_head = (sq_r < HD_B) == (sq_c < HD_B)
    low_bf = lax.broadcasted_iota(jnp.int32, (chunk, PAIR), 1) < HD_B

    def bd(s):
        return tuple(jnp.concatenate([jnp.where(low_bf, x, jnp.zeros_like(x)),
                                      jnp.where(low_bf, jnp.zeros_like(x), x)], axis=0)
                     for x in s)

    pm = lambda x, y: _mm(_split(x), bd(_split(y)))

    chains = [(b, p) for b in range(nb) for p in range(H_B // 2)]
    each = lambda f: [f(i) for i in range(len(chains))]

    vv, lhs2, kg, bg, kdbd, etot, s_p = [], [], [], [], [], [], []
    for b, p in chains:
        sl = slice(p * PAIR, (p + 1) * PAIR)
        r = r_ref[b, :, sl]
        km = km_ref[b, :, sl]
        lw = lw_ref[b, :, sl]
        kk = kk_ref[b, :, sl]
        bb = bb_ref[b, :, sl]
        cum = lw
        d = 1
        while d < chunk:
            cum = cum + jnp.where(row >= d, pltpu.roll(cum, d, 0), 0.0)
            d *= 2
        tot = cum[chunk - 1:chunk, :]
        g_end = jnp.exp(tot - cum)
        g_inv = jnp.exp(-cum)
        vv.append(v_ref[b, :, sl])
        lhs2.append(_split(jnp.concatenate([kk * jnp.exp(cum - lw), r * jnp.exp(cum)], axis=0)))
        kg.append(bd(_split(km * g_inv)))
        bg.append(bd(_split(bb * g_inv)))
        kdbd.append(_split(jnp.concatenate([km * g_end, bb * g_end], axis=0)))
        etot.append(jnp.exp(tot))
        s_p.append(s_scr[b, p])

    ss = each(lambda i: _mm(lhs2[i], _split(s_p[i]), nt=True))
    kb = each(lambda i: _mm(lhs2[i], tuple(jnp.concatenate([x, y], axis=0)
                                           for x, y in zip(kg[i], bg[i])), nt=True))
    a_b = each(lambda i: jnp.where(strict, kb[i][:chunk, PAIR:], 0.0))
    b_b = each(lambda i: jnp.where(incl, kb[i][chunk:, PAIR:], 0.0))
    av = each(lambda i: _mm(_split(jnp.concatenate(
        [jnp.where(strict, kb[i][:chunk, :PAIR], 0.0), jnp.where(incl, kb[i][chunk:, :PAIR], 0.0)],
        axis=0)), bd(_split(vv[i]))))

    dg = each(lambda i: jnp.where(m8, a_b[i], 0.0))
    dg_s = each(lambda i: _split(dg[i]))
    d2 = each(lambda i: _mm(dg_s[i], bd(dg_s[i])))
    d2_s = each(lambda i: _split(d2[i]))
    d2_b = each(lambda i: bd(d2_s[i]))
    d3 = each(lambda i: _mm(dg_s[i], d2_b[i]))
    d4 = each(lambda i: _mm(d2_s[i], d2_b[i]))
    t1 = each(lambda i: eye - dg[i] + d2[i] - d3[i])
    inv = each(lambda i: t1[i] + pm(t1[i], d4[i]))
    for m in levels:
        mlev = each(lambda i: pm(inv[i], jnp.where(m, a_b[i], 0.0)))
        inv = each(lambda i: inv[i] - pm(mlev[i], inv[i]))

    u = each(lambda i: pm(inv[i], ss[i][:chunk] + av[i][:chunk]))
    upd = each(lambda i: _mm(_split(jnp.transpose(jnp.concatenate([vv[i], -u[i]], axis=0))),
                             kdbd[i]))
    bu = each(lambda i: pm(b_b[i], u[i]))
    for i, (b, p) in enumerate(chains):
        sl = slice(p * PAIR, (p + 1) * PAIR)
        o_ref[b, :, sl] = ss[i][chunk:] + av[i][chunk:] - bu[i]
        s_scr[b, p] = s_p[i] * etot[i] + jnp.where(same_head, upd[i], 0.0)

    sout_ref[...] = s_scr[...]


def _rwkv_chunk(r, km, v, lw, kk, bb, s0, *, chunk, nb):
    nseq, t, _ = r.shape
    assert chunk == HD_B and nseq % nb == 0
    npair = H_B // 2
    row = pl.BlockSpec((nb, chunk, W_B), lambda b, c: (b, c, 0))
    st = pl.BlockSpec((nb, npair, PAIR, PAIR), lambda b, c: (b, 0, 0, 0))
    kern = functools.partial(_rwkv_chunk_kernel, chunk=chunk, nb=nb)
    return pl.pallas_call(
        kern,
        grid=(nseq // nb, t // chunk),
        in_specs=[row] * 6 + [st],
        out_specs=[row, st],
        out_shape=[jax.ShapeDtypeStruct((nseq, t, W_B), F32),
                   jax.ShapeDtypeStruct((nseq, npair, PAIR, PAIR), F32)],
        scratch_shapes=[pltpu.VMEM((nb, npair, PAIR, PAIR), F32)],
        compiler_params=_cparams(("parallel", "arbitrary")),
        name="rwkv_chunk",
    )(r, km, v, lw, kk, bb, s0)


MXU_N = 256


def _mix_residual(x_ref, oa_ref, o_ref, r_ref, km_ref, v_ref, g_ref, lnw_ref, lnb_ref, rk_ref,
                  wa_ref, wb_ref):
    ones = _head_ones(W_B)
    o = o_ref[...]
    cen = o - _head_sum(o, ones) * (1.0 / HD_B)
    var = _head_sum(cen * cen, ones) * (1.0 / HD_B)
    on = cen * lax.rsqrt(var + GN_EPS) * lnw_ref[...] + lnb_ref[...]
    bonus = _head_sum(r_ref[...] * km_ref[...] * rk_ref[...], ones) * v_ref[...]
    ob = (on + bonus) * g_ref[...]
    return (x_ref[...]
            + jnp.dot(oa_ref[...].astype(BF16), wa_ref[...], preferred_element_type=F32)
            + jnp.dot(ob.astype(BF16), wb_ref[...], preferred_element_type=F32))


def _proj_residual(x_ref, o_ref, w_ref):
    return x_ref[...] + jnp.dot(o_ref[...].astype(BF16), w_ref[...],
                                preferred_element_type=F32)


def _res_ffn_kernel(*refs, residual, n_res, final_norm):
    g_ref, wg_ref, wu_ref, wd_ref, gf_ref, y_ref, xf_scr, x_scr, h_scr = refs[n_res:]
    x = residual(*refs[:n_res])
    xf_scr[...] = _rms(x, g_ref[...]).astype(BF16)
    x_scr[...] = x
    for j in range(wg_ref.shape[1] // MXU_N):
        cols = slice(j * MXU_N, (j + 1) * MXU_N)
        xf = xf_scr[...]
        hg = jnp.dot(xf, wg_ref[:, cols], preferred_element_type=F32)
        hu = jnp.dot(xf, wu_ref[:, cols], preferred_element_type=F32)
        h_scr[:, cols] = ((hg * _sigmoid(hg)) * hu).astype(BF16)
    y = x_scr[...] + jnp.dot(h_scr[...], wd_ref[...], preferred_element_type=F32)
    if final_norm:
        y = _rms(y, gf_ref[...])
    y_ref[...] = y


def _res_ffn(residual, row_ins, const_ins, g, wg, wu, wd, gf, *, tm, final_norm):
    n, d = row_ins[0].shape
    whole = lambda a: pl.BlockSpec(a.shape, lambda i: (0,) * a.ndim,
                                   pipeline_mode=pl.Buffered(1))
    consts = list(const_ins) + [g.reshape(1, d), wg, wu, wd, gf.reshape(1, d)]
    kern = functools.partial(_res_ffn_kernel, residual=residual,
                             n_res=len(row_ins) + len(const_ins), final_norm=final_norm)
    return pl.pallas_call(
        kern,
        grid=(n // tm,),
        in_specs=[pl.BlockSpec((tm, a.shape[1]), lambda i: (i, 0)) for a in row_ins]
                 + [whole(a) for a in consts],
        out_specs=pl.BlockSpec((tm, d), lambda i: (i, 0)),
        out_shape=jax.ShapeDtypeStruct((n, d), F32),
        scratch_shapes=[pltpu.VMEM((tm, d), BF16), pltpu.VMEM((tm, d), F32),
                        pltpu.VMEM((tm, wg.shape[1]), BF16)],
        compiler_params=_cparams(("parallel",)),
        name="res_ffn",
    )(*row_ins, *consts)


def _hgrn_kernel(q_ref, f_ref, i_ref, g_ref, s0_ref, lbc_ref, gn_ref, o_ref, sout_ref,
                 s_scr, *, layer, chunk, valid, nb):
    c = pl.program_id(1)

    @pl.when(c == 0)
    def _():
        s_scr[...] = s0_ref[...]

    lbc = lbc_ref[...]
    e = jnp.exp(lbc - jnp.max(lbc, axis=0, keepdims=True))
    sm = e / jnp.sum(e, axis=0, keepdims=True)
    lbf = sm[1:2, :]
    for l in range(2, layer + 1):
        lbf = lbf + sm[l:l + 1, :]

    nblk = chunk // SUBLANES
    sub = lax.broadcasted_iota(jnp.int32, (SUBLANES, DK_C), 0)
    lane_c = lax.broadcasted_iota(jnp.int32, (SUBLANES, chunk), 1)
    rowc = lax.broadcasted_iota(jnp.int32, (chunk, F_C), 0)
    tril = jnp.where(lax.broadcasted_iota(jnp.int32, (chunk, chunk), 0)
                     >= lax.broadcasted_iota(jnp.int32, (chunk, chunk), 1), 1.0, 0.0).astype(BF16)

    qs, ks, bs, ivs = [], [], [], []
    for b in range(nb):
        fr = f_ref[b]
        ez = jnp.exp(-jnp.abs(fr))
        rz = 1.0 / (1.0 + ez)
        sig = jnp.where(fr >= 0.0, rz, ez * rz)
        nsig = jnp.where(fr >= 0.0, ez * rz, rz)
        logf = jnp.log(lbf + (1.0 - lbf) * sig)
        kfull = (1.0 - lbf) * nsig
        if valid < chunk:
            logf = jnp.where(rowc < valid, logf, 0.0)
            kfull = jnp.where(rowc < valid, kfull, 0.0)
        hi, lo = _split(logf)
        bc = jnp.dot(jnp.concatenate([tril, tril], axis=1), jnp.concatenate([hi, lo], axis=0),
                     preferred_element_type=F32)
        qs.append(jax.nn.silu(q_ref[b]))
        ks.append(kfull)
        bs.append(bc)
        ivs.append(i_ref[b].astype(BF16))

    chains = [(b, h) for b in range(nb) for h in range(H_C)]
    each = lambda f: [f(i) for i in range(len(chains))]
    hsl = lambda h: slice(h * DK_C, (h + 1) * DK_C)
    q = each(lambda i: qs[chains[i][0]][:, hsl(chains[i][1])])
    k = each(lambda i: ks[chains[i][0]][:, hsl(chains[i][1])])
    bcs = each(lambda i: bs[chains[i][0]][:, hsl(chains[i][1])])
    iv = each(lambda i: ivs[chains[i][0]][:, hsl(chains[i][1])])
    s_h = each(lambda i: s_scr[chains[i][0], chains[i][1]])

    def off_diag(i, tb):
        lo = tb * SUBLANES
        rs = slice(lo, lo + SUBLANES)
        edge = bcs[i][lo - 1:lo, :]
        qi = q[i][rs, :] * jnp.exp(bcs[i][rs, :] - edge)
        ki = k[i][0:lo, :] * jnp.exp(edge - bcs[i][0:lo, :])
        ki = jnp.concatenate([ki, jnp.zeros((chunk - lo, DK_C), F32)], axis=0)
        return lax.dot_general(qi.astype(BF16), ki.astype(BF16), (((1,), (1,)), ((), ())),
                               preferred_element_type=F32)

    offd = each(lambda i: [off_diag(i, tb) for tb in range(1, nblk)])
    inter = each(lambda i: jnp.dot((q[i] * jnp.exp(bcs[i])).astype(BF16), s_h[i].astype(BF16),
                                   preferred_element_type=F32))

    def diag(i, tb):
        rs = slice(tb * SUBLANES, (tb + 1) * SUBLANES)
        blk = jnp.zeros((SUBLANES, chunk), F32) if tb == 0 else offd[i][tb - 1]
        for j in range(SUBLANES):
            s = tb * SUBLANES + j
            if s >= valid:
                break
            diff = jnp.where(sub >= j, bcs[i][rs, :] - bcs[i][s:s + 1, :], -jnp.inf)
            ev = jnp.exp(diff) * (q[i][rs, :] * k[i][s:s + 1, :])
            blk = jnp.where(lane_c == s, jnp.sum(ev, axis=1, keepdims=True), blk)
        return blk

    def attend(i):
        blks = [diag(i, tb) for tb in range(nblk)]
        att = jnp.concatenate(blks, axis=0) if nblk > 1 else blks[0]
        return jnp.dot(att.astype(BF16), iv[i], preferred_element_type=F32) + inter[i]

    o_h = each(attend)

    def new_state(i):
        bl = bcs[i][chunk - 1:chunk, :]
        kd = k[i] * jnp.exp(bl - bcs[i])
        pieces = [kd, jnp.broadcast_to(jnp.exp(bl), (SUBLANES, DK_C))]
        fill = DK_C - chunk - SUBLANES
        if fill > 0:
            pieces.append(jnp.zeros((fill, DK_C), F32))
        xt = jnp.transpose(jnp.concatenate(pieces, axis=0))
        return (xt[:, chunk:chunk + 1] * s_h[i]
                + jnp.dot(xt[:, 0:chunk].astype(BF16), iv[i], preferred_element_type=F32))

    s_new = each(new_state)
    for i, (b, h) in enumerate(chains):
        s_scr[b, h] = s_new[i]
    for b in range(nb):
        o = jnp.concatenate([o_h[b * H_C + h] for h in range(H_C)], axis=1)
        o_ref[b] = _rms(o, gn_ref[...]) * jax.nn.silu(g_ref[b])
    sout_ref[...] = s_scr[...]


def _hgrn(pc3, s0, lbc, gn, *, layer, chunk, valid, nb):
    nseq, t, _ = pc3.shape
    assert nseq % nb == 0
    col = lambda j: pl.BlockSpec((nb, chunk, F_C), lambda b, c, j=j: (b, c, j))
    st = pl.BlockSpec((nb, H_C, DK_C, DK_C), lambda b, c: (b, 0, 0, 0))
    kern = functools.partial(_hgrn_kernel, layer=layer, chunk=chunk, valid=valid, nb=nb)
    return pl.pallas_call(
        kern,
        grid=(nseq // nb, t // chunk),
        in_specs=[col(0), col(1), col(2), col(3), st,
                  pl.BlockSpec(lbc.shape, lambda b, c: (0, 0)),
                  pl.BlockSpec((1, D_MODEL), lambda b, c: (0, 0))],
        out_specs=[pl.BlockSpec((nb, chunk, D_MODEL), lambda b, c: (b, c, 0)), st],
        out_shape=[jax.ShapeDtypeStruct((nseq, t, D_MODEL), F32),
                   jax.ShapeDtypeStruct(s0.shape, F32)],
        scratch_shapes=[pltpu.VMEM((nb, H_C, DK_C, DK_C), F32)],
        compiler_params=_cparams(("parallel", "arbitrary")),
        name="hgrn",
    )(pc3, pc3, pc3, pc3, s0, lbc, gn.reshape(1, D_MODEL))


def _block_diag(w):
    h, n, _ = w.shape
    eye = jnp.eye(h, dtype=w.dtype)
    return (eye[:, None, :, None] * w[:, :, None, :]).reshape(h * n, h * n)


def _fold(x):
    half = x.shape[-2] // 2
    return jnp.concatenate([x[..., :half, :], x[..., half:, :]], axis=-1)


def _unfold(x):
    half = x.shape[-1] // 2
    return jnp.concatenate([x[..., :half], x[..., half:]], axis=-2)


def _trunk(x2d, conv_st, h_st, shift_st, rs_st, hs_st, P, *, nbatch, nsteps, time_major):
    n = x2d.shape[0]
    if time_major:
        nseq, step, rows = 1, nbatch, n
        tm = n
    else:
        nseq, step, rows = nbatch, 1, 256
        tm = 512
    fold = not time_major
    nchain = nbatch * H_B

    pa, pb = _norm_matmul(x2d, P['ln_mix'][0], P['w_in_ab'], (P_A, P_B), tm)
    out_a, conv_new, h_new = _rglru(
        pa, conv_st, h_st, P['conv_w'], P['conv_b'], P['wr'], P['wi'], P['gr_b'], P['gi_b'],
        P['lru_lambda'], nseq=nseq, step=step, rows=rows)
    r, km, v, lw, kk, bb, g, shift_new = _rwkv_prep(
        pb, shift_st, P['mu_b'], P['w0_b'], P['a0_b'], P['w2p'], P['a2p'], P['g2_b'],
        P['kk_b'], P['ka_b'], nseq=nseq, step=step, rows=rows)

    if time_major:
        def to_lanes(z):
            z = z.reshape(nsteps, nbatch, H_B, HD_B).transpose(0, 3, 1, 2)
            return z.reshape(nsteps, HD_B, nchain)

        s0 = rs_st.transpose(2, 3, 0, 1).reshape(HD_B, HD_B, nchain)
        o, s_new = _rwkv_scan(to_lanes(r), to_lanes(km), to_lanes(v), to_lanes(lw), to_lanes(kk),
                              to_lanes(bb), s0, tb=nsteps)
        o = o.reshape(nsteps, HD_B, nbatch, H_B).transpose(0, 2, 3, 1).reshape(n, W_B)
        rs_new = s_new.reshape(HD_B, HD_B, nbatch, H_B).transpose(2, 3, 0, 1)
    else:
        z = jnp.zeros_like(rs_st[:, 0::2])
        s0 = jnp.concatenate([jnp.concatenate([rs_st[:, 0::2], z], axis=-1),
                              jnp.concatenate([z, rs_st[:, 1::2]], axis=-1)], axis=-2)
        seq3 = lambda y: y.reshape(nbatch, nsteps, W_B)
        o, s_new = _rwkv_chunk(seq3(r), seq3(km), seq3(v), seq3(lw), seq3(kk), seq3(bb), s0,
                               chunk=HD_B, nb=RWKV_SEQS_PER_STEP)
        o = o.reshape(n, W_B)
        rs_new = jnp.stack([s_new[:, :, :HD_B, :HD_B], s_new[:, :, HD_B:, HD_B:]],
                           axis=2).reshape(nbatch, H_B, HD_B, HD_B)

    vec = lambda p: p.reshape(1, W_B)
    x2 = _res_ffn(_mix_residual, [x2d, out_a, o, r, km, v, g],
                  [vec(P['lnx_w']), vec(P['lnx_b']), vec(P['rk_b']), P['w_out_a'], P['w_out_b']],
                  P['ln_ffn'][0], P['ffn_gate'][0], P['ffn_up'][0], P['ffn_down'][0],
                  P['ln_final'], tm=tm, final_norm=False)

    (pc,) = _norm_matmul(x2, P['ln_mix'][1], P['w_in_c'], (4 * F_C,), tm)
    if time_major:
        pc3 = pc.reshape(nsteps, nbatch, 4 * F_C).transpose(1, 0, 2)
        chunk = SUBLANES
        pc3 = jnp.pad(pc3, ((0, 0), (0, chunk - nsteps), (0, 0)))
    else:
        pc3 = pc.reshape(nbatch, nsteps, 4 * F_C)
        chunk = 64
    o3, hs_new = _hgrn(pc3, hs_st, P['lb_c'], P['gn_c'], layer=1, chunk=chunk,
                       valid=min(chunk, nsteps), nb=4 if time_major else 1)
    if time_major:
        o = o3[:, :nsteps].transpose(1, 0, 2).reshape(n, D_MODEL)
    else:
        o = o3.reshape(n, D_MODEL)
    y = _res_ffn(_proj_residual, [x2, o], [P['w_out_c']],
                 P['ln_ffn'][1], P['ffn_gate'][1], P['ffn_up'][1], P['ffn_down'][1],
                 P['ln_final'], tm=tm, final_norm=True)
    return y, conv_new, h_new, shift_new, rs_new, hs_new


def kernel(x_prompt, x_sample, state_rglru_conv, state_rglru_h, state_rwkv_shift, state_rwkv_S,
           state_hgrn_S, ln_mix, ln_ffn, ln_final, w_in_ab, conv_w, conv_b, gr_w, gr_b, gi_w,
           gi_b, lru_lambda, mu_b, w0_b, w2_b, a0_b, a2_b, g2_b, kk_b, ka_b, rk_b, lnx_w, lnx_b,
           w_out_ab, w_in_c, lb_c, gn_c, w_out_c, ffn_gate, ffn_up, ffn_down):
    bp, tp, _ = x_prompt.shape
    bs, ts, _ = x_sample.shape
    zpad_w = jnp.zeros((LORA_A, W_B), F32)
    zpad_a = jnp.zeros((LORA_W, W_B), F32)
    P = dict(
        ln_mix=ln_mix, ln_ffn=ln_ffn, ln_final=ln_final,
        w_in_ab=w_in_ab[0].astype(BF16), conv_w=conv_w[0], conv_b=conv_b[0],
        wr=_block_diag(gr_w[0]).astype(BF16), wi=_block_diag(gi_w[0]).astype(BF16),
        gr_b=gr_b[0], gi_b=gi_b[0], lru_lambda=lru_lambda[0], mu_b=mu_b[0], w0_b=w0_b[0],
        a0_b=a0_b[0],
        w2p=jnp.concatenate([w2_b[0], zpad_w], axis=0).astype(BF16),
        a2p=jnp.concatenate([zpad_a, a2_b[0]], axis=0).astype(BF16),
        g2_b=g2_b[0].astype(BF16), kk_b=kk_b[0], ka_b=ka_b[0], rk_b=rk_b[0],
        lnx_w=lnx_w[0], lnx_b=lnx_b[0],
        w_out_a=w_out_ab[0, :W_A].astype(BF16), w_out_b=w_out_ab[0, W_A:].astype(BF16),
        w_in_c=w_in_c[0].astype(BF16), lb_c=lb_c, gn_c=gn_c[0],
        w_out_c=w_out_c[0].astype(BF16), ffn_gate=ffn_gate.astype(BF16),
        ffn_up=ffn_up.astype(BF16), ffn_down=ffn_down.astype(BF16))

    yp, p_conv, p_h, p_shift, p_rs, p_hs = _trunk(
        x_prompt.reshape(bp * tp, D_MODEL),
        jnp.zeros((bp, CONV_W - 1, W_A), F32), jnp.zeros((bp, 1, W_A), F32),
        jnp.zeros((bp, 1, P_B), F32), jnp.zeros((bp, H_B, HD_B, HD_B), F32),
        jnp.zeros((bp, H_C, DK_C, DK_C), F32), P, nbatch=bp, nsteps=tp, time_major=False)

    ys, s_conv, s_h, s_shift, s_rs, s_hs = _trunk(
        x_sample.transpose(1, 0, 2).reshape(ts * bs, D_MODEL),
        state_rglru_conv[0].transpose(1, 0, 2).reshape(1, (CONV_W - 1) * bs, W_A),
        state_rglru_h[0].reshape(1, bs, W_A), state_rwkv_shift[0].reshape(1, bs, P_B),
        state_rwkv_S[0], state_hgrn_S[0], P, nbatch=bs, nsteps=ts, time_major=True)

    return (yp.reshape(bp, tp, D_MODEL),
            ys.reshape(ts, bs, D_MODEL).transpose(1, 0, 2),
            p_conv[None], p_h.reshape(1, bp, W_A), p_shift.reshape(1, bp, P_B), p_rs[None],
            p_hs[None],
            s_conv.reshape(CONV_W - 1, bs, W_A).transpose(1, 0, 2)[None],
            s_h.reshape(1, bs, W_A), s_shift.reshape(1, bs, P_B), s_rs[None], s_hs[None])
```

```python
import functools

import jax
import jax.numpy as jnp
from jax import lax
from jax.experimental import pallas as pl
from jax.experimental.pallas import tpu as pltpu

F32 = jnp.float32
BF16 = jnp.bfloat16

D_MODEL = 1024
W_A = 512
H_A = 8
CONV_W = 4
LRU_C = 8.0
W_B = 512
HD_B = 64
H_B = 8
LORA_W = 64
LORA_A = 64
LORA_G = 128
P_A = 2 * W_A
P_B = 3 * W_B + LORA_W + LORA_A + LORA_G
DK_C = 128
H_C = 8
F_C = 1024
D_FF = 2816
RMS_EPS = 1e-6
GN_EPS = 64e-5

LANES = 128
SUBLANES = 8
VMEM_LIMIT = 56 * 1024 * 1024


def _cparams(sem):
    return pltpu.CompilerParams(dimension_semantics=sem, vmem_limit_bytes=VMEM_LIMIT)


def _softplus(x):
    return jnp.maximum(x, 0.0) + jnp.log1p(jnp.exp(-jnp.abs(x)))


def _sigmoid(x):
    return 1.0 / (1.0 + jnp.exp(-x))


def _rms(x, g):
    ms = jnp.mean(x * x, axis=-1, keepdims=True)
    return x * lax.rsqrt(ms + RMS_EPS) * g


def _dot(a, b):
    return jnp.dot(a.astype(BF16), b.astype(BF16), preferred_element_type=F32)


def _split(x):
    hi = x.astype(BF16)
    lo = (x - hi.astype(F32)).astype(BF16)
    return hi, lo


def _mm(a, b, nt=False):
    ah, al = a
    bh, bl = b
    lhs = jnp.concatenate([ah, al, ah], axis=1)
    if nt:
        rhs = jnp.concatenate([bh, bh, bl], axis=1)
        return lax.dot_general(lhs, rhs, (((1,), (1,)), ((), ())), preferred_element_type=F32)
    rhs = jnp.concatenate([bh, bh, bl], axis=0)
    return jnp.dot(lhs, rhs, preferred_element_type=F32)


def _head_ones(width):
    r = lax.broadcasted_iota(jnp.int32, (width, width), 0) // HD_B
    c = lax.broadcasted_iota(jnp.int32, (width, width), 1) // HD_B
    return jnp.where(r == c, 1.0, 0.0).astype(BF16)


def _head_sum(x, ones):
    hi, lo = _split(x)
    return jnp.dot(jnp.concatenate([hi, lo], axis=1), jnp.concatenate([ones, ones], axis=0),
                   preferred_element_type=F32)


def _norm_matmul_kernel(x_ref, g_ref, w_ref, *o_refs):
    xn = _rms(x_ref[...], g_ref[...]).astype(BF16)
    off = 0
    for o_ref in o_refs:
        n = o_ref.shape[-1]
        o_ref[...] = jnp.dot(xn, w_ref[:, off:off + n], preferred_element_type=F32)
        off += n


def _norm_matmul(x, g, w, splits, tm):
    n, d = x.shape
    p = w.shape[1]
    assert sum(splits) == p and n % tm == 0
    return pl.pallas_call(
        _norm_matmul_kernel,
        grid=(n // tm,),
        in_specs=[pl.BlockSpec((tm, d), lambda i: (i, 0)),
                  pl.BlockSpec((1, d), lambda i: (0, 0)),
                  pl.BlockSpec((d, p), lambda i: (0, 0))],
        out_specs=[pl.BlockSpec((tm, s), lambda i: (i, 0)) for s in splits],
        out_shape=[jax.ShapeDtypeStruct((n, s), F32) for s in splits],
        compiler_params=_cparams(("parallel",)),
        name="norm_matmul",
    )(x, g.reshape(1, d), w)


def _rglru_body(u_in, gate, cw_ref, cb_ref, wr_ref, wi_ref, br_ref, bi_ref, lam_ref,
                out_ref, cnew_ref, hnew_ref, xbuf, hcar, bbuf, *, step, rows, pad):
    hist = (CONV_W - 1) * step

    if step == 1:
        u = u_in
        prev = xbuf[0:SUBLANES, :]
        sub8 = lax.broadcasted_iota(jnp.int32, (SUBLANES, W_A), 0)
        conv = cb_ref[...]
        for j in range(CONV_W):
            d = CONV_W - 1 - j
            if d == 0:
                ush = u
            else:
                rolled = pltpu.roll(u, d, 0)
                head = jnp.where(sub8 < d, pltpu.roll(prev, d, 0), rolled[0:SUBLANES, :])
                ush = jnp.concatenate([head, rolled[SUBLANES:, :]], axis=0)
            conv = conv + ush * cw_ref[j:j + 1, :]
        cnew_ref[0] = u[rows - hist:rows, :]
        xbuf[0:SUBLANES, :] = u[rows - SUBLANES:rows, :]
    else:
        xbuf[pad:pad + rows, :] = u_in
        conv = cb_ref[...] + xbuf[pad - hist:pad - hist + rows, :] * cw_ref[0:1, :]
        for j in range(1, CONV_W):
            o = pad - hist + j * step
            conv = conv + xbuf[o:o + rows, :] * cw_ref[j:j + 1, :]
        tail = xbuf[pad + rows - hist:pad + rows, :]
        cnew_ref[0] = tail
        xbuf[pad - hist:pad, :] = tail

    ub = conv.astype(BF16)
    r = _sigmoid(jnp.dot(ub, wr_ref[...], preferred_element_type=F32) + br_ref[...])
    ig = _sigmoid(jnp.dot(ub, wi_ref[...], preferred_element_type=F32) + bi_ref[...])
    log_a = (-LRU_C) * r * _softplus(-lam_ref[...])
    a = jnp.exp(log_a)
    bterm = jnp.sqrt(-jnp.tanh(log_a) * (a * a + 1.0)) * (ig * conv)
    bbuf[...] = bterm
    bbuf[0:step, :] = bterm[0:step, :] + a[0:step, :] * hcar[...]
    bv = bbuf[...]

    def scan_levels(a, bv, pos, first, count):
        d = first
        while d < first * count:
            m = pos >= d
            bv = jnp.where(m, a * pltpu.roll(bv, d, 0) + bv, bv)
            a = jnp.where(m, a * pltpu.roll(a, d, 0), a)
            d *= 2
        return a, bv

    row = lax.broadcasted_iota(jnp.int32, (rows, W_A), 0)
    a, bv = scan_levels(a, bv, row, step, rows // step)
    hcar[...] = bv[rows - step:rows, :]
    hnew_ref[0] = bv[rows - step:rows, :]
    out_ref[...] = jax.nn.gelu(gate) * bv


def _rwkv_prep_body(pf, mu_ref, w0_ref, a0_ref, w2_ref, a2_ref, g2_ref, kkp_ref, kap_ref,
                    r_ref, km_ref, v_ref, lw_ref, kk_ref, bb_ref, g_ref, snew_ref, pbuf,
                    *, step, rows, pad):

    if step == 1:
        rolled = pltpu.roll(pf, 1, 0)
        sub8 = lax.broadcasted_iota(jnp.int32, (SUBLANES, P_B), 0)
        head = jnp.where(sub8 < 1, pltpu.roll(pbuf[0:SUBLANES, :], 1, 0), rolled[0:SUBLANES, :])
        shifted = jnp.concatenate([head, rolled[SUBLANES:, :]], axis=0)
        pbuf[0:SUBLANES, :] = pf[rows - SUBLANES:rows, :]
    else:
        pbuf[pad:pad + rows, :] = pf
        shifted = pbuf[pad - step:pad - step + rows, :]
        pbuf[pad - step:pad, :] = pf[rows - step:rows, :]
    snew_ref[0] = pf[rows - step:rows, :]

    m = pf + (shifted - pf) * mu_ref[...]
    o1 = 3 * W_B
    r_ref[...] = m[:, 0:W_B]
    kraw = m[:, W_B:2 * W_B]
    v_ref[...] = m[:, 2 * W_B:o1]
    xwa = m[:, o1:o1 + LORA_W + LORA_A]
    xg = m[:, o1 + LORA_W + LORA_A:]
    lane = lax.broadcasted_iota(jnp.int32, xwa.shape, 1)
    lhs = jnp.where(lane < LORA_W, jnp.tanh(xwa), xwa).astype(BF16)
    lw = jnp.dot(lhs, w2_ref[...], preferred_element_type=F32)
    la = jnp.dot(lhs, a2_ref[...], preferred_element_type=F32)
    w_log = -_softplus(-(w0_ref[...] + lw)) - 0.5
    lw_ref[...] = -jnp.exp(w_log)
    a = _sigmoid(a0_ref[...] + la)
    g_ref[...] = jnp.dot(_sigmoid(xg).astype(BF16), g2_ref[...], preferred_element_type=F32)
    kk = kraw * kkp_ref[...]
    kk = kk * lax.rsqrt(jnp.maximum(_head_sum(kk * kk, _head_ones(W_B)), 1e-24))
    kk_ref[...] = kk
    bb_ref[...] = kk * a
    km_ref[...] = kraw * (1.0 + (a - 1.0) * kap_ref[...])


N_RGLRU_IN = 9
N_PREP_IN = 9


def _front_kernel(*refs, step, rows, pad_a, pad_b, nsplit):
    x_ref, gn_ref, w_ref = refs[:3]
    rg_in = refs[3:3 + N_RGLRU_IN]
    pp_in = refs[3 + N_RGLRU_IN:3 + N_RGLRU_IN + N_PREP_IN]
    outs = refs[3 + N_RGLRU_IN + N_PREP_IN:]
    rg_out, pp_out = outs[:3], outs[3:11]
    xbuf, hcar, bbuf, pbuf, p_scr = outs[11:]
    part = rows // nsplit

    @pl.when(pl.program_id(1) == 0)
    def _():
        hist = (CONV_W - 1) * step
        xbuf[pad_a - hist:pad_a, :] = rg_in[0][0]
        hcar[...] = rg_in[1][0]
        pbuf[pad_b - step:pad_b, :] = pp_in[0][0]

    xn = _rms(x_ref[...], gn_ref[...]).astype(BF16)
    for s in range(nsplit):
        p_scr[s] = jnp.dot(xn[s * part:(s + 1) * part, :], w_ref[...],
                           preferred_element_type=F32)
    for s in range(nsplit):
        sub = lambda ref, s=s: ref.at[s * part:(s + 1) * part, :]
        _rglru_body(p_scr[s, :, 0:W_A], p_scr[s, :, W_A:P_A], *rg_in[2:],
                    sub(rg_out[0]), rg_out[1], rg_out[2], xbuf, hcar, bbuf,
                    step=step, rows=part, pad=pad_a)
        _rwkv_prep_body(p_scr[s, :, P_A:], *pp_in[1:], *[sub(r) for r in pp_out[:7]],
                        pp_out[7], pbuf, step=step, rows=part, pad=pad_b)


def _front(x, gn, w, conv_st, h_st, cw, cb, wr, wi, br, bi, lam,
           prev, mu, w0, a0, w2p, a2p, g2, kkp, kap, *, nseq, step, rows):
    n, d = x.shape
    nt = n // (nseq * rows)
    hist = (CONV_W - 1) * step
    pad_a = max(SUBLANES, hist)
    pad_b = max(SUBLANES, step)
    row_map = lambda b, t: (b * nt + t, 0)
    seq_map = lambda b, t: (b, 0, 0)
    whole = lambda a: pl.BlockSpec(a.shape, lambda b, t: (0,) * a.ndim,
                                   pipeline_mode=pl.Buffered(1))
    va = lambda p: p.reshape(1, W_A)
    vb = lambda p: p.reshape(1, W_B)
    consts_a = [cw, va(cb), wr, wi, va(br), va(bi), va(lam)]
    consts_b = [mu.reshape(1, P_B), vb(w0), vb(a0), w2p, a2p, g2, vb(kkp), vb(kap)]
    nsplit = 2 if step == 1 else 1
    part = rows // nsplit
    kern = functools.partial(_front_kernel, step=step, rows=rows, pad_a=pad_a, pad_b=pad_b,
                             nsplit=nsplit)
    tile = lambda wd: pl.BlockSpec((rows, wd), row_map)
    return pl.pallas_call(
        kern,
        grid=(nseq, nt),
        in_specs=[tile(d), whole(gn.reshape(1, d)), whole(w),
                  pl.BlockSpec((1, hist, W_A), seq_map), pl.BlockSpec((1, step, W_A), seq_map)]
                 + [whole(a) for a in consts_a]
                 + [pl.BlockSpec((1, step, P_B), seq_map)] + [whole(a) for a in consts_b],
        out_specs=[tile(W_A), pl.BlockSpec((1, hist, W_A), seq_map),
                   pl.BlockSpec((1, step, W_A), seq_map)]
                  + [tile(W_B)] * 7 + [pl.BlockSpec((1, step, P_B), seq_map)],
        out_shape=[jax.ShapeDtypeStruct((n, W_A), F32),
                   jax.ShapeDtypeStruct((nseq, hist, W_A), F32),
                   jax.ShapeDtypeStruct((nseq, step, W_A), F32)]
                  + [jax.ShapeDtypeStruct((n, W_B), F32)] * 7
                  + [jax.ShapeDtypeStruct((nseq, step, P_B), F32)],
        scratch_shapes=[pltpu.VMEM((pad_a + part, W_A), F32),
                        pltpu.VMEM((step, W_A), F32),
                        pltpu.VMEM((part, W_A), F32),
                        pltpu.VMEM((pad_b + part, P_B), F32),
                        pltpu.VMEM((nsplit, part, P_A + P_B), F32)],
        compiler_params=_cparams(("parallel", "arbitrary")),
        name="front",
    )(x, gn.reshape(1, d), w, conv_st, h_st, *consts_a, prev, *consts_b)


VGROUP = 4


def _rwkv_scan_kernel(r_ref, km_ref, v_ref, lw_ref, kk_ref, bb_ref, s0_ref, o_ref, sout_ref,
                      s_scr, *, tb):
    tblk = pl.program_id(1)

    @pl.when(tblk == 0)
    def _():
        s_scr[...] = s0_ref[...]

    def colsum(x):
        return jnp.sum(x, axis=0, keepdims=True)

    sub = lax.broadcasted_iota(jnp.int32, (VGROUP, LANES), 0)
    for g in range(HD_B // VGROUP):
        vbase = g * VGROUP

        def step(t, state, vbase=vbase):
            kk = kk_ref[t]
            w = jnp.exp(lw_ref[t])
            bvec = bb_ref[t]
            km = km_ref[t]
            r = r_ref[t]
            vt = v_ref[t, vbase:vbase + VGROUP, :]
            new = []
            otile = jnp.zeros((VGROUP, LANES), F32)
            for i in range(VGROUP):
                sk = colsum(state[i] * kk)
                sv = state[i] * w - sk * bvec + vt[i:i + 1, :] * km
                new.append(sv)
                otile = jnp.where(sub == i, colsum(sv * r), otile)
            o_ref[t, vbase:vbase + VGROUP, :] = otile
            return tuple(new)

        state = tuple(s_scr[vbase + i] for i in range(VGROUP))
        state = lax.fori_loop(0, tb, step, state, unroll=2)
        for i in range(VGROUP):
            s_scr[vbase + i] = state[i]

    sout_ref[...] = s_scr[...]


def _rwkv_scan(r, km, v, lw, kk, bb, s0, *, tb):
    nt, _, nl = r.shape
    seq = pl.BlockSpec((tb, HD_B, LANES), lambda g, t: (t, 0, g))
    st = pl.BlockSpec((HD_B, HD_B, LANES), lambda g, t: (0, 0, g))
    kern = functools.partial(_rwkv_scan_kernel, tb=tb)
    return pl.pallas_call(
        kern,
        grid=(nl // LANES, nt // tb),
        in_specs=[seq] * 6 + [st],
        out_specs=[seq, st],
        out_shape=[jax.ShapeDtypeStruct((nt, HD_B, nl), F32),
                   jax.ShapeDtypeStruct((HD_B, HD_B, nl), F32)],
        scratch_shapes=[pltpu.VMEM((HD_B, HD_B, LANES), F32)],
        compiler_params=_cparams(("parallel", "arbitrary")),
        name="rwkv_scan",
    )(r, km, v, lw, kk, bb, s0)


PAIR = 2 * HD_B
RWKV_SEQS_PER_STEP = 4


def _rwkv_chunk_kernel(r_ref, km_ref, v_ref, lw_ref, kk_ref, bb_ref, s0_ref, o_ref, sout_ref,
                       s_scr, *, chunk, nb):
    c = pl.program_id(1)

    @pl.when(c == 0)
    def _():
        s_scr[...] = s0_ref[...]

    lane = lax.broadcasted_iota(jnp.int32, (chunk, PAIR), 1)
    row = lax.broadcasted_iota(jnp.int32, (chunk, PAIR), 0)
    low_half = lane < HD_B
    li = jnp.where(low_half, lane, lane - HD_B)
    strict = li < row
    incl = li <= row
    eye = jnp.where(li == row, 1.0, 0.0)
    same = lambda n: (li // n) == (row // n)
    m8 = same(8)
    levels = []
    n = 16
    prev = m8
    while n <= chunk:
        cur = same(n)
        levels.append(jnp.logical_and(cur, jnp.logical_not(prev)))
        prev = cur
        n *= 2
    sq_r = lax.broadcasted_iota(jnp.int32, (PAIR, PAIR), 0)
    sq_c = lax.broadcasted_iota(jnp.int32, (PAIR, PAIR), 1)
    same_head = (sq_r < HD_B) == (sq_c < HD_B)
    low_bf = lax.broadcasted_iota(jnp.int32, (chunk, PAIR), 1) < HD_B

    def bd(s):
        return tuple(jnp.concatenate([jnp.where(low_bf, x, jnp.zeros_like(x)),
                                      jnp.where(low_bf, jnp.zeros_like(x), x)], axis=0)
                     for x in s)

    pm = lambda x, y: _mm(_split(x), bd(_split(y)))

    chains = [(b, p) for b in range(nb) for p in range(H_B // 2)]
    each = lambda f: [f(i) for i in range(len(chains))]

    vv, lhs2, kg, bg, kdbd, etot, s_p = [], [], [], [], [], [], []
    for b, p in chains:
        sl = slice(p * PAIR, (p + 1) * PAIR)
        r = r_ref[b, :, sl]
        km = km_ref[b, :, sl]
        lw = lw_ref[b, :, sl]
        kk = kk_ref[b, :, sl]
        bb = bb_ref[b, :, sl]
        cum = lw
        d = 1
        while d < chunk:
            cum = cum + jnp.where(row >= d, pltpu.roll(cum, d, 0), 0.0)
            d *= 2
        tot = cum[chunk - 1:chunk, :]
        g_end = jnp.exp(tot - cum)
        g_inv = jnp.exp(-cum)
        vv.append(v_ref[b, :, sl])
        lhs2.append(_split(jnp.concatenate([kk * jnp.exp(cum - lw), r * jnp.exp(cum)], axis=0)))
        kg.append(bd(_split(km * g_inv)))
        bg.append(bd(_split(bb * g_inv)))
        kdbd.append(_split(jnp.concatenate([km * g_end, bb * g_end], axis=0)))
        etot.append(jnp.exp(tot))
        s_p.append(s_scr[b, p])

    ss = each(lambda i: _mm(lhs2[i], _split(s_p[i]), nt=True))
    kb = each(lambda i: _mm(lhs2[i], tuple(jnp.concatenate([x, y], axis=0)
                                           for x, y in zip(kg[i], bg[i])), nt=True))
    a_b = each(lambda i: jnp.where(strict, kb[i][:chunk, PAIR:], 0.0))
    b_b = each(lambda i: jnp.where(incl, kb[i][chunk:, PAIR:], 0.0))
    av = each(lambda i: _mm(_split(jnp.concatenate(
        [jnp.where(strict, kb[i][:chunk, :PAIR], 0.0), jnp.where(incl, kb[i][chunk:, :PAIR], 0.0)],
        axis=0)), bd(_split(vv[i]))))

    dg = each(lambda i: jnp.where(m8, a_b[i], 0.0))
    dg_s = each(lambda i: _split(dg[i]))
    d2 = each(lambda i: _mm(dg_s[i], bd(dg_s[i])))
    d2_s = each(lambda i: _split(d2[i]))
    d2_b = each(lambda i: bd(d2_s[i]))
    d3 = each(lambda i: _mm(dg_s[i], d2_b[i]))
    d4 = each(lambda i: _mm(d2_s[i], d2_b[i]))
    t1 = each(lambda i: eye - dg[i] + d2[i] - d3[i])
    inv = each(lambda i: t1[i] + pm(t1[i], d4[i]))
    for m in levels:
        mlev = each(lambda i: pm(inv[i], jnp.where(m, a_b[i], 0.0)))
        inv = each(lambda i: inv[i] - pm(mlev[i], inv[i]))

    u = each(lambda i: pm(inv[i], ss[i][:chunk] + av[i][:chunk]))
    upd = each(lambda i: _mm(_split(jnp.transpose(jnp.concatenate([vv[i], -u[i]], axis=0))),
                             kdbd[i]))
    bu = each(lambda i: pm(b_b[i], u[i]))
    for i, (b, p) in enumerate(chains):
        sl = slice(p * PAIR, (p + 1) * PAIR)
        o_ref[b, :, sl] = ss[i][chunk:] + av[i][chunk:] - bu[i]
        s_scr[b, p] = s_p[i] * etot[i] + jnp.where(same_head, upd[i], 0.0)

    sout_ref[...] = s_scr[...]


def _rwkv_chunk(r, km, v, lw, kk, bb, s0, *, chunk, nb):
    nseq, t, _ = r.shape
    assert chunk == HD_B and nseq % nb == 0
    npair = H_B // 2
    row = pl.BlockSpec((nb, chunk, W_B), lambda b, c: (b, c, 0))
    st = pl.BlockSpec((nb, npair, PAIR, PAIR), lambda b, c: (b, 0, 0, 0))
    kern = functools.partial(_rwkv_chunk_kernel, chunk=chunk, nb=nb)
    return pl.pallas_call(
        kern,
        grid=(nseq // nb, t // chunk),
        in_specs=[row] * 6 + [st],
        out_specs=[row, st],
        out_shape=[jax.ShapeDtypeStruct((nseq, t, W_B), F32),
                   jax.ShapeDtypeStruct((nseq, npair, PAIR, PAIR), F32)],
        scratch_shapes=[pltpu.VMEM((nb, npair, PAIR, PAIR), F32)],
        compiler_params=_cparams(("parallel", "arbitrary")),
        name="rwkv_chunk",
    )(r, km, v, lw, kk, bb, s0)


MXU_N = 256


def _mix_residual(x_ref, oa_ref, o_ref, r_ref, km_ref, v_ref, g_ref, lnw_ref, lnb_ref, rk_ref,
                  wa_ref, wb_ref):
    ones = _head_ones(W_B)
    o = o_ref[...]
    cen = o - _head_sum(o, ones) * (1.0 / HD_B)
    var = _head_sum(cen * cen, ones) * (1.0 / HD_B)
    on = cen * lax.rsqrt(var + GN_EPS) * lnw_ref[...] + lnb_ref[...]
    bonus = _head_sum(r_ref[...] * km_ref[...] * rk_ref[...], ones) * v_ref[...]
    ob = (on + bonus) * g_ref[...]
    return (x_ref[...]
            + jnp.dot(oa_ref[...].astype(BF16), wa_ref[...], preferred_element_type=F32)
            + jnp.dot(ob.astype(BF16), wb_ref[...], preferred_element_type=F32))


def _proj_residual(x_ref, o_ref, w_ref):
    return x_ref[...] + jnp.dot(o_ref[...].astype(BF16), w_ref[...],
                                preferred_element_type=F32)


def _res_ffn_kernel(*refs, residual, n_res, final_norm):
    g_ref, wg_ref, wu_ref, wd_ref, gf_ref, y_ref, xf_scr, x_scr, h_scr = refs[n_res:]
    x = residual(*refs[:n_res])
    xf_scr[...] = _rms(x, g_ref[...]).astype(BF16)
    x_scr[...] = x
    for j in range(wg_ref.shape[1] // MXU_N):
        cols = slice(j * MXU_N, (j + 1) * MXU_N)
        xf = xf_scr[...]
        hg = jnp.dot(xf, wg_ref[:, cols], preferred_element_type=F32)
        hu = jnp.dot(xf, wu_ref[:, cols], preferred_element_type=F32)
        h_scr[:, cols] = ((hg * _sigmoid(hg)) * hu).astype(BF16)
    y = x_scr[...] + jnp.dot(h_scr[...], wd_ref[...], preferred_element_type=F32)
    if final_norm:
        y = _rms(y, gf_ref[...])
    y_ref[...] = y


def _res_ffn(residual, row_ins, const_ins, g, wg, wu, wd, gf, *, tm, final_norm):
    n, d = row_ins[0].shape
    whole = lambda a: pl.BlockSpec(a.shape, lambda i: (0,) * a.ndim,
                                   pipeline_mode=pl.Buffered(1))
    consts = list(const_ins) + [g.reshape(1, d), wg, wu, wd, gf.reshape(1, d)]
    kern = functools.partial(_res_ffn_kernel, residual=residual,
                             n_res=len(row_ins) + len(const_ins), final_norm=final_norm)
    return pl.pallas_call(
        kern,
        grid=(n // tm,),
        in_specs=[pl.BlockSpec((tm, a.shape[1]), lambda i: (i, 0)) for a in row_ins]
                 + [whole(a) for a in consts],
        out_specs=pl.BlockSpec((tm, d), lambda i: (i, 0)),
        out_shape=jax.ShapeDtypeStruct((n, d), F32),
        scratch_shapes=[pltpu.VMEM((tm, d), BF16), pltpu.VMEM((tm, d), F32),
                        pltpu.VMEM((tm, wg.shape[1]), BF16)],
        compiler_params=_cparams(("parallel",)),
        name="res_ffn",
    )(*row_ins, *consts)


def _hgrn_kernel(q_ref, f_ref, i_ref, g_ref, s0_ref, lbc_ref, gn_ref, o_ref, sout_ref,
                 s_scr, *, layer, chunk, valid, nb):
    c = pl.program_id(1)

    @pl.when(c == 0)
    def _():
        s_scr[...] = s0_ref[...]

    lbc = lbc_ref[...]
    e = jnp.exp(lbc - jnp.max(lbc, axis=0, keepdims=True))
    sm = e / jnp.sum(e, axis=0, keepdims=True)
    lbf = sm[1:2, :]
    for l in range(2, layer + 1):
        lbf = lbf + sm[l:l + 1, :]

    nblk = chunk // SUBLANES
    sub = lax.broadcasted_iota(jnp.int32, (SUBLANES, DK_C), 0)
    lane_c = lax.broadcasted_iota(jnp.int32, (SUBLANES, chunk), 1)
    rowc = lax.broadcasted_iota(jnp.int32, (chunk, F_C), 0)
    tril = jnp.where(lax.broadcasted_iota(jnp.int32, (chunk, chunk), 0)
                     >= lax.broadcasted_iota(jnp.int32, (chunk, chunk), 1), 1.0, 0.0).astype(BF16)

    qs, ks, bs, ivs = [], [], [], []
    for b in range(nb):
        fr = f_ref[b]
        ez = jnp.exp(-jnp.abs(fr))
        rz = 1.0 / (1.0 + ez)
        sig = jnp.where(fr >= 0.0, rz, ez * rz)
        nsig = jnp.where(fr >= 0.0, ez * rz, rz)
        logf = jnp.log(lbf + (1.0 - lbf) * sig)
        kfull = (1.0 - lbf) * nsig
        if valid < chunk:
            logf = jnp.where(rowc < valid, logf, 0.0)
            kfull = jnp.where(rowc < valid, kfull, 0.0)
        hi, lo = _split(logf)
        bc = jnp.dot(jnp.concatenate([tril, tril], axis=1), jnp.concatenate([hi, lo], axis=0),
                     preferred_element_type=F32)
        qs.append(jax.nn.silu(q_ref[b]))
        ks.append(kfull)
        bs.append(bc)
        ivs.append(i_ref[b].astype(BF16))

    chains = [(b, h) for b in range(nb) for h in range(H_C)]
    each = lambda f: [f(i) for i in range(len(chains))]
    hsl = lambda h: slice(h * DK_C, (h + 1) * DK_C)
    q = each(lambda i: qs[chains[i][0]][:, hsl(chains[i][1])])
    k = each(lambda i: ks[chains[i][0]][:, hsl(chains[i][1])])
    bcs = each(lambda i: bs[chains[i][0]][:, hsl(chains[i][1])])
    iv = each(lambda i: ivs[chains[i][0]][:, hsl(chains[i][1])])
    s_h = each(lambda i: s_scr[chains[i][0], chains[i][1]])

    def off_diag(i, tb):
        lo = tb * SUBLANES
        rs = slice(lo, lo + SUBLANES)
        edge = bcs[i][lo - 1:lo, :]
        qi = q[i][rs, :] * jnp.exp(bcs[i][rs, :] - edge)
        ki = k[i][0:lo, :] * jnp.exp(edge - bcs[i][0:lo, :])
        ki = jnp.concatenate([ki, jnp.zeros((chunk - lo, DK_C), F32)], axis=0)
        return lax.dot_general(qi.astype(BF16), ki.astype(BF16), (((1,), (1,)), ((), ())),
                               preferred_element_type=F32)

    offd = each(lambda i: [off_diag(i, tb) for tb in range(1, nblk)])
    inter = each(lambda i: jnp.dot((q[i] * jnp.exp(bcs[i])).astype(BF16), s_h[i].astype(BF16),
                                   preferred_element_type=F32))

    def diag(i, tb):
        rs = slice(tb * SUBLANES, (tb + 1) * SUBLANES)
        blk = jnp.zeros((SUBLANES, chunk), F32) if tb == 0 else offd[i][tb - 1]
        for j in range(SUBLANES):
            s = tb * SUBLANES + j
            if s >= valid:
                break
            diff = jnp.where(sub >= j, bcs[i][rs, :] - bcs[i][s:s + 1, :], -jnp.inf)
            ev = jnp.exp(diff) * (q[i][rs, :] * k[i][s:s + 1, :])
            blk = jnp.where(lane_c == s, jnp.sum(ev, axis=1, keepdims=True), blk)
        return blk

    def attend(i):
        blks = [diag(i, tb) for tb in range(nblk)]
        att = jnp.concatenate(blks, axis=0) if nblk > 1 else blks[0]
        return jnp.dot(att.astype(BF16), iv[i], preferred_element_type=F32) + inter[i]

    o_h = each(attend)

    def new_state(i):
        bl = bcs[i][chunk - 1:chunk, :]
        kd = k[i] * jnp.exp(bl - bcs[i])
        pieces = [kd, jnp.broadcast_to(jnp.exp(bl), (SUBLANES, DK_C))]
        fill = DK_C - chunk - SUBLANES
        if fill > 0:
            pieces.append(jnp.zeros((fill, DK_C), F32))
        xt = jnp.transpose(jnp.concatenate(pieces, axis=0))
        return (xt[:, chunk:chunk + 1] * s_h[i]
                + jnp.dot(xt[:, 0:chunk].astype(BF16), iv[i], preferred_element_type=F32))

    s_new = each(new_state)
    for i, (b, h) in enumerate(chains):
        s_scr[b, h] = s_new[i]
    for b in range(nb):
        o = jnp.concatenate([o_h[b * H_C + h] for h in range(H_C)], axis=1)
        o_ref[b] = _rms(o, gn_ref[...]) * jax.nn.silu(g_ref[b])
    sout_ref[...] = s_scr[...]


def _hgrn(pc3, s0, lbc, gn, *, layer, chunk, valid, nb):
    nseq, t, _ = pc3.shape
    assert nseq % nb == 0
    col = lambda j: pl.BlockSpec((nb, chunk, F_C), lambda b, c, j=j: (b, c, j))
    st = pl.BlockSpec((nb, H_C, DK_C, DK_C), lambda b, c: (b, 0, 0, 0))
    kern = functools.partial(_hgrn_kernel, layer=layer, chunk=chunk, valid=valid, nb=nb)
    return pl.pallas_call(
        kern,
        grid=(nseq // nb, t // chunk),
        in_specs=[col(0), col(1), col(2), col(3), st,
                  pl.BlockSpec(lbc.shape, lambda b, c: (0, 0)),
                  pl.BlockSpec((1, D_MODEL), lambda b, c: (0, 0))],
        out_specs=[pl.BlockSpec((nb, chunk, D_MODEL), lambda b, c: (b, c, 0)), st],
        out_shape=[jax.ShapeDtypeStruct((nseq, t, D_MODEL), F32),
                   jax.ShapeDtypeStruct(s0.shape, F32)],
        scratch_shapes=[pltpu.VMEM((nb, H_C, DK_C, DK_C), F32)],
        compiler_params=_cparams(("parallel", "arbitrary")),
        name="hgrn",
    )(pc3, pc3, pc3, pc3, s0, lbc, gn.reshape(1, D_MODEL))


def _block_diag(w):
    h, n, _ = w.shape
    eye = jnp.eye(h, dtype=w.dtype)
    return (eye[:, None, :, None] * w[:, :, None, :]).reshape(h * n, h * n)


def _fold(x):
    half = x.shape[-2] // 2
    return jnp.concatenate([x[..., :half, :], x[..., half:, :]], axis=-1)


def _unfold(x):
    half = x.shape[-1] // 2
    return jnp.concatenate([x[..., :half], x[..., half:]], axis=-2)


def _trunk(x2d, conv_st, h_st, shift_st, rs_st, hs_st, P, *, nbatch, nsteps, time_major):
    n = x2d.shape[0]
    if time_major:
        nseq, step, rows = 1, nbatch, n
        tm = n
    else:
        nseq, step, rows = nbatch, 1, 512
        tm = 512
    nchain = nbatch * H_B

    out_a, conv_new, h_new, r, km, v, lw, kk, bb, g, shift_new = _front(
        x2d, P['ln_mix'][0], P['w_in_ab'],
        conv_st, h_st, P['conv_w'], P['conv_b'], P['wr'], P['wi'], P['gr_b'], P['gi_b'],
        P['lru_lambda'],
        shift_st, P['mu_b'], P['w0_b'], P['a0_b'], P['w2p'], P['a2p'], P['g2_b'],
        P['kk_b'], P['ka_b'], nseq=nseq, step=step, rows=rows)

    if time_major:
        def to_lanes(z):
            z = z.reshape(nsteps, nbatch, H_B, HD_B).transpose(0, 3, 1, 2)
            return z.reshape(nsteps, HD_B, nchain)

        s0 = rs_st.transpose(2, 3, 0, 1).reshape(HD_B, HD_B, nchain)
        o, s_new = _rwkv_scan(to_lanes(r), to_lanes(km), to_lanes(v), to_lanes(lw), to_lanes(kk),
                              to_lanes(bb), s0, tb=nsteps)
        o = o.reshape(nsteps, HD_B, nbatch, H_B).transpose(0, 2, 3, 1).reshape(n, W_B)
        rs_new = s_new.reshape(HD_B, HD_B, nbatch, H_B).transpose(2, 3, 0, 1)
    else:
        z = jnp.zeros_like(rs_st[:, 0::2])
        s0 = jnp.concatenate([jnp.concatenate([rs_st[:, 0::2], z], axis=-1),
                              jnp.concatenate([z, rs_st[:, 1::2]], axis=-1)], axis=-2)
        seq3 = lambda y: y.reshape(nbatch, nsteps, W_B)
        o, s_new = _rwkv_chunk(seq3(r), seq3(km), seq3(v), seq3(lw), seq3(kk), seq3(bb), s0,
                               chunk=HD_B, nb=RWKV_SEQS_PER_STEP)
        o = o.reshape(n, W_B)
        rs_new = jnp.stack([s_new[:, :, :HD_B, :HD_B], s_new[:, :, HD_B:, HD_B:]],
                           axis=2).reshape(nbatch, H_B, HD_B, HD_B)

    vec = lambda p: p.reshape(1, W_B)
    x2 = _res_ffn(_mix_residual, [x2d, out_a, o, r, km, v, g],
                  [vec(P['lnx_w']), vec(P['lnx_b']), vec(P['rk_b']), P['w_out_a'], P['w_out_b']],
                  P['ln_ffn'][0], P['ffn_gate'][0], P['ffn_up'][0], P['ffn_down'][0],
                  P['ln_final'], tm=tm, final_norm=False)

    (pc,) = _norm_matmul(x2, P['ln_mix'][1], P['w_in_c'], (4 * F_C,), tm)
    if time_major:
        pc3 = pc.reshape(nsteps, nbatch, 4 * F_C).transpose(1, 0, 2)
        chunk = SUBLANES
        pc3 = jnp.pad(pc3, ((0, 0), (0, chunk - nsteps), (0, 0)))
    else:
        pc3 = pc.reshape(nbatch, nsteps, 4 * F_C)
        chunk = 64
    o3, hs_new = _hgrn(pc3, hs_st, P['lb_c'], P['gn_c'], layer=1, chunk=chunk,
                       valid=min(chunk, nsteps), nb=4 if time_major else 1)
    if time_major:
        o = o3[:, :nsteps].transpose(1, 0, 2).reshape(n, D_MODEL)
    else:
        o = o3.reshape(n, D_MODEL)
    y = _res_ffn(_proj_residual, [x2, o], [P['w_out_c']],
                 P['ln_ffn'][1], P['ffn_gate'][1], P['ffn_up'][1], P['ffn_down'][1],
                 P['ln_final'], tm=tm, final_norm=True)
    return y, conv_new, h_new, shift_new, rs_new, hs_new


def kernel(x_prompt, x_sample, state_rglru_conv, state_rglru_h, state_rwkv_shift, state_rwkv_S,
           state_hgrn_S, ln_mix, ln_ffn, ln_final, w_in_ab, conv_w, conv_b, gr_w, gr_b, gi_w,
           gi_b, lru_lambda, mu_b, w0_b, w2_b, a0_b, a2_b, g2_b, kk_b, ka_b, rk_b, lnx_w, lnx_b,
           w_out_ab, w_in_c, lb_c, gn_c, w_out_c, ffn_gate, ffn_up, ffn_down):
    bp, tp, _ = x_prompt.shape
    bs, ts, _ = x_sample.shape
    zpad_w = jnp.zeros((LORA_A, W_B), F32)
    zpad_a = jnp.zeros((LORA_W, W_B), F32)
    P = dict(
        ln_mix=ln_mix, ln_ffn=ln_ffn, ln_final=ln_final,
        w_in_ab=w_in_ab[0].astype(BF16), conv_w=conv_w[0], conv_b=conv_b[0],
        wr=_block_diag(gr_w[0]).astype(BF16), wi=_block_diag(gi_w[0]).astype(BF16),
        gr_b=gr_b[0], gi_b=gi_b[0], lru_lambda=lru_lambda[0], mu_b=mu_b[0], w0_b=w0_b[0],
        a0_b=a0_b[0],
        w2p=jnp.concatenate([w2_b[0], zpad_w], axis=0).astype(BF16),
        a2p=jnp.concatenate([zpad_a, a2_b[0]], axis=0).astype(BF16),
        g2_b=g2_b[0].astype(BF16), kk_b=kk_b[0], ka_b=ka_b[0], rk_b=rk_b[0],
        lnx_w=lnx_w[0], lnx_b=lnx_b[0],
        w_out_a=w_out_ab[0, :W_A].astype(BF16), w_out_b=w_out_ab[0, W_A:].astype(BF16),
        w_in_c=w_in_c[0].astype(BF16), lb_c=lb_c, gn_c=gn_c[0],
        w_out_c=w_out_c[0].astype(BF16), ffn_gate=ffn_gate.astype(BF16),
        ffn_up=ffn_up.astype(BF16), ffn_down=ffn_down.astype(BF16))

    yp, p_conv, p_h, p_shift, p_rs, p_hs = _trunk(
        x_prompt.reshape(bp * tp, D_MODEL),
        jnp.zeros((bp, CONV_W - 1, W_A), F32), jnp.zeros((bp, 1, W_A), F32),
        jnp.zeros((bp, 1, P_B), F32), jnp.zeros((bp, H_B, HD_B, HD_B), F32),
        jnp.zeros((bp, H_C, DK_C, DK_C), F32), P, nbatch=bp, nsteps=tp, time_major=False)

    ys, s_conv, s_h, s_shift, s_rs, s_hs = _trunk(
        x_sample.transpose(1, 0, 2).reshape(ts * bs, D_MODEL),
        state_rglru_conv[0].transpose(1, 0, 2).reshape(1, (CONV_W - 1) * bs, W_A),
        state_rglru_h[0].reshape(1, bs, W_A), state_rwkv_shift[0].reshape(1, bs, P_B),
        state_rwkv_S[0], state_hgrn_S[0], P, nbatch=bs, nsteps=ts, time_major=True)

    return (yp.reshape(bp, tp, D_MODEL),
            ys.reshape(ts, bs, D_MODEL).transpose(1, 0, 2),
            p_conv[None], p_h.reshape(1, bp, W_A), p_shift.reshape(1, bp, P_B), p_rs[None],
            p_hs[None],
            s_conv.reshape(CONV_W - 1, bs, W_A).transpose(1, 0, 2)[None],
            s_h.reshape(1, bs, W_A), s_shift.reshape(1, bs, P_B), s_rs[None], s_hs[None])
```

```python
import functools

import jax
import jax.numpy as jnp
from jax import lax
from jax.experimental import pallas as pl
from jax.experimental.pallas import tpu as pltpu

F32 = jnp.float32
BF16 = jnp.bfloat16

D_MODEL = 1024
W_A = 512
H_A = 8
CONV_W = 4
LRU_C = 8.0
W_B = 512
HD_B = 64
H_B = 8
LORA_W = 64
LORA_A = 64
LORA_G = 128
P_A = 2 * W_A
P_B = 3 * W_B + LORA_W + LORA_A + LORA_G
DK_C = 128
H_C = 8
F_C = 1024
D_FF = 2816
RMS_EPS = 1e-6
GN_EPS = 64e-5

LANES = 128
SUBLANES = 8
VMEM_LIMIT = 56 * 1024 * 1024


def _cparams(sem):
    return pltpu.CompilerParams(dimension_semantics=sem, vmem_limit_bytes=VMEM_LIMIT)


def _softplus(x):
    return jnp.maximum(x, 0.0) + jnp.log1p(jnp.exp(-jnp.abs(x)))


def _sigmoid(x):
    return 1.0 / (1.0 + jnp.exp(-x))


def _rms(x, g):
    ms = jnp.mean(x * x, axis=-1, keepdims=True)
    return x * lax.rsqrt(ms + RMS_EPS) * g


def _dot(a, b):
    return jnp.dot(a.astype(BF16), b.astype(BF16), preferred_element_type=F32)


def _split(x):
    hi = x.astype(BF16)
    lo = (x - hi.astype(F32)).astype(BF16)
    return hi, lo


def _mm(a, b, nt=False):
    ah, al = a
    bh, bl = b
    lhs = jnp.concatenate([ah, al, ah], axis=1)
    if nt:
        rhs = jnp.concatenate([bh, bh, bl], axis=1)
        return lax.dot_general(lhs, rhs, (((1,), (1,)), ((), ())), preferred_element_type=F32)
    rhs = jnp.concatenate([bh, bh, bl], axis=0)
    return jnp.dot(lhs, rhs, preferred_element_type=F32)


def _head_ones(width):
    r = lax.broadcasted_iota(jnp.int32, (width, width), 0) // HD_B
    c = lax.broadcasted_iota(jnp.int32, (width, width), 1) // HD_B
    return jnp.where(r == c, 1.0, 0.0).astype(BF16)


def _head_sum(x, ones):
    hi, lo = _split(x)
    return jnp.dot(jnp.concatenate([hi, lo], axis=1), jnp.concatenate([ones, ones], axis=0),
                   preferred_element_type=F32)


def _norm_matmul_kernel(x_ref, g_ref, w_ref, *o_refs):
    xn = _rms(x_ref[...], g_ref[...]).astype(BF16)
    off = 0
    for o_ref in o_refs:
        n = o_ref.shape[-1]
        o_ref[...] = jnp.dot(xn, w_ref[:, off:off + n], preferred_element_type=F32)
        off += n


def _norm_matmul(x, g, w, splits, tm):
    n, d = x.shape
    p = w.shape[1]
    assert sum(splits) == p and n % tm == 0
    return pl.pallas_call(
        _norm_matmul_kernel,
        grid=(n // tm,),
        in_specs=[pl.BlockSpec((tm, d), lambda i: (i, 0)),
                  pl.BlockSpec((1, d), lambda i: (0, 0)),
                  pl.BlockSpec((d, p), lambda i: (0, 0))],
        out_specs=[pl.BlockSpec((tm, s), lambda i: (i, 0)) for s in splits],
        out_shape=[jax.ShapeDtypeStruct((n, s), F32) for s in splits],
        compiler_params=_cparams(("parallel",)),
        name="norm_matmul",
    )(x, g.reshape(1, d), w)


def _rglru_body(u_in, gate, cw_ref, cb_ref, wr_ref, wi_ref, br_ref, bi_ref, lam_ref,
                out_ref, cnew_ref, hnew_ref, xbuf, hcar, bbuf, *, step, rows, pad):
    hist = (CONV_W - 1) * step

    if step == 1:
        u = u_in
        prev = xbuf[0:SUBLANES, :]
        sub8 = lax.broadcasted_iota(jnp.int32, (SUBLANES, W_A), 0)
        conv = cb_ref[...]
        for j in range(CONV_W):
            d = CONV_W - 1 - j
            if d == 0:
                ush = u
            else:
                rolled = pltpu.roll(u, d, 0)
                head = jnp.where(sub8 < d, pltpu.roll(prev, d, 0), rolled[0:SUBLANES, :])
                ush = jnp.concatenate([head, rolled[SUBLANES:, :]], axis=0)
            conv = conv + ush * cw_ref[j:j + 1, :]
        cnew_ref[0] = u[rows - hist:rows, :]
        xbuf[0:SUBLANES, :] = u[rows - SUBLANES:rows, :]
    else:
        xbuf[pad:pad + rows, :] = u_in
        conv = cb_ref[...] + xbuf[pad - hist:pad - hist + rows, :] * cw_ref[0:1, :]
        for j in range(1, CONV_W):
            o = pad - hist + j * step
            conv = conv + xbuf[o:o + rows, :] * cw_ref[j:j + 1, :]
        tail = xbuf[pad + rows - hist:pad + rows, :]
        cnew_ref[0] = tail
        xbuf[pad - hist:pad, :] = tail

    ub = conv.astype(BF16)
    r = _sigmoid(jnp.dot(ub, wr_ref[...], preferred_element_type=F32) + br_ref[...])
    ig = _sigmoid(jnp.dot(ub, wi_ref[...], preferred_element_type=F32) + bi_ref[...])
    log_a = (-LRU_C) * r * _softplus(-lam_ref[...])
    a = jnp.exp(log_a)
    bterm = jnp.sqrt(-jnp.tanh(log_a) * (a * a + 1.0)) * (ig * conv)
    bbuf[...] = bterm
    bbuf[0:step, :] = bterm[0:step, :] + a[0:step, :] * hcar[...]
    bv = bbuf[...]

    def scan_levels(a, bv, pos, first, count):
        d = first
        while d < first * count:
            m = pos >= d
            bv = jnp.where(m, a * pltpu.roll(bv, d, 0) + bv, bv)
            a = jnp.where(m, a * pltpu.roll(a, d, 0), a)
            d *= 2
        return a, bv

    row = lax.broadcasted_iota(jnp.int32, (rows, W_A), 0)
    a, bv = scan_levels(a, bv, row, step, rows // step)
    hcar[...] = bv[rows - step:rows, :]
    hnew_ref[0] = bv[rows - step:rows, :]
    out_ref[...] = jax.nn.gelu(gate) * bv


def _rwkv_prep_body(pf, mu_ref, w0_ref, a0_ref, w2_ref, a2_ref, g2_ref, kkp_ref, kap_ref,
                    r_ref, km_ref, v_ref, lw_ref, kk_ref, bb_ref, g_ref, snew_ref, pbuf,
                    *, step, rows, pad):

    if step == 1:
        rolled = pltpu.roll(pf, 1, 0)
        sub8 = lax.broadcasted_iota(jnp.int32, (SUBLANES, P_B), 0)
        head = jnp.where(sub8 < 1, pltpu.roll(pbuf[0:SUBLANES, :], 1, 0), rolled[0:SUBLANES, :])
        shifted = jnp.concatenate([head, rolled[SUBLANES:, :]], axis=0)
        pbuf[0:SUBLANES, :] = pf[rows - SUBLANES:rows, :]
    else:
        pbuf[pad:pad + rows, :] = pf
        shifted = pbuf[pad - step:pad - step + rows, :]
        pbuf[pad - step:pad, :] = pf[rows - step:rows, :]
    snew_ref[0] = pf[rows - step:rows, :]

    m = pf + (shifted - pf) * mu_ref[...]
    o1 = 3 * W_B
    r_ref[...] = m[:, 0:W_B]
    kraw = m[:, W_B:2 * W_B]
    v_ref[...] = m[:, 2 * W_B:o1]
    xwa = m[:, o1:o1 + LORA_W + LORA_A]
    xg = m[:, o1 + LORA_W + LORA_A:]
    lane = lax.broadcasted_iota(jnp.int32, xwa.shape, 1)
    lhs = jnp.where(lane < LORA_W, jnp.tanh(xwa), xwa).astype(BF16)
    lw = jnp.dot(lhs, w2_ref[...], preferred_element_type=F32)
    la = jnp.dot(lhs, a2_ref[...], preferred_element_type=F32)
    w_log = -_softplus(-(w0_ref[...] + lw)) - 0.5
    lw_ref[...] = -jnp.exp(w_log)
    a = _sigmoid(a0_ref[...] + la)
    g_ref[...] = jnp.dot(_sigmoid(xg).astype(BF16), g2_ref[...], preferred_element_type=F32)
    kk = kraw * kkp_ref[...]
    kk = kk * lax.rsqrt(jnp.maximum(_head_sum(kk * kk, _head_ones(W_B)), 1e-24))
    kk_ref[...] = kk
    bb_ref[...] = kk * a
    km_ref[...] = kraw * (1.0 + (a - 1.0) * kap_ref[...])


N_RGLRU_IN = 9
N_PREP_IN = 9


def _front_kernel(*refs, step, rows, pad_a, pad_b, nsplit):
    x_ref, gn_ref, w_ref = refs[:3]
    rg_in = refs[3:3 + N_RGLRU_IN]
    pp_in = refs[3 + N_RGLRU_IN:3 + N_RGLRU_IN + N_PREP_IN]
    outs = refs[3 + N_RGLRU_IN + N_PREP_IN:]
    rg_out, pp_out = outs[:3], outs[3:11]
    xbuf, hcar, bbuf, pbuf, p_scr = outs[11:]
    part = rows // nsplit

    @pl.when(pl.program_id(1) == 0)
    def _():
        hist = (CONV_W - 1) * step
        xbuf[pad_a - hist:pad_a, :] = rg_in[0][0]
        hcar[...] = rg_in[1][0]
        pbuf[pad_b - step:pad_b, :] = pp_in[0][0]

    xn = _rms(x_ref[...], gn_ref[...]).astype(BF16)
    for s in range(nsplit):
        p_scr[s] = jnp.dot(xn[s * part:(s + 1) * part, :], w_ref[...],
                           preferred_element_type=F32)
    for s in range(nsplit):
        sub = lambda ref, s=s: ref.at[s * part:(s + 1) * part, :]
        _rglru_body(p_scr[s, :, 0:W_A], p_scr[s, :, W_A:P_A], *rg_in[2:],
                    sub(rg_out[0]), rg_out[1], rg_out[2], xbuf, hcar, bbuf,
                    step=step, rows=part, pad=pad_a)
        _rwkv_prep_body(p_scr[s, :, P_A:], *pp_in[1:], *[sub(r) for r in pp_out[:7]],
                        pp_out[7], pbuf, step=step, rows=part, pad=pad_b)


def _front(x, gn, w, conv_st, h_st, cw, cb, wr, wi, br, bi, lam,
           prev, mu, w0, a0, w2p, a2p, g2, kkp, kap, *, nseq, step, rows):
    n, d = x.shape
    nt = n // (nseq * rows)
    hist = (CONV_W - 1) * step
    pad_a = max(SUBLANES, hist)
    pad_b = max(SUBLANES, step)
    row_map = lambda b, t: (b * nt + t, 0)
    seq_map = lambda b, t: (b, 0, 0)
    whole = lambda a: pl.BlockSpec(a.shape, lambda b, t: (0,) * a.ndim,
                                   pipeline_mode=pl.Buffered(1))
    va = lambda p: p.reshape(1, W_A)
    vb = lambda p: p.reshape(1, W_B)
    consts_a = [cw, va(cb), wr, wi, va(br), va(bi), va(lam)]
    consts_b = [mu.reshape(1, P_B), vb(w0), vb(a0), w2p, a2p, g2, vb(kkp), vb(kap)]
    nsplit = 2 if step == 1 else 1
    part = rows // nsplit
    kern = functools.partial(_front_kernel, step=step, rows=rows, pad_a=pad_a, pad_b=pad_b,
                             nsplit=nsplit)
    tile = lambda wd: pl.BlockSpec((rows, wd), row_map)
    return pl.pallas_call(
        kern,
        grid=(nseq, nt),
        in_specs=[tile(d), whole(gn.reshape(1, d)), whole(w),
                  pl.BlockSpec((1, hist, W_A), seq_map), pl.BlockSpec((1, step, W_A), seq_map)]
                 + [whole(a) for a in consts_a]
                 + [pl.BlockSpec((1, step, P_B), seq_map)] + [whole(a) for a in consts_b],
        out_specs=[tile(W_A), pl.BlockSpec((1, hist, W_A), seq_map),
                   pl.BlockSpec((1, step, W_A), seq_map)]
                  + [tile(W_B)] * 7 + [pl.BlockSpec((1, step, P_B), seq_map)],
        out_shape=[jax.ShapeDtypeStruct((n, W_A), F32),
                   jax.ShapeDtypeStruct((nseq, hist, W_A), F32),
                   jax.ShapeDtypeStruct((nseq, step, W_A), F32)]
                  + [jax.ShapeDtypeStruct((n, W_B), F32)] * 7
                  + [jax.ShapeDtypeStruct((nseq, step, P_B), F32)],
        scratch_shapes=[pltpu.VMEM((pad_a + part, W_A), F32),
                        pltpu.VMEM((step, W_A), F32),
                        pltpu.VMEM((part, W_A), F32),
                        pltpu.VMEM((pad_b + part, P_B), F32),
                        pltpu.VMEM((nsplit, part, P_A + P_B), F32)],
        compiler_params=_cparams(("parallel", "arbitrary")),
        name="front",
    )(x, gn.reshape(1, d), w, conv_st, h_st, *consts_a, prev, *consts_b)


VGROUP = 4


def _rwkv_scan_kernel(r_ref, km_ref, v_ref, lw_ref, kk_ref, bb_ref, s0_ref, o_ref, sout_ref,
                      s_scr, *, tb):
    tblk = pl.program_id(1)

    @pl.when(tblk == 0)
    def _():
        s_scr[...] = s0_ref[...]

    def colsum(x):
        return jnp.sum(x, axis=0, keepdims=True)

    sub = lax.broadcasted_iota(jnp.int32, (VGROUP, LANES), 0)
    for g in range(HD_B // VGROUP):
        vbase = g * VGROUP

        def step(t, state, vbase=vbase):
            kk = kk_ref[t]
            w = jnp.exp(lw_ref[t])
            bvec = bb_ref[t]
            km = km_ref[t]
            r = r_ref[t]
            vt = v_ref[t, vbase:vbase + VGROUP, :]
            new = []
            otile = jnp.zeros((VGROUP, LANES), F32)
            for i in range(VGROUP):
                sk = colsum(state[i] * kk)
                sv = state[i] * w - sk * bvec + vt[i:i + 1, :] * km
                new.append(sv)
                otile = jnp.where(sub == i, colsum(sv * r), otile)
            o_ref[t, vbase:vbase + VGROUP, :] = otile
            return tuple(new)

        state = tuple(s_scr[vbase + i] for i in range(VGROUP))
        state = lax.fori_loop(0, tb, step, state, unroll=2)
        for i in range(VGROUP):
            s_scr[vbase + i] = state[i]

    sout_ref[...] = s_scr[...]


def _rwkv_scan(r, km, v, lw, kk, bb, s0, *, tb):
    nt, _, nl = r.shape
    seq = pl.BlockSpec((tb, HD_B, LANES), lambda g, t: (t, 0, g))
    st = pl.BlockSpec((HD_B, HD_B, LANES), lambda g, t: (0, 0, g))
    kern = functools.partial(_rwkv_scan_kernel, tb=tb)
    return pl.pallas_call(
        kern,
        grid=(nl // LANES, nt // tb),
        in_specs=[seq] * 6 + [st],
        out_specs=[seq, st],
        out_shape=[jax.ShapeDtypeStruct((nt, HD_B, nl), F32),
                   jax.ShapeDtypeStruct((HD_B, HD_B, nl), F32)],
        scratch_shapes=[pltpu.VMEM((HD_B, HD_B, LANES), F32)],
        compiler_params=_cparams(("parallel", "arbitrary")),
        name="rwkv_scan",
    )(r, km, v, lw, kk, bb, s0)


PAIR = 2 * HD_B
RWKV_SEQS_PER_STEP = 4


def _rwkv_chunk_kernel(r_ref, km_ref, v_ref, lw_ref, kk_ref, bb_ref, s0_ref, o_ref, sout_ref,
                       s_scr, *, chunk, nb):
    c = pl.program_id(1)

    @pl.when(c == 0)
    def _():
        s_scr[...] = s0_ref[...]

    lane = lax.broadcasted_iota(jnp.int32, (chunk, PAIR), 1)
    row = lax.broadcasted_iota(jnp.int32, (chunk, PAIR), 0)
    low_half = lane < HD_B
    li = jnp.where(low_half, lane, lane - HD_B)
    strict = li < row
    incl = li <= row
    eye = jnp.where(li == row, 1.0, 0.0)
    same = lambda n: (li // n) == (row // n)
    m8 = same(8)
    levels = []
    n = 16
    prev = m8
    while n <= chunk:
        cur = same(n)
        levels.append(jnp.logical_and(cur, jnp.logical_not(prev)))
        prev = cur
        n *= 2
    sq_r = lax.broadcasted_iota(jnp.int32, (PAIR, PAIR), 0)
    sq_c = lax.broadcasted_iota(jnp.int32, (PAIR, PAIR), 1)
    same_head = (sq_r < HD_B) == (sq_c < HD_B)
    low_bf = lax.broadcasted_iota(jnp.int32, (chunk, PAIR), 1) < HD_B

    def bd(s):
        return tuple(jnp.concatenate([jnp.where(low_bf, x, jnp.zeros_like(x)),
                                      jnp.where(low_bf, jnp.zeros_like(x), x)], axis=0)
                     for x in s)

    pm = lambda x, y: _mm(_split(x), bd(_split(y)))

    chains = [(b, p) for b in range(nb) for p in range(H_B // 2)]
    each = lambda f: [f(i) for i in range(len(chains))]

    vv, lhs2, kg, bg, kdbd, etot, s_p = [], [], [], [], [], [], []
    for b, p in chains:
        sl = slice(p * PAIR, (p + 1) * PAIR)
        r = r_ref[b, :, sl]
        km = km_ref[b, :, sl]
        lw = lw_ref[b, :, sl]
        kk = kk_ref[b, :, sl]
        bb = bb_ref[b, :, sl]
        cum = lw
        d = 1
        while d < chunk:
            cum = cum + jnp.where(row >= d, pltpu.roll(cum, d, 0), 0.0)
            d *= 2
        tot = cum[chunk - 1:chunk, :]
        g_end = jnp.exp(tot - cum)
        g_inv = jnp.exp(-cum)
        vv.append(v_ref[b, :, sl])
        lhs2.append(_split(jnp.concatenate([kk * jnp.exp(cum - lw), r * jnp.exp(cum)], axis=0)))
        kg.append(bd(_split(km * g_inv)))
        bg.append(bd(_split(bb * g_inv)))
        kdbd.append(_split(jnp.concatenate([km * g_end, bb * g_end], axis=0)))
        etot.append(jnp.exp(tot))
        s_p.append(s_scr[b, p])

    ss = each(lambda i: _mm(lhs2[i], _split(s_p[i]), nt=True))
    kb = each(lambda i: _mm(lhs2[i], tuple(jnp.concatenate([x, y], axis=0)
                                           for x, y in zip(kg[i], bg[i])), nt=True))
    a_b = each(lambda i: jnp.where(strict, kb[i][:chunk, PAIR:], 0.0))
    b_b = each(lambda i: jnp.where(incl, kb[i][chunk:, PAIR:], 0.0))
    av = each(lambda i: _mm(_split(jnp.concatenate(
        [jnp.where(strict, kb[i][:chunk, :PAIR], 0.0), jnp.where(incl, kb[i][chunk:, :PAIR], 0.0)],
        axis=0)), bd(_split(vv[i]))))

    dg = each(lambda i: jnp.where(m8, a_b[i], 0.0))
    dg_s = each(lambda i: _split(dg[i]))
    d2 = each(lambda i: _mm(dg_s[i], bd(dg_s[i])))
    d2_s = each(lambda i: _split(d2[i]))
    d2_b = each(lambda i: bd(d2_s[i]))
    d3 = each(lambda i: _mm(dg_s[i], d2_b[i]))
    d4 = each(lambda i: _mm(d2_s[i], d2_b[i]))
    t1 = each(lambda i: eye - dg[i] + d2[i] - d3[i])
    inv = each(lambda i: t1[i] + pm(t1[i], d4[i]))
    for m in levels:
        mlev = each(lambda i: pm(inv[i], jnp.where(m, a_b[i], 0.0)))
        inv = each(lambda i: inv[i] - pm(mlev[i], inv[i]))

    u = each(lambda i: pm(inv[i], ss[i][:chunk] + av[i][:chunk]))
    upd = each(lambda i: _mm(_split(jnp.transpose(jnp.concatenate([vv[i], -u[i]], axis=0))),
                             kdbd[i]))
    bu = each(lambda i: pm(b_b[i], u[i]))
    for i, (b, p) in enumerate(chains):
        sl = slice(p * PAIR, (p + 1) * PAIR)
        o_ref[b, :, sl] = ss[i][chunk:] + av[i][chunk:] - bu[i]
        s_scr[b, p] = s_p[i] * etot[i] + jnp.where(same_head, upd[i], 0.0)

    sout_ref[...] = s_scr[...]


def _rwkv_chunk(r, km, v, lw, kk, bb, s0, *, chunk, nb):
    nseq, t, _ = r.shape
    assert chunk == HD_B and nseq % nb == 0
    npair = H_B // 2
    row = pl.BlockSpec((nb, chunk, W_B), lambda b, c: (b, c, 0))
    st = pl.BlockSpec((nb, npair, PAIR, PAIR), lambda b, c: (b, 0, 0, 0))
    kern = functools.partial(_rwkv_chunk_kernel, chunk=chunk, nb=nb)
    return pl.pallas_call(
        kern,
        grid=(nseq // nb, t // chunk),
        in_specs=[row] * 6 + [st],
        out_specs=[row, st],
        out_shape=[jax.ShapeDtypeStruct((nseq, t, W_B), F32),
                   jax.ShapeDtypeStruct((nseq, npair, PAIR, PAIR), F32)],
        scratch_shapes=[pltpu.VMEM((nb, npair, PAIR, PAIR), F32)],
        compiler_params=_cparams(("parallel", "arbitrary")),
        name="rwkv_chunk",
    )(r, km, v, lw, kk, bb, s0)


MXU_N = 256


def _mix_residual(x_ref, oa_ref, o_ref, r_ref, km_ref, v_ref, g_ref, lnw_ref, lnb_ref, rk_ref,
                  wa_ref, wb_ref):
    ones = _head_ones(W_B)
    o = o_ref[...]
    cen = o - _head_sum(o, ones) * (1.0 / HD_B)
    head_sum1 = lambda z: jnp.dot(z.astype(BF16), ones, preferred_element_type=F32)
    var = head_sum1(cen * cen) * (1.0 / HD_B)
    on = cen * lax.rsqrt(var + GN_EPS) * lnw_ref[...] + lnb_ref[...]
    bonus = head_sum1(r_ref[...] * km_ref[...] * rk_ref[...]) * v_ref[...]
    ob = (on + bonus) * g_ref[...]
    return (x_ref[...]
            + jnp.dot(oa_ref[...].astype(BF16), wa_ref[...], preferred_element_type=F32)
            + jnp.dot(ob.astype(BF16), wb_ref[...], preferred_element_type=F32))


def _proj_residual(x_ref, o_ref, w_ref):
    return x_ref[...] + jnp.dot(o_ref[...].astype(BF16), w_ref[...],
                                preferred_element_type=F32)


def _res_ffn_kernel(*refs, residual, n_res, final_norm):
    g_ref, wg_ref, wu_ref, wd_ref, gf_ref, y_ref, xf_scr, x_scr, h_scr = refs[n_res:]
    x = residual(*refs[:n_res])
    xf_scr[...] = _rms(x, g_ref[...]).astype(BF16)
    x_scr[...] = x
    for j in range(wg_ref.shape[1] // MXU_N):
        cols = slice(j * MXU_N, (j + 1) * MXU_N)
        xf = xf_scr[...]
        hg = jnp.dot(xf, wg_ref[:, cols], preferred_element_type=F32)
        hu = jnp.dot(xf, wu_ref[:, cols], preferred_element_type=F32)
        h_scr[:, cols] = ((hg * _sigmoid(hg)) * hu).astype(BF16)
    y = x_scr[...] + jnp.dot(h_scr[...], wd_ref[...], preferred_element_type=F32)
    if final_norm:
        y = _rms(y, gf_ref[...])
    y_ref[...] = y


def _res_ffn(residual, row_ins, const_ins, g, wg, wu, wd, gf, *, tm, final_norm):
    n, d = row_ins[0].shape
    whole = lambda a: pl.BlockSpec(a.shape, lambda i: (0,) * a.ndim,
                                   pipeline_mode=pl.Buffered(1))
    consts = list(const_ins) + [g.reshape(1, d), wg, wu, wd, gf.reshape(1, d)]
    kern = functools.partial(_res_ffn_kernel, residual=residual,
                             n_res=len(row_ins) + len(const_ins), final_norm=final_norm)
    return pl.pallas_call(
        kern,
        grid=(n // tm,),
        in_specs=[pl.BlockSpec((tm, a.shape[1]), lambda i: (i, 0)) for a in row_ins]
                 + [whole(a) for a in consts],
        out_specs=pl.BlockSpec((tm, d), lambda i: (i, 0)),
        out_shape=jax.ShapeDtypeStruct((n, d), F32),
        scratch_shapes=[pltpu.VMEM((tm, d), BF16), pltpu.VMEM((tm, d), F32),
                        pltpu.VMEM((tm, wg.shape[1]), BF16)],
        compiler_params=_cparams(("parallel",)),
        name="res_ffn",
    )(*row_ins, *consts)


HGRN_SAFE_LOG = 80.0


def _hgrn_kernel(q_ref, f_ref, i_ref, g_ref, s0_ref, lbc_ref, gn_ref, o_ref, sout_ref,
                 s_scr, att_scr, *, layer, chunk, valid, nb):
    c = pl.program_id(1)

    @pl.when(c == 0)
    def _():
        s_scr[...] = s0_ref[...]

    lbc = lbc_ref[...]
    e = jnp.exp(lbc - jnp.max(lbc, axis=0, keepdims=True))
    sm = e / jnp.sum(e, axis=0, keepdims=True)
    lbf = sm[1:2, :]
    for l in range(2, layer + 1):
        lbf = lbf + sm[l:l + 1, :]

    nblk = chunk // SUBLANES
    sub = lax.broadcasted_iota(jnp.int32, (SUBLANES, DK_C), 0)
    lane_c = lax.broadcasted_iota(jnp.int32, (SUBLANES, chunk), 1)
    rowc = lax.broadcasted_iota(jnp.int32, (chunk, F_C), 0)
    tril = jnp.where(lax.broadcasted_iota(jnp.int32, (chunk, chunk), 0)
                     >= lax.broadcasted_iota(jnp.int32, (chunk, chunk), 1), 1.0, 0.0).astype(BF16)

    qs, ks, bs, ivs = [], [], [], []
    for b in range(nb):
        fr = f_ref[b]
        ez = jnp.exp(-jnp.abs(fr))
        rz = 1.0 / (1.0 + ez)
        sig = jnp.where(fr >= 0.0, rz, ez * rz)
        nsig = jnp.where(fr >= 0.0, ez * rz, rz)
        logf = jnp.log(lbf + (1.0 - lbf) * sig)
        kfull = (1.0 - lbf) * nsig
        if valid < chunk:
            logf = jnp.where(rowc < valid, logf, 0.0)
            kfull = jnp.where(rowc < valid, kfull, 0.0)
        hi, lo = _split(logf)
        bc = jnp.dot(jnp.concatenate([tril, tril], axis=1), jnp.concatenate([hi, lo], axis=0),
                     preferred_element_type=F32)
        qs.append(jax.nn.silu(q_ref[b]))
        ks.append(kfull)
        bs.append(bc)
        ivs.append(i_ref[b].astype(BF16))

    chains = [(b, h) for b in range(nb) for h in range(H_C)]
    each = lambda f: [f(i) for i in range(len(chains))]
    hsl = lambda h: slice(h * DK_C, (h + 1) * DK_C)
    q = each(lambda i: qs[chains[i][0]][:, hsl(chains[i][1])])
    k = each(lambda i: ks[chains[i][0]][:, hsl(chains[i][1])])
    bcs = each(lambda i: bs[chains[i][0]][:, hsl(chains[i][1])])
    iv = each(lambda i: ivs[chains[i][0]][:, hsl(chains[i][1])])
    s_h = each(lambda i: s_scr[chains[i][0], chains[i][1]])

    def off_diag(i, tb):
        lo = tb * SUBLANES
        rs = slice(lo, lo + SUBLANES)
        edge = bcs[i][lo - 1:lo, :]
        qi = q[i][rs, :] * jnp.exp(bcs[i][rs, :] - edge)
        ki = k[i][0:lo, :] * jnp.exp(edge - bcs[i][0:lo, :])
        ki = jnp.concatenate([ki, jnp.zeros((chunk - lo, DK_C), F32)], axis=0)
        return lax.dot_general(qi.astype(BF16), ki.astype(BF16), (((1,), (1,)), ((), ())),
                               preferred_element_type=F32)

    inter = each(lambda i: jnp.dot((q[i] * jnp.exp(bcs[i])).astype(BF16), s_h[i].astype(BF16),
                                   preferred_element_type=F32))


    sub_c = lax.broadcasted_iota(jnp.int32, (SUBLANES, chunk), 0)

    def att_on_matrix_unit():
        def block(i, tb):
            lo = tb * SUBLANES
            rs = slice(lo, lo + SUBLANES)
            dloc = bcs[i][rs, :] if tb == 0 else bcs[i][rs, :] - bcs[i][lo - 1:lo, :]
            qi = q[i][rs, :] * jnp.exp(dloc)
            parts = [k[i][rs, :] * jnp.exp(-dloc)]
            if tb > 0:
                parts.insert(0, k[i][0:lo, :] * jnp.exp(bcs[i][lo - 1:lo, :] - bcs[i][0:lo, :]))
            if chunk - lo - SUBLANES > 0:
                parts.append(jnp.zeros((chunk - lo - SUBLANES, DK_C), F32))
            ki = jnp.concatenate(parts, axis=0) if len(parts) > 1 else parts[0]
            return lax.dot_general(qi.astype(BF16), ki.astype(BF16), (((1,), (1,)), ((), ())),
                                   preferred_element_type=F32)

        blks = each(lambda i: [block(i, tb) for tb in range(nblk)])
        for i in range(len(chains)):
            for tb in range(nblk):
                att_scr[i, tb * SUBLANES:(tb + 1) * SUBLANES, :] = jnp.where(
                    lane_c <= sub_c + tb * SUBLANES, blks[i][tb], 0.0)

    def att_pairwise_diagonal():
        offd = each(lambda i: [off_diag(i, tb) for tb in range(1, nblk)])
        for i in range(len(chains)):
            for tb in range(nblk):
                rs = slice(tb * SUBLANES, (tb + 1) * SUBLANES)
                blk = jnp.zeros((SUBLANES, chunk), F32) if tb == 0 else offd[i][tb - 1]
                for j in range(SUBLANES):
                    s = tb * SUBLANES + j
                    if s >= valid:
                        break
                    diff = jnp.where(sub >= j, bcs[i][rs, :] - bcs[i][s:s + 1, :], -jnp.inf)
                    ev = jnp.exp(diff) * (q[i][rs, :] * k[i][s:s + 1, :])
                    blk = jnp.where(lane_c == s, jnp.sum(ev, axis=1, keepdims=True), blk)
                att_scr[i, rs, :] = blk

    if nblk > 1:
        worst = jnp.float32(0.0)
        for b in range(nb):
            win = bs[b] - jnp.where(rowc >= SUBLANES, pltpu.roll(bs[b], SUBLANES, 0), 0.0)
            worst = jnp.minimum(worst, jnp.min(win))
        safe = worst >= -HGRN_SAFE_LOG
        pl.when(safe)(att_on_matrix_unit)
        pl.when(jnp.logical_not(safe))(att_pairwise_diagonal)
    else:
        att_pairwise_diagonal()

    o_h = each(lambda i: jnp.dot(att_scr[i].astype(BF16), iv[i], preferred_element_type=F32)
               + inter[i])

    def new_state(i):
        bl = bcs[i][chunk - 1:chunk, :]
        kd = k[i] * jnp.exp(bl - bcs[i])
        pieces = [kd, jnp.broadcast_to(jnp.exp(bl), (SUBLANES, DK_C))]
        fill = DK_C - chunk - SUBLANES
        if fill > 0:
            pieces.append(jnp.zeros((fill, DK_C), F32))
        xt = jnp.transpose(jnp.concatenate(pieces, axis=0))
        return (xt[:, chunk:chunk + 1] * s_h[i]
                + jnp.dot(xt[:, 0:chunk].astype(BF16), iv[i], preferred_element_type=F32))

    s_new = each(new_state)
    for i, (b, h) in enumerate(chains):
        s_scr[b, h] = s_new[i]
    for b in range(nb):
        o = jnp.concatenate([o_h[b * H_C + h] for h in range(H_C)], axis=1)
        o_ref[b] = _rms(o, gn_ref[...]) * jax.nn.silu(g_ref[b])
    sout_ref[...] = s_scr[...]


def _hgrn(pc3, s0, lbc, gn, *, layer, chunk, valid, nb):
    nseq, t, _ = pc3.shape
    assert nseq % nb == 0
    col = lambda j: pl.BlockSpec((nb, chunk, F_C), lambda b, c, j=j: (b, c, j))
    st = pl.BlockSpec((nb, H_C, DK_C, DK_C), lambda b, c: (b, 0, 0, 0))
    kern = functools.partial(_hgrn_kernel, layer=layer, chunk=chunk, valid=valid, nb=nb)
    return pl.pallas_call(
        kern,
        grid=(nseq // nb, t // chunk),
        in_specs=[col(0), col(1), col(2), col(3), st,
                  pl.BlockSpec(lbc.shape, lambda b, c: (0, 0)),
                  pl.BlockSpec((1, D_MODEL), lambda b, c: (0, 0))],
        out_specs=[pl.BlockSpec((nb, chunk, D_MODEL), lambda b, c: (b, c, 0)), st],
        out_shape=[jax.ShapeDtypeStruct((nseq, t, D_MODEL), F32),
                   jax.ShapeDtypeStruct(s0.shape, F32)],
        scratch_shapes=[pltpu.VMEM((nb, H_C, DK_C, DK_C), F32),
                        pltpu.VMEM((nb * H_C, chunk, chunk), F32)],
        compiler_params=_cparams(("parallel", "arbitrary")),
        name="hgrn",
    )(pc3, pc3, pc3, pc3, s0, lbc, gn.reshape(1, D_MODEL))


def _block_diag(w):
    h, n, _ = w.shape
    eye = jnp.eye(h, dtype=w.dtype)
    return (eye[:, None, :, None] * w[:, :, None, :]).reshape(h * n, h * n)


def _fold(x):
    half = x.shape[-2] // 2
    return jnp.concatenate([x[..., :half, :], x[..., half:, :]], axis=-1)


def _unfold(x):
    half = x.shape[-1] // 2
    return jnp.concatenate([x[..., :half], x[..., half:]], axis=-2)


def _trunk(x2d, conv_st, h_st, shift_st, rs_st, hs_st, P, *, nbatch, nsteps, time_major):
    n = x2d.shape[0]
    if time_major:
        nseq, step, rows = 1, nbatch, n
        tm = n
    else:
        nseq, step, rows = nbatch, 1, 512
        tm = 512
    nchain = nbatch * H_B

    out_a, conv_new, h_new, r, km, v, lw, kk, bb, g, shift_new = _front(
        x2d, P['ln_mix'][0], P['w_in_ab'],
        conv_st, h_st, P['conv_w'], P['conv_b'], P['wr'], P['wi'], P['gr_b'], P['gi_b'],
        P['lru_lambda'],
        shift_st, P['mu_b'], P['w0_b'], P['a0_b'], P['w2p'], P['a2p'], P['g2_b'],
        P['kk_b'], P['ka_b'], nseq=nseq, step=step, rows=rows)

    if time_major:
        def to_lanes(z):
            z = z.reshape(nsteps, nbatch, H_B, HD_B).transpose(0, 3, 1, 2)
            return z.reshape(nsteps, HD_B, nchain)

        s0 = rs_st.transpose(2, 3, 0, 1).reshape(HD_B, HD_B, nchain)
        o, s_new = _rwkv_scan(to_lanes(r), to_lanes(km), to_lanes(v), to_lanes(lw), to_lanes(kk),
                              to_lanes(bb), s0, tb=nsteps)
        o = o.reshape(nsteps, HD_B, nbatch, H_B).transpose(0, 2, 3, 1).reshape(n, W_B)
        rs_new = s_new.reshape(HD_B, HD_B, nbatch, H_B).transpose(2, 3, 0, 1)
    else:
        z = jnp.zeros_like(rs_st[:, 0::2])
        s0 = jnp.concatenate([jnp.concatenate([rs_st[:, 0::2], z], axis=-1),
                              jnp.concatenate([z, rs_st[:, 1::2]], axis=-1)], axis=-2)
        seq3 = lambda y: y.reshape(nbatch, nsteps, W_B)
        o, s_new = _rwkv_chunk(seq3(r), seq3(km), seq3(v), seq3(lw), seq3(kk), seq3(bb), s0,
                               chunk=HD_B, nb=RWKV_SEQS_PER_STEP)
        o = o.reshape(n, W_B)
        rs_new = jnp.stack([s_new[:, :, :HD_B, :HD_B], s_new[:, :, HD_B:, HD_B:]],
                           axis=2).reshape(nbatch, H_B, HD_B, HD_B)

    vec = lambda p: p.reshape(1, W_B)
    x2 = _res_ffn(_mix_residual, [x2d, out_a, o, r, km, v, g],
                  [vec(P['lnx_w']), vec(P['lnx_b']), vec(P['rk_b']), P['w_out_a'], P['w_out_b']],
                  P['ln_ffn'][0], P['ffn_gate'][0], P['ffn_up'][0], P['ffn_down'][0],
                  P['ln_final'], tm=tm, final_norm=False)

    (pc,) = _norm_matmul(x2, P['ln_mix'][1], P['w_in_c'], (4 * F_C,), tm)
    if time_major:
        pc3 = pc.reshape(nsteps, nbatch, 4 * F_C).transpose(1, 0, 2)
        chunk = SUBLANES
        pc3 = jnp.pad(pc3, ((0, 0), (0, chunk - nsteps), (0, 0)))
    else:
        pc3 = pc.reshape(nbatch, nsteps, 4 * F_C)
        chunk = 64
    o3, hs_new = _hgrn(pc3, hs_st, P['lb_c'], P['gn_c'], layer=1, chunk=chunk,
                       valid=min(chunk, nsteps), nb=4 if time_major else 1)
    if time_major:
        o = o3[:, :nsteps].transpose(1, 0, 2).reshape(n, D_MODEL)
    else:
        o = o3.reshape(n, D_MODEL)
    y = _res_ffn(_proj_residual, [x2, o], [P['w_out_c']],
                 P['ln_ffn'][1], P['ffn_gate'][1], P['ffn_up'][1], P['ffn_down'][1],
                 P['ln_final'], tm=tm, final_norm=True)
    return y, conv_new, h_new, shift_new, rs_new, hs_new


def kernel(x_prompt, x_sample, state_rglru_conv, state_rglru_h, state_rwkv_shift, state_rwkv_S,
           state_hgrn_S, ln_mix, ln_ffn, ln_final, w_in_ab, conv_w, conv_b, gr_w, gr_b, gi_w,
           gi_b, lru_lambda, mu_b, w0_b, w2_b, a0_b, a2_b, g2_b, kk_b, ka_b, rk_b, lnx_w, lnx_b,
           w_out_ab, w_in_c, lb_c, gn_c, w_out_c, ffn_gate, ffn_up, ffn_down):
    bp, tp, _ = x_prompt.shape
    bs, ts, _ = x_sample.shape
    zpad_w = jnp.zeros((LORA_A, W_B), F32)
    zpad_a = jnp.zeros((LORA_W, W_B), F32)
    P = dict(
        ln_mix=ln_mix, ln_ffn=ln_ffn, ln_final=ln_final,
        w_in_ab=w_in_ab[0].astype(BF16), conv_w=conv_w[0], conv_b=conv_b[0],
        wr=_block_diag(gr_w[0]).astype(BF16), wi=_block_diag(gi_w[0]).astype(BF16),
        gr_b=gr_b[0], gi_b=gi_b[0], lru_lambda=lru_lambda[0], mu_b=mu_b[0], w0_b=w0_b[0],
        a0_b=a0_b[0],
        w2p=jnp.concatenate([w2_b[0], zpad_w], axis=0).astype(BF16),
        a2p=jnp.concatenate([zpad_a, a2_b[0]], axis=0).astype(BF16),
        g2_b=g2_b[0].astype(BF16), kk_b=kk_b[0], ka_b=ka_b[0], rk_b=rk_b[0],
        lnx_w=lnx_w[0], lnx_b=lnx_b[0],
        w_out_a=w_out_ab[0, :W_A].astype(BF16), w_out_b=w_out_ab[0, W_A:].astype(BF16),
        w_in_c=w_in_c[0].astype(BF16), lb_c=lb_c, gn_c=gn_c[0],
        w_out_c=w_out_c[0].astype(BF16), ffn_gate=ffn_gate.astype(BF16),
        ffn_up=ffn_up.astype(BF16), ffn_down=ffn_down.astype(BF16))

    yp, p_conv, p_h, p_shift, p_rs, p_hs = _trunk(
        x_prompt.reshape(bp * tp, D_MODEL),
        jnp.zeros((bp, CONV_W - 1, W_A), F32), jnp.zeros((bp, 1, W_A), F32),
        jnp.zeros((bp, 1, P_B), F32), jnp.zeros((bp, H_B, HD_B, HD_B), F32),
        jnp.zeros((bp, H_C, DK_C, DK_C), F32), P, nbatch=bp, nsteps=tp, time_major=False)

    ys, s_conv, s_h, s_shift, s_rs, s_hs = _trunk(
        x_sample.transpose(1, 0, 2).reshape(ts * bs, D_MODEL),
        state_rglru_conv[0].transpose(1, 0, 2).reshape(1, (CONV_W - 1) * bs, W_A),
        state_rglru_h[0].reshape(1, bs, W_A), state_rwkv_shift[0].reshape(1, bs, P_B),
        state_rwkv_S[0], state_hgrn_S[0], P, nbatch=bs, nsteps=ts, time_major=True)

    return (yp.reshape(bp, tp, D_MODEL),
            ys.reshape(ts, bs, D_MODEL).transpose(1, 0, 2),
            p_conv[None], p_h.reshape(1, bp, W_A), p_shift.reshape(1, bp, P_B), p_rs[None],
            p_hs[None],
            s_conv.reshape(CONV_W - 1, bs, W_A).transpose(1, 0, 2)[None],
            s_h.reshape(1, bs, W_A), s_shift.reshape(1, bs, P_B), s_rs[None], s_hs[None])
```

```python
import functools

import jax
import jax.numpy as jnp
from jax import lax
from jax.experimental import pallas as pl
from jax.experimental.pallas import tpu as pltpu

F32 = jnp.float32
BF16 = jnp.bfloat16

D_MODEL = 1024
W_A = 512
H_A = 8
CONV_W = 4
LRU_C = 8.0
W_B = 512
HD_B = 64
H_B = 8
LORA_W = 64
LORA_A = 64
LORA_G = 128
P_A = 2 * W_A
P_B = 3 * W_B + LORA_W + LORA_A + LORA_G
DK_C = 128
H_C = 8
F_C = 1024
D_FF = 2816
RMS_EPS = 1e-6
GN_EPS = 64e-5

LANES = 128
SUBLANES = 8
VMEM_LIMIT = 56 * 1024 * 1024


def _cparams(sem):
    return pltpu.CompilerParams(dimension_semantics=sem, vmem_limit_bytes=VMEM_LIMIT)


def _softplus(x):
    return jnp.maximum(x, 0.0) + jnp.log1p(jnp.exp(-jnp.abs(x)))

def _sigmoid(x):
    return 1.0 / (1.0 + jnp.exp(-x))


def _rms(x, g):
    ms = jnp.mean(x * x, axis=-1, keepdims=True)
    return x * lax.rsqrt(ms + RMS_EPS) * g


def _dot(a, b):
    return jnp.dot(a.astype(BF16), b.astype(BF16), preferred_element_type=F32)


def _split(x):
    hi = x.astype(BF16)
    lo = (x - hi.astype(F32)).astype(BF16)
    return hi, lo


def _mm(a, b, nt=False):
    ah, al = a
    bh, bl = b
    lhs = jnp.concatenate([ah, al, ah], axis=1)
    if nt:
        rhs = jnp.concatenate([bh, bh, bl], axis=1)
        return lax.dot_general(lhs, rhs, (((1,), (1,)), ((), ())), preferred_element_type=F32)
    rhs = jnp.concatenate([bh, bh, bl], axis=0)
    return jnp.dot(lhs, rhs, preferred_element_type=F32)


def _head_ones(width):
    r = lax.broadcasted_iota(jnp.int32, (width, width), 0) // HD_B
    c = lax.broadcasted_iota(jnp.int32, (width, width), 1) // HD_B
    return jnp.where(r == c, 1.0, 0.0).astype(BF16)


def _head_sum(x, ones):
    hi, lo = _split(x)
    return jnp.dot(jnp.concatenate([hi, lo], axis=1), jnp.concatenate([ones, ones], axis=0),
                   preferred_element_type=F32)


def _norm_matmul_kernel(x_ref, g_ref, w_ref, *o_refs):
    xn = _rms(x_ref[...], g_ref[...]).astype(BF16)
    off = 0
    for o_ref in o_refs:
        n = o_ref.shape[-1]
        o_ref[...] = jnp.dot(xn, w_ref[:, off:off + n], preferred_element_type=F32)
        off += n


def _norm_matmul(x, g, w, splits, tm):
    n, d = x.shape
    p = w.shape[1]
    assert sum(splits) == p and n % tm == 0
    return pl.pallas_call(
        _norm_matmul_kernel,
        grid=(n // tm,),
        in_specs=[pl.BlockSpec((tm, d), lambda i: (i, 0)),
                  pl.BlockSpec((1, d), lambda i: (0, 0)),
                  pl.BlockSpec((d, p), lambda i: (0, 0))],
        out_specs=[pl.BlockSpec((tm, s), lambda i: (i, 0)) for s in splits],
        out_shape=[jax.ShapeDtypeStruct((n, s), F32) for s in splits],
        compiler_params=_cparams(("parallel",)),
        name="norm_matmul",
    )(x, g.reshape(1, d), w)


def _rglru_body(u_in, gate, cw_ref, cb_ref, wr_ref, wi_ref, br_ref, bi_ref, lam_ref,
                out_ref, cnew_ref, hnew_ref, xbuf, hcar, bbuf, *, step, rows, pad):
    hist = (CONV_W - 1) * step

    if step == 1:
        u = u_in
        prev = xbuf[0:SUBLANES, :]
        sub8 = lax.broadcasted_iota(jnp.int32, (SUBLANES, W_A), 0)
        conv = cb_ref[...]
        for j in range(CONV_W):
            d = CONV_W - 1 - j
            if d == 0:
                ush = u
            else:
                rolled = pltpu.roll(u, d, 0)
                head = jnp.where(sub8 < d, pltpu.roll(prev, d, 0), rolled[0:SUBLANES, :])
                ush = jnp.concatenate([head, rolled[SUBLANES:, :]], axis=0)
            conv = conv + ush * cw_ref[j:j + 1, :]
        cnew_ref[0] = u[rows - hist:rows, :]
        xbuf[0:SUBLANES, :] = u[rows - SUBLANES:rows, :]
    else:
        xbuf[pad:pad + rows, :] = u_in
        conv = cb_ref[...] + xbuf[pad - hist:pad - hist + rows, :] * cw_ref[0:1, :]
        for j in range(1, CONV_W):
            o = pad - hist + j * step
            conv = conv + xbuf[o:o + rows, :] * cw_ref[j:j + 1, :]
        tail = xbuf[pad + rows - hist:pad + rows, :]
        cnew_ref[0] = tail
        xbuf[pad - hist:pad, :] = tail

    ub = conv.astype(BF16)
    r = _sigmoid(jnp.dot(ub, wr_ref[...], preferred_element_type=F32) + br_ref[...])
    ig = _sigmoid(jnp.dot(ub, wi_ref[...], preferred_element_type=F32) + bi_ref[...])
    log_a = (-LRU_C) * r * _softplus(-lam_ref[...])
    a = jnp.exp(log_a)
    bterm = jnp.sqrt(-jnp.tanh(log_a) * (a * a + 1.0)) * (ig * conv)
    bbuf[...] = bterm
    bbuf[0:step, :] = bterm[0:step, :] + a[0:step, :] * hcar[...]
    bv = bbuf[...]

    def scan_levels(a, bv, pos, first, count):
        d = first
        while d < first * count:
            m = pos >= d
            bv = jnp.where(m, a * pltpu.roll(bv, d, 0) + bv, bv)
            a = jnp.where(m, a * pltpu.roll(a, d, 0), a)
            d *= 2
        return a, bv

    row = lax.broadcasted_iota(jnp.int32, (rows, W_A), 0)
    a, bv = scan_levels(a, bv, row, step, rows // step)
    hcar[...] = bv[rows - step:rows, :]
    hnew_ref[0] = bv[rows - step:rows, :]
    out_ref[...] = jax.nn.gelu(gate) * bv


def _rwkv_prep_body(pf, mu_ref, w0_ref, a0_ref, w2_ref, a2_ref, g2_ref, kkp_ref, kap_ref,
                    r_ref, km_ref, v_ref, lw_ref, kk_ref, bb_ref, g_ref, snew_ref, pbuf,
                    *, step, rows, pad):

    if step == 1:
        rolled = pltpu.roll(pf, 1, 0)
        sub8 = lax.broadcasted_iota(jnp.int32, (SUBLANES, P_B), 0)
        head = jnp.where(sub8 < 1, pltpu.roll(pbuf[0:SUBLANES, :], 1, 0), rolled[0:SUBLANES, :])
        shifted = jnp.concatenate([head, rolled[SUBLANES:, :]], axis=0)
        pbuf[0:SUBLANES, :] = pf[rows - SUBLANES:rows, :]
    else:
        pbuf[pad:pad + rows, :] = pf
        shifted = pbuf[pad - step:pad - step + rows, :]
        pbuf[pad - step:pad, :] = pf[rows - step:rows, :]
    snew_ref[0] = pf[rows - step:rows, :]

    m = pf + (shifted - pf) * mu_ref[...]
    o1 = 3 * W_B
    r_ref[...] = m[:, 0:W_B]
    kraw = m[:, W_B:2 * W_B]
    v_ref[...] = m[:, 2 * W_B:o1]
    xwa = m[:, o1:o1 + LORA_W + LORA_A]
    xg = m[:, o1 + LORA_W + LORA_A:]
    lane = lax.broadcasted_iota(jnp.int32, xwa.shape, 1)
    lhs = jnp.where(lane < LORA_W, jnp.tanh(xwa), xwa).astype(BF16)
    lw = jnp.dot(lhs, w2_ref[...], preferred_element_type=F32)
    la = jnp.dot(lhs, a2_ref[...], preferred_element_type=F32)
    w_log = -_softplus(-(w0_ref[...] + lw)) - 0.5
    lw_ref[...] = -jnp.exp(w_log)
    a = _sigmoid(a0_ref[...] + la)
    g_ref[...] = jnp.dot(_sigmoid(xg).astype(BF16), g2_ref[...], preferred_element_type=F32)
    kk = kraw * kkp_ref[...]
    kk = kk * lax.rsqrt(jnp.maximum(_head_sum(kk * kk, _head_ones(W_B)), 1e-24))
    kk_ref[...] = kk
    bb_ref[...] = kk * a
    km_ref[...] = kraw * (1.0 + (a - 1.0) * kap_ref[...])


N_RGLRU_IN = 9
N_PREP_IN = 9


def _front_kernel(*refs, step, rows, pad_a, pad_b, nsplit):
    x_ref, gn_ref, w_ref = refs[:3]
    rg_in = refs[3:3 + N_RGLRU_IN]
    pp_in = refs[3 + N_RGLRU_IN:3 + N_RGLRU_IN + N_PREP_IN]
    outs = refs[3 + N_RGLRU_IN + N_PREP_IN:]
    rg_out, pp_out = outs[:3], outs[3:11]
    xbuf, hcar, bbuf, pbuf, p_scr = outs[11:]
    part = rows // nsplit

    @pl.when(pl.program_id(1) == 0)
    def _():
        hist = (CONV_W - 1) * step
        xbuf[pad_a - hist:pad_a, :] = rg_in[0][0]
        hcar[...] = rg_in[1][0]
        pbuf[pad_b - step:pad_b, :] = pp_in[0][0]

    xn = _rms(x_ref[...], gn_ref[...]).astype(BF16)
    for s in range(nsplit):
        p_scr[s] = jnp.dot(xn[s * part:(s + 1) * part, :], w_ref[...],
                           preferred_element_type=F32)
    for s in range(nsplit):
        sub = lambda ref, s=s: ref.at[s * part:(s + 1) * part, :]
        _rglru_body(p_scr[s, :, 0:W_A], p_scr[s, :, W_A:P_A], *rg_in[2:],
                    sub(rg_out[0]), rg_out[1], rg_out[2], xbuf, hcar, bbuf,
                    step=step, rows=part, pad=pad_a)
        _rwkv_prep_body(p_scr[s, :, P_A:], *pp_in[1:], *[sub(r) for r in pp_out[:7]],
                        pp_out[7], pbuf, step=step, rows=part, pad=pad_b)


def _front(x, gn, w, conv_st, h_st, cw, cb, wr, wi, br, bi, lam,
           prev, mu, w0, a0, w2p, a2p, g2, kkp, kap, *, nseq, step, rows):
    n, d = x.shape
    nt = n // (nseq * rows)
    hist = (CONV_W - 1) * step
    pad_a = max(SUBLANES, hist)
    pad_b = max(SUBLANES, step)
    row_map = lambda b, t: (b * nt + t, 0)
    seq_map = lambda b, t: (b, 0, 0)
    whole = lambda a: pl.BlockSpec(a.shape, lambda b, t: (0,) * a.ndim,
                                   pipeline_mode=pl.Buffered(1))
    va = lambda p: p.reshape(1, W_A)
    vb = lambda p: p.reshape(1, W_B)
    consts_a = [cw, va(cb), wr, wi, va(br), va(bi), va(lam)]
    consts_b = [mu.reshape(1, P_B), vb(w0), vb(a0), w2p, a2p, g2, vb(kkp), vb(kap)]
    nsplit = 2 if step == 1 else 1
    part = rows // nsplit
    kern = functools.partial(_front_kernel, step=step, rows=rows, pad_a=pad_a, pad_b=pad_b,
                             nsplit=nsplit)
    tile = lambda wd: pl.BlockSpec((rows, wd), row_map)
    return pl.pallas_call(
        kern,
        grid=(nseq, nt),
        in_specs=[tile(d), whole(gn.reshape(1, d)), whole(w),
                  pl.BlockSpec((1, hist, W_A), seq_map), pl.BlockSpec((1, step, W_A), seq_map)]
                 + [whole(a) for a in consts_a]
                 + [pl.BlockSpec((1, step, P_B), seq_map)] + [whole(a) for a in consts_b],
        out_specs=[tile(W_A), pl.BlockSpec((1, hist, W_A), seq_map),
                   pl.BlockSpec((1, step, W_A), seq_map)]
                  + [tile(W_B)] * 7 + [pl.BlockSpec((1, step, P_B), seq_map)],
        out_shape=[jax.ShapeDtypeStruct((n, W_A), F32),
                   jax.ShapeDtypeStruct((nseq, hist, W_A), F32),
                   jax.ShapeDtypeStruct((nseq, step, W_A), F32)]
                  + [jax.ShapeDtypeStruct((n, W_B), F32)] * 7
                  + [jax.ShapeDtypeStruct((nseq, step, P_B), F32)],
        scratch_shapes=[pltpu.VMEM((pad_a + part, W_A), F32),
                        pltpu.VMEM((step, W_A), F32),
                        pltpu.VMEM((part, W_A), F32),
                        pltpu.VMEM((pad_b + part, P_B), F32),
                        pltpu.VMEM((nsplit, part, P_A + P_B), F32)],
        compiler_params=_cparams(("parallel", "arbitrary")),
        name="front",
    )(x, gn.reshape(1, d), w, conv_st, h_st, *consts_a, prev, *consts_b)


VGROUP = 4


def _rwkv_scan_kernel(r_ref, km_ref, v_ref, lw_ref, kk_ref, bb_ref, s0_ref, o_ref, sout_ref,
                      s_scr, *, tb):
    tblk = pl.program_id(1)

    @pl.when(tblk == 0)
    def _():
        s_scr[...] = s0_ref[...]

    def colsum(x):
        return jnp.sum(x, axis=0, keepdims=True)

    sub = lax.broadcasted_iota(jnp.int32, (VGROUP, LANES), 0)
    for g in range(HD_B // VGROUP):
        vbase = g * VGROUP

        def step(t, state, vbase=vbase):
            kk = kk_ref[t]
            w = jnp.exp(lw_ref[t])
            bvec = bb_ref[t]
            km = km_ref[t]
            r = r_ref[t]
            vt = v_ref[t, vbase:vbase + VGROUP, :]
            new = []
            otile = jnp.zeros((VGROUP, LANES), F32)
            for i in range(VGROUP):
                sk = colsum(state[i] * kk)
                sv = state[i] * w - sk * bvec + vt[i:i + 1, :] * km
                new.append(sv)
                otile = jnp.where(sub == i, colsum(sv * r), otile)
            o_ref[t, vbase:vbase + VGROUP, :] = otile
            return tuple(new)

        state = tuple(s_scr[vbase + i] for i in range(VGROUP))
        state = lax.fori_loop(0, tb, step, state, unroll=2)
        for i in range(VGROUP):
            s_scr[vbase + i] = state[i]

    sout_ref[...] = s_scr[...]


def _rwkv_scan(r, km, v, lw, kk, bb, s0, *, tb):
    nt, _, nl = r.shape
    seq = pl.BlockSpec((tb, HD_B, LANES), lambda g, t: (t, 0, g))
    st = pl.BlockSpec((HD_B, HD_B, LANES), lambda g, t: (0, 0, g))
    kern = functools.partial(_rwkv_scan_kernel, tb=tb)
    return pl.pallas_call(
        kern,
        grid=(nl // LANES, nt // tb),
        in_specs=[seq] * 6 + [st],
        out_specs=[seq, st],
        out_shape=[jax.ShapeDtypeStruct((nt, HD_B, nl), F32),
                   jax.ShapeDtypeStruct((HD_B, HD_B, nl), F32)],
        scratch_shapes=[pltpu.VMEM((HD_B, HD_B, LANES), F32)],
        compiler_params=_cparams(("parallel", "arbitrary")),
        name="rwkv_scan",
    )(r, km, v, lw, kk, bb, s0)


PAIR = 2 * HD_B
RWKV_SEQS_PER_STEP = 8


def _rwkv_chunk_kernel(r_ref, km_ref, v_ref, lw_ref, kk_ref, bb_ref, s0_ref, o_ref, sout_ref,
                       s_scr, *, chunk, nb):
    c = pl.program_id(1)

    @pl.when(c == 0)
    def _():
        s_scr[...] = s0_ref[...]

    lane = lax.broadcasted_iota(jnp.int32, (chunk, PAIR), 1)
    row = lax.broadcasted_iota(jnp.int32, (chunk, PAIR), 0)
    low_half = lane < HD_B
    li = jnp.where(low_half, lane, lane - HD_B)
    strict = li < row
    incl = li <= row
    eye = jnp.where(li == row, 1.0, 0.0)
    same = lambda n: (li // n) == (row // n)
    m8 = same(8)
    levels = []
    n = 16
    prev = m8
    while n <= chunk:
        cur = same(n)
        levels.append(jnp.logical_and(cur, jnp.logical_not(prev)))
        prev = cur
        n *= 2
    sq_r = lax.broadcasted_iota(jnp.int32, (PAIR, PAIR), 0)
    sq_c = lax.broadcasted_iota(jnp.int32, (PAIR, PAIR), 1)
    same_head = (sq_r < HD_B) == (sq_c < HD_B)
    low_bf = lax.broadcasted_iota(jnp.int32, (chunk, PAIR), 1) < HD_B

    def bd(s):
        return tuple(jnp.concatenate([jnp.where(low_bf, x, jnp.zeros_like(x)),
                                      jnp.where(low_bf, jnp.zeros_like(x), x)], axis=0)
                     for x in s)

    def pm1(x, y):
        return jnp.dot(x.astype(BF16), bd((y.astype(BF16),))[0], preferred_element_type=F32)

    chains = [(b, p) for b in range(nb) for p in range(H_B // 2)]
    each = lambda f: [f(i) for i in range(len(chains))]

    vv, lhs2, kg, bg, kdbd, etot, s_p = [], [], [], [], [], [], []
    for b, p in chains:
        sl = slice(p * PAIR, (p + 1) * PAIR)
        r = r_ref[b, :, sl]
        km = km_ref[b, :, sl]
        lw = lw_ref[b, :, sl]
        kk = kk_ref[b, :, sl]
        bb = bb_ref[b, :, sl]
        cum = lw
        d = 1
        while d < chunk:
            cum = cum + jnp.where(row >= d, pltpu.roll(cum, d, 0), 0.0)
            d *= 2
        tot = cum[chunk - 1:chunk, :]
        g_end = jnp.exp(tot - cum)
        g_inv = jnp.exp(-cum)
        vv.append(v_ref[b, :, sl])
        lhs2.append(_split(jnp.concatenate([kk * jnp.exp(cum - lw), r * jnp.exp(cum)], axis=0)))
        kg.append(bd(_split(km * g_inv)))
        bg.append(bd(_split(bb * g_inv)))
        kdbd.append(_split(jnp.concatenate([km * g_end, bb * g_end], axis=0)))
        etot.append(jnp.exp(tot))
        s_p.append(s_scr[b, p])

    ss = each(lambda i: lax.dot_general(lhs2[i][0], s_p[i].astype(BF16), (((1,), (1,)), ((), ())),
                                        preferred_element_type=F32))
    kb = each(lambda i: _mm(lhs2[i], tuple(jnp.concatenate([x, y], axis=0)
                                           for x, y in zip(kg[i], bg[i])), nt=True))
    a_b = each(lambda i: jnp.where(strict, kb[i][:chunk, PAIR:], 0.0))
    b_b = each(lambda i: jnp.where(incl, kb[i][chunk:, PAIR:], 0.0))
    av = each(lambda i: _mm(_split(jnp.concatenate(
        [jnp.where(strict, kb[i][:chunk, :PAIR], 0.0), jnp.where(incl, kb[i][chunk:, :PAIR], 0.0)],
        axis=0)), bd(_split(vv[i]))))

    dg = each(lambda i: jnp.where(m8, a_b[i], 0.0))
    d2 = each(lambda i: pm1(dg[i], dg[i]))
    d3 = each(lambda i: pm1(dg[i], d2[i]))
    d4 = each(lambda i: pm1(d2[i], d2[i]))
    t1 = each(lambda i: eye - dg[i] + d2[i] - d3[i])
    inv = each(lambda i: t1[i] + pm1(t1[i], d4[i]))
    for m in levels:
        mlev = each(lambda i: pm1(inv[i], jnp.where(m, a_b[i], 0.0)))
        inv = each(lambda i: inv[i] - pm1(mlev[i], inv[i]))

    u = each(lambda i: pm1(inv[i], ss[i][:chunk] + av[i][:chunk]))
    upd = each(lambda i: _mm(_split(jnp.transpose(jnp.concatenate([vv[i], -u[i]], axis=0))),
                             kdbd[i]))
    bu = each(lambda i: pm1(b_b[i], u[i]))
    for i, (b, p) in enumerate(chains):
        sl = slice(p * PAIR, (p + 1) * PAIR)
        o_ref[b, :, sl] = ss[i][chunk:] + av[i][chunk:] - bu[i]
        s_scr[b, p] = s_p[i] * etot[i] + jnp.where(same_head, upd[i], 0.0)

    sout_ref[...] = s_scr[...]


def _rwkv_chunk(r, km, v, lw, kk, bb, s0, *, chunk, nb):
    nseq, t, _ = r.shape
    assert chunk == HD_B and nseq % nb == 0
    npair = H_B // 2
    row = pl.BlockSpec((nb, chunk, W_B), lambda b, c: (b, c, 0))
    st = pl.BlockSpec((nb, npair, PAIR, PAIR), lambda b, c: (b, 0, 0, 0))
    kern = functools.partial(_rwkv_chunk_kernel, chunk=chunk, nb=nb)
    return pl.pallas_call(
        kern,
        grid=(nseq // nb, t // chunk),
        in_specs=[row] * 6 + [st],
        out_specs=[row, st],
        out_shape=[jax.ShapeDtypeStruct((nseq, t, W_B), F32),
                   jax.ShapeDtypeStruct((nseq, npair, PAIR, PAIR), F32)],
        scratch_shapes=[pltpu.VMEM((nb, npair, PAIR, PAIR), F32)],
        compiler_params=_cparams(("parallel", "arbitrary")),
        name="rwkv_chunk",
    )(r, km, v, lw, kk, bb, s0)


MXU_N = 256


def _mix_residual(x_ref, oa_ref, o_ref, r_ref, km_ref, v_ref, g_ref, lnw_ref, lnb_ref, rk_ref,
                  wa_ref, wb_ref):
    ones = _head_ones(W_B)
    o = o_ref[...]
    cen = o - _head_sum(o, ones) * (1.0 / HD_B)
    head_sum1 = lambda z: jnp.dot(z.astype(BF16), ones, preferred_element_type=F32)
    var = head_sum1(cen * cen) * (1.0 / HD_B)
    on = cen * lax.rsqrt(var + GN_EPS) * lnw_ref[...] + lnb_ref[...]
    bonus = head_sum1(r_ref[...] * km_ref[...] * rk_ref[...]) * v_ref[...]
    ob = (on + bonus) * g_ref[...]
    return (x_ref[...]
            + jnp.dot(oa_ref[...].astype(BF16), wa_ref[...], preferred_element_type=F32)
            + jnp.dot(ob.astype(BF16), wb_ref[...], preferred_element_type=F32))


def _proj_residual(x_ref, o_ref, w_ref):
    return x_ref[...] + jnp.dot(o_ref[...].astype(BF16), w_ref[...],
                                preferred_element_type=F32)


def _res_ffn_kernel(*refs, residual, n_res, final_norm):
    g_ref, wg_ref, wu_ref, wd_ref, gf_ref, y_ref, xf_scr, x_scr, h_scr = refs[n_res:]
    x = residual(*refs[:n_res])
    xf_scr[...] = _rms(x, g_ref[...]).astype(BF16)
    x_scr[...] = x
    for j in range(wg_ref.shape[1] // MXU_N):
        cols = slice(j * MXU_N, (j + 1) * MXU_N)
        xf = xf_scr[...]
        hg = jnp.dot(xf, wg_ref[:, cols], preferred_element_type=F32)
        hu = jnp.dot(xf, wu_ref[:, cols], preferred_element_type=F32)
        h_scr[:, cols] = ((hg * _sigmoid(hg)) * hu).astype(BF16)
    y = x_scr[...] + jnp.dot(h_scr[...], wd_ref[...], preferred_element_type=F32)
    if final_norm:
        y = _rms(y, gf_ref[...])
    y_ref[...] = y


def _res_ffn(residual, row_ins, const_ins, g, wg, wu, wd, gf, *, tm, final_norm):
    n, d = row_ins[0].shape
    whole = lambda a: pl.BlockSpec(a.shape, lambda i: (0,) * a.ndim,
                                   pipeline_mode=pl.Buffered(1))
    consts = list(const_ins) + [g.reshape(1, d), wg, wu, wd, gf.reshape(1, d)]
    kern = functools.partial(_res_ffn_kernel, residual=residual,
                             n_res=len(row_ins) + len(const_ins), final_norm=final_norm)
    return pl.pallas_call(
        kern,
        grid=(n // tm,),
        in_specs=[pl.BlockSpec((tm, a.shape[1]), lambda i: (i, 0)) for a in row_ins]
                 + [whole(a) for a in consts],
        out_specs=pl.BlockSpec((tm, d), lambda i: (i, 0)),
        out_shape=jax.ShapeDtypeStruct((n, d), F32),
        scratch_shapes=[pltpu.VMEM((tm, d), BF16), pltpu.VMEM((tm, d), F32),
                        pltpu.VMEM((tm, wg.shape[1]), BF16)],
        compiler_params=_cparams(("parallel",)),
        name="res_ffn",
    )(*row_ins, *consts)


HGRN_SAFE_LOG = 80.0


def _hgrn_kernel(q_ref, f_ref, i_ref, g_ref, s0_ref, lbc_ref, gn_ref, o_ref, sout_ref,
                 s_scr, att_scr, *, layer, chunk, valid, nb):
    c = pl.program_id(1)

    @pl.when(c == 0)
    def _():
        s_scr[...] = s0_ref[...]

    lbc = lbc_ref[...]
    e = jnp.exp(lbc - jnp.max(lbc, axis=0, keepdims=True))
    sm = e / jnp.sum(e, axis=0, keepdims=True)
    lbf = sm[1:2, :]
    for l in range(2, layer + 1):
        lbf = lbf + sm[l:l + 1, :]

    nblk = chunk // SUBLANES
    sub = lax.broadcasted_iota(jnp.int32, (SUBLANES, DK_C), 0)
    lane_c = lax.broadcasted_iota(jnp.int32, (SUBLANES, chunk), 1)
    rowc = lax.broadcasted_iota(jnp.int32, (chunk, F_C), 0)
    tril = jnp.where(lax.broadcasted_iota(jnp.int32, (chunk, chunk), 0)
                     >= lax.broadcasted_iota(jnp.int32, (chunk, chunk), 1), 1.0, 0.0).astype(BF16)

    qs, ks, bs, ivs = [], [], [], []
    for b in range(nb):
        fr = f_ref[b]
        ez = jnp.exp(-jnp.abs(fr))
        rz = 1.0 / (1.0 + ez)
        sig = jnp.where(fr >= 0.0, rz, ez * rz)
        nsig = jnp.where(fr >= 0.0, ez * rz, rz)
        logf = jnp.log(lbf + (1.0 - lbf) * sig)
        kfull = (1.0 - lbf) * nsig
        if valid < chunk:
            logf = jnp.where(rowc < valid, logf, 0.0)
            kfull = jnp.where(rowc < valid, kfull, 0.0)
        hi, lo = _split(logf)
        bc = jnp.dot(jnp.concatenate([tril, tril], axis=1), jnp.concatenate([hi, lo], axis=0),
                     preferred_element_type=F32)
        qs.append(jax.nn.silu(q_ref[b]))
        ks.append(kfull)
        bs.append(bc)
        ivs.append(i_ref[b].astype(BF16))

    chains = [(b, h) for b in range(nb) for h in range(H_C)]
    each = lambda f: [f(i) for i in range(len(chains))]
    hsl = lambda h: slice(h * DK_C, (h + 1) * DK_C)
    q = each(lambda i: qs[chains[i][0]][:, hsl(chains[i][1])])
    k = each(lambda i: ks[chains[i][0]][:, hsl(chains[i][1])])
    bcs = each(lambda i: bs[chains[i][0]][:, hsl(chains[i][1])])
    iv = each(lambda i: ivs[chains[i][0]][:, hsl(chains[i][1])])
    s_h = each(lambda i: s_scr[chains[i][0], chains[i][1]])

    def off_diag(i, tb):
        lo = tb * SUBLANES
        rs = slice(lo, lo + SUBLANES)
        edge = bcs[i][lo - 1:lo, :]
        qi = q[i][rs, :] * jnp.exp(bcs[i][rs, :] - edge)
        ki = k[i][0:lo, :] * jnp.exp(edge - bcs[i][0:lo, :])
        ki = jnp.concatenate([ki, jnp.zeros((chunk - lo, DK_C), F32)], axis=0)
        return lax.dot_general(qi.astype(BF16), ki.astype(BF16), (((1,), (1,)), ((), ())),
                               preferred_element_type=F32)

    inter = each(lambda i: jnp.dot((q[i] * jnp.exp(bcs[i])).astype(BF16), s_h[i].astype(BF16),
                                   preferred_element_type=F32))


    sub_c = lax.broadcasted_iota(jnp.int32, (SUBLANES, chunk), 0)

    def att_on_matrix_unit():
        def block(i, tb):
            lo = tb * SUBLANES
            rs = slice(lo, lo + SUBLANES)
            dloc = bcs[i][rs, :] if tb == 0 else bcs[i][rs, :] - bcs[i][lo - 1:lo, :]
            qi = q[i][rs, :] * jnp.exp(dloc)
            parts = [k[i][rs, :] * jnp.exp(-dloc)]
            if tb > 0:
                parts.insert(0, k[i][0:lo, :] * jnp.exp(bcs[i][lo - 1:lo, :] - bcs[i][0:lo, :]))
            if chunk - lo - SUBLANES > 0:
                parts.append(jnp.zeros((chunk - lo - SUBLANES, DK_C), F32))
            ki = jnp.concatenate(parts, axis=0) if len(parts) > 1 else parts[0]
            return lax.dot_general(qi.astype(BF16), ki.astype(BF16), (((1,), (1,)), ((), ())),
                                   preferred_element_type=F32)

        blks = each(lambda i: [block(i, tb) for tb in range(nblk)])
        for i in range(len(chains)):
            for tb in range(nblk):
                att_scr[i, tb * SUBLANES:(tb + 1) * SUBLANES, :] = jnp.where(
                    lane_c <= sub_c + tb * SUBLANES, blks[i][tb], 0.0)

    def att_pairwise_diagonal():
        offd = each(lambda i: [off_diag(i, tb) for tb in range(1, nblk)])
        for i in range(len(chains)):
            for tb in range(nblk):
                rs = slice(tb * SUBLANES, (tb + 1) * SUBLANES)
                blk = jnp.zeros((SUBLANES, chunk), F32) if tb == 0 else offd[i][tb - 1]
                for j in range(SUBLANES):
                    s = tb * SUBLANES + j
                    if s >= valid:
                        break
                    diff = jnp.where(sub >= j, bcs[i][rs, :] - bcs[i][s:s + 1, :], -jnp.inf)
                    ev = jnp.exp(diff) * (q[i][rs, :] * k[i][s:s + 1, :])
                    blk = jnp.where(lane_c == s, jnp.sum(ev, axis=1, keepdims=True), blk)
                att_scr[i, rs, :] = blk

    def new_state(i):
        bl = bcs[i][chunk - 1:chunk, :]
        kd = k[i] * jnp.exp(bl - bcs[i])
        pieces = [kd, jnp.broadcast_to(jnp.exp(bl), (SUBLANES, DK_C))]
        fill = DK_C - chunk - SUBLANES
        if fill > 0:
            pieces.append(jnp.zeros((fill, DK_C), F32))
        xt = jnp.transpose(jnp.concatenate(pieces, axis=0))
        return (xt[:, chunk:chunk + 1] * s_h[i]
                + jnp.dot(xt[:, 0:chunk].astype(BF16), iv[i], preferred_element_type=F32))

    if nblk > 1:
        att_on_matrix_unit()
        worst = jnp.float32(0.0)
        for b in range(nb):
            win = bs[b] - jnp.where(rowc >= SUBLANES, pltpu.roll(bs[b], SUBLANES, 0), 0.0)
            worst = jnp.minimum(worst, jnp.min(win))
        pl.when(worst < -HGRN_SAFE_LOG)(att_pairwise_diagonal)
    else:
        att_pairwise_diagonal()

    o_h = each(lambda i: jnp.dot(att_scr[i].astype(BF16), iv[i], preferred_element_type=F32)
               + inter[i])
    s_new = each(new_state)
    for i, (b, h) in enumerate(chains):
        s_scr[b, h] = s_new[i]
    for b in range(nb):
        o = jnp.concatenate([o_h[b * H_C + h] for h in range(H_C)], axis=1)
        o_ref[b] = _rms(o, gn_ref[...]) * jax.nn.silu(g_ref[b])
    sout_ref[...] = s_scr[...]


def _hgrn(pc3, s0, lbc, gn, *, layer, chunk, valid, nb):
    nseq, t, _ = pc3.shape
    assert nseq % nb == 0
    col = lambda j: pl.BlockSpec((nb, chunk, F_C), lambda b, c, j=j: (b, c, j))
    st = pl.BlockSpec((nb, H_C, DK_C, DK_C), lambda b, c: (b, 0, 0, 0))
    kern = functools.partial(_hgrn_kernel, layer=layer, chunk=chunk, valid=valid, nb=nb)
    return pl.pallas_call(
        kern,
        grid=(nseq // nb, t // chunk),
        in_specs=[col(0), col(1), col(2), col(3), st,
                  pl.BlockSpec(lbc.shape, lambda b, c: (0, 0)),
                  pl.BlockSpec((1, D_MODEL), lambda b, c: (0, 0))],
        out_specs=[pl.BlockSpec((nb, chunk, D_MODEL), lambda b, c: (b, c, 0)), st],
        out_shape=[jax.ShapeDtypeStruct((nseq, t, D_MODEL), F32),
                   jax.ShapeDtypeStruct(s0.shape, F32)],
        scratch_shapes=[pltpu.VMEM((nb, H_C, DK_C, DK_C), F32),
                        pltpu.VMEM((nb * H_C, chunk, chunk), F32)],
        compiler_params=_cparams(("parallel", "arbitrary")),
        name="hgrn",
    )(pc3, pc3, pc3, pc3, s0, lbc, gn.reshape(1, D_MODEL))


def _block_diag(w):
    h, n, _ = w.shape
    eye = jnp.eye(h, dtype=w.dtype)
    return (eye[:, None, :, None] * w[:, :, None, :]).reshape(h * n, h * n)


def _fold(x):
    half = x.shape[-2] // 2
    return jnp.concatenate([x[..., :half, :], x[..., half:, :]], axis=-1)


def _unfold(x):
    half = x.shape[-1] // 2
    return jnp.concatenate([x[..., :half], x[..., half:]], axis=-2)


def _trunk(x2d, conv_st, h_st, shift_st, rs_st, hs_st, P, *, nbatch, nsteps, time_major):
    n = x2d.shape[0]
    if time_major:
        nseq, step, rows = 1, nbatch, n
        tm = n
    else:
        nseq, step, rows = nbatch, 1, 512
        tm = 512
    nchain = nbatch * H_B

    out_a, conv_new, h_new, r, km, v, lw, kk, bb, g, shift_new = _front(
        x2d, P['ln_mix'][0], P['w_in_ab'],
        conv_st, h_st, P['conv_w'], P['conv_b'], P['wr'], P['wi'], P['gr_b'], P['gi_b'],
        P['lru_lambda'],
        shift_st, P['mu_b'], P['w0_b'], P['a0_b'], P['w2p'], P['a2p'], P['g2_b'],
        P['kk_b'], P['ka_b'], nseq=nseq, step=step, rows=rows)

    if time_major:
        def to_lanes(z):
            z = z.reshape(nsteps, nbatch, H_B, HD_B).transpose(0, 3, 1, 2)
            return z.reshape(nsteps, HD_B, nchain)

        s0 = rs_st.transpose(2, 3, 0, 1).reshape(HD_B, HD_B, nchain)
        o, s_new = _rwkv_scan(to_lanes(r), to_lanes(km), to_lanes(v), to_lanes(lw), to_lanes(kk),
                              to_lanes(bb), s0, tb=nsteps)
        o = o.reshape(nsteps, HD_B, nbatch, H_B).transpose(0, 2, 3, 1).reshape(n, W_B)
        rs_new = s_new.reshape(HD_B, HD_B, nbatch, H_B).transpose(2, 3, 0, 1)
    else:
        z = jnp.zeros_like(rs_st[:, 0::2])
        s0 = jnp.concatenate([jnp.concatenate([rs_st[:, 0::2], z], axis=-1),
                              jnp.concatenate([z, rs_st[:, 1::2]], axis=-1)], axis=-2)
        seq3 = lambda y: y.reshape(nbatch, nsteps, W_B)
        o, s_new = _rwkv_chunk(seq3(r), seq3(km), seq3(v), seq3(lw), seq3(kk), seq3(bb), s0,
                               chunk=HD_B, nb=RWKV_SEQS_PER_STEP)
        o = o.reshape(n, W_B)
        rs_new = jnp.stack([s_new[:, :, :HD_B, :HD_B], s_new[:, :, HD_B:, HD_B:]],
                           axis=2).reshape(nbatch, H_B, HD_B, HD_B)

    vec = lambda p: p.reshape(1, W_B)
    x2 = _res_ffn(_mix_residual, [x2d, out_a, o, r, km, v, g],
                  [vec(P['lnx_w']), vec(P['lnx_b']), vec(P['rk_b']), P['w_out_a'], P['w_out_b']],
                  P['ln_ffn'][0], P['ffn_gate'][0], P['ffn_up'][0], P['ffn_down'][0],
                  P['ln_final'], tm=tm, final_norm=False)

    (pc,) = _norm_matmul(x2, P['ln_mix'][1], P['w_in_c'], (4 * F_C,), tm)
    if time_major:
        pc3 = pc.reshape(nsteps, nbatch, 4 * F_C).transpose(1, 0, 2)
        chunk = SUBLANES
        pc3 = jnp.pad(pc3, ((0, 0), (0, chunk - nsteps), (0, 0)))
    else:
        pc3 = pc.reshape(nbatch, nsteps, 4 * F_C)
        chunk = 64
    o3, hs_new = _hgrn(pc3, hs_st, P['lb_c'], P['gn_c'], layer=1, chunk=chunk,
                       valid=min(chunk, nsteps), nb=4 if time_major else 1)
    if time_major:
        o = o3[:, :nsteps].transpose(1, 0, 2).reshape(n, D_MODEL)
    else:
        o = o3.reshape(n, D_MODEL)
    y = _res_ffn(_proj_residual, [x2, o], [P['w_out_c']],
                 P['ln_ffn'][1], P['ffn_gate'][1], P['ffn_up'][1], P['ffn_down'][1],
                 P['ln_final'], tm=tm, final_norm=True)
    return y, conv_new, h_new, shift_new, rs_new, hs_new


def kernel(x_prompt, x_sample, state_rglru_conv, state_rglru_h, state_rwkv_shift, state_rwkv_S,
           state_hgrn_S, ln_mix, ln_ffn, ln_final, w_in_ab, conv_w, conv_b, gr_w, gr_b, gi_w,
           gi_b, lru_lambda, mu_b, w0_b, w2_b, a0_b, a2_b, g2_b, kk_b, ka_b, rk_b, lnx_w, lnx_b,
           w_out_ab, w_in_c, lb_c, gn_c, w_out_c, ffn_gate, ffn_up, ffn_down):
    bp, tp, _ = x_prompt.shape
    bs, ts, _ = x_sample.shape
    zpad_w = jnp.zeros((LORA_A, W_B), F32)
    zpad_a = jnp.zeros((LORA_W, W_B), F32)
    P = dict(
        ln_mix=ln_mix, ln_ffn=ln_ffn, ln_final=ln_final,
        w_in_ab=w_in_ab[0].astype(BF16), conv_w=conv_w[0], conv_b=conv_b[0],
        wr=_block_diag(gr_w[0]).astype(BF16), wi=_block_diag(gi_w[0]).astype(BF16),
        gr_b=gr_b[0], gi_b=gi_b[0], lru_lambda=lru_lambda[0], mu_b=mu_b[0], w0_b=w0_b[0],
        a0_b=a0_b[0],
        w2p=jnp.concatenate([w2_b[0], zpad_w], axis=0).astype(BF16),
        a2p=jnp.concatenate([zpad_a, a2_b[0]], axis=0).astype(BF16),
        g2_b=g2_b[0].astype(BF16), kk_b=kk_b[0], ka_b=ka_b[0], rk_b=rk_b[0],
        lnx_w=lnx_w[0], lnx_b=lnx_b[0],
        w_out_a=w_out_ab[0, :W_A].astype(BF16), w_out_b=w_out_ab[0, W_A:].astype(BF16),
        w_in_c=w_in_c[0].astype(BF16), lb_c=lb_c, gn_c=gn_c[0],
        w_out_c=w_out_c[0].astype(BF16), ffn_gate=ffn_gate.astype(BF16),
        ffn_up=ffn_up.astype(BF16), ffn_down=ffn_down.astype(BF16))

    yp, p_conv, p_h, p_shift, p_rs, p_hs = _trunk(
        x_prompt.reshape(bp * tp, D_MODEL),
        jnp.zeros((bp, CONV_W - 1, W_A), F32), jnp.zeros((bp, 1, W_A), F32),
        jnp.zeros((bp, 1, P_B), F32), jnp.zeros((bp, H_B, HD_B, HD_B), F32),
        jnp.zeros((bp, H_C, DK_C, DK_C), F32), P, nbatch=bp, nsteps=tp, time_major=False)

    ys, s_conv, s_h, s_shift, s_rs, s_hs = _trunk(
        x_sample.transpose(1, 0, 2).reshape(ts * bs, D_MODEL),
        state_rglru_conv[0].transpose(1, 0, 2).reshape(1, (CONV_W - 1) * bs, W_A),
        state_rglru_h[0].reshape(1, bs, W_A), state_rwkv_shift[0].reshape(1, bs, P_B),
        state_rwkv_S[0], state_hgrn_S[0], P, nbatch=bs, nsteps=ts, time_major=True)

    return (yp.reshape(bp, tp, D_MODEL),
            ys.reshape(ts, bs, D_MODEL).transpose(1, 0, 2),
            p_conv[None], p_h.reshape(1, bp, W_A), p_shift.reshape(1, bp, P_B), p_rs[None],
            p_hs[None],
            s_conv.reshape(CONV_W - 1, bs, W_A).transpose(1, 0, 2)[None],
            s_h.reshape(1, bs, W_A), s_shift.reshape(1, bs, P_B), s_rs[None], s_hs[None])
```

```python
import functools

import jax
import jax.numpy as jnp
from jax import lax
from jax.experimental import pallas as pl
from jax.experimental.pallas import tpu as pltpu

F32 = jnp.float32
BF16 = jnp.bfloat16

D_MODEL = 1024
W_A = 512
H_A = 8
CONV_W = 4
LRU_C = 8.0
W_B = 512
HD_B = 64
H_B = 8
LORA_W = 64
LORA_A = 64
LORA_G = 128
P_A = 2 * W_A
P_B = 3 * W_B + LORA_W + LORA_A + LORA_G
DK_C = 128
H_C = 8
F_C = 1024
D_FF = 2816
RMS_EPS = 1e-6
GN_EPS = 64e-5

LANES = 128
SUBLANES = 8
VMEM_LIMIT = 56 * 1024 * 1024


def _cparams(sem):
    return pltpu.CompilerParams(dimension_semantics=sem, vmem_limit_bytes=VMEM_LIMIT)


def _softplus(x):
    return jnp.maximum(x, 0.0) + jnp.log1p(jnp.exp(-jnp.abs(x)))

def _sigmoid(x):
    return jax.nn.sigmoid(x)


def _rms(x, g):
    ms = jnp.mean(x * x, axis=-1, keepdims=True)
    return x * lax.rsqrt(ms + RMS_EPS) * g


def _dot(a, b):
    return jnp.dot(a.astype(BF16), b.astype(BF16), preferred_element_type=F32)


def _split(x):
    hi = x.astype(BF16)
    lo = (x - hi.astype(F32)).astype(BF16)
    return hi, lo


def _mm(a, b, nt=False):
    ah, al = a
    bh, bl = b
    lhs = jnp.concatenate([ah, al, ah], axis=1)
    if nt:
        rhs = jnp.concatenate([bh, bh, bl], axis=1)
        return lax.dot_general(lhs, rhs, (((1,), (1,)), ((), ())), preferred_element_type=F32)
    rhs = jnp.concatenate([bh, bh, bl], axis=0)
    return jnp.dot(lhs, rhs, preferred_element_type=F32)


def _head_ones(width):
    r = lax.broadcasted_iota(jnp.int32, (width, width), 0) // HD_B
    c = lax.broadcasted_iota(jnp.int32, (width, width), 1) // HD_B
    return jnp.where(r == c, 1.0, 0.0).astype(BF16)


def _head_sum(x, ones):
    hi, lo = _split(x)
    return jnp.dot(jnp.concatenate([hi, lo], axis=1), jnp.concatenate([ones, ones], axis=0),
                   preferred_element_type=F32)


def _norm_matmul_kernel(x_ref, g_ref, w_ref, *o_refs):
    xn = _rms(x_ref[...], g_ref[...]).astype(BF16)
    off = 0
    for o_ref in o_refs:
        n = o_ref.shape[-1]
        o_ref[...] = jnp.dot(xn, w_ref[:, off:off + n], preferred_element_type=F32)
        off += n


def _norm_matmul(x, g, w, splits, tm):
    n, d = x.shape
    p = w.shape[1]
    assert sum(splits) == p and n % tm == 0
    return pl.pallas_call(
        _norm_matmul_kernel,
        grid=(n // tm,),
        in_specs=[pl.BlockSpec((tm, d), lambda i: (i, 0)),
                  pl.BlockSpec((1, d), lambda i: (0, 0)),
                  pl.BlockSpec((d, p), lambda i: (0, 0), pipeline_mode=pl.Buffered(1))],
        out_specs=[pl.BlockSpec((tm, s), lambda i: (i, 0)) for s in splits],
        out_shape=[jax.ShapeDtypeStruct((n, s), F32) for s in splits],
        compiler_params=_cparams(("parallel",)),
        name="norm_matmul",
    )(x, g.reshape(1, d), w)


def _rglru_body(u_in, gate, cw_ref, cb_ref, wr_ref, wi_ref, br_ref, bi_ref, lam_ref,
                out_ref, cnew_ref, hnew_ref, xbuf, hcar, bbuf, *, step, rows, pad):
    hist = (CONV_W - 1) * step

    if step == 1:
        u = u_in
        prev = xbuf[0:SUBLANES, :]
        sub8 = lax.broadcasted_iota(jnp.int32, (SUBLANES, W_A), 0)
        conv = cb_ref[...]
        for j in range(CONV_W):
            d = CONV_W - 1 - j
            if d == 0:
                ush = u
            else:
                rolled = pltpu.roll(u, d, 0)
                head = jnp.where(sub8 < d, pltpu.roll(prev, d, 0), rolled[0:SUBLANES, :])
                ush = jnp.concatenate([head, rolled[SUBLANES:, :]], axis=0)
            conv = conv + ush * cw_ref[j:j + 1, :]
        cnew_ref[0] = u[rows - hist:rows, :]
        xbuf[0:SUBLANES, :] = u[rows - SUBLANES:rows, :]
    else:
        xbuf[pad:pad + rows, :] = u_in
        conv = cb_ref[...] + xbuf[pad - hist:pad - hist + rows, :] * cw_ref[0:1, :]
        for j in range(1, CONV_W):
            o = pad - hist + j * step
            conv = conv + xbuf[o:o + rows, :] * cw_ref[j:j + 1, :]
        tail = xbuf[pad + rows - hist:pad + rows, :]
        cnew_ref[0] = tail
        xbuf[pad - hist:pad, :] = tail

    ub = conv.astype(BF16)
    r = _sigmoid(jnp.dot(ub, wr_ref[...], preferred_element_type=F32) + br_ref[...])
    ig = _sigmoid(jnp.dot(ub, wi_ref[...], preferred_element_type=F32) + bi_ref[...])
    log_a = (-LRU_C) * r * _softplus(-lam_ref[...])
    a = jnp.exp(log_a)
    bterm = jnp.sqrt(-jnp.tanh(log_a) * (a * a + 1.0)) * (ig * conv)
    bbuf[...] = bterm
    bbuf[0:step, :] = bterm[0:step, :] + a[0:step, :] * hcar[...]
    bv = bbuf[...]

    def scan_levels(a, bv, pos, first, count):
        d = first
        while d < first * count:
            m = pos >= d
            bv = jnp.where(m, a * pltpu.roll(bv, d, 0) + bv, bv)
            a = jnp.where(m, a * pltpu.roll(a, d, 0), a)
            d *= 2
        return a, bv

    row = lax.broadcasted_iota(jnp.int32, (rows, W_A), 0)
    a, bv = scan_levels(a, bv, row, step, rows // step)
    hcar[...] = bv[rows - step:rows, :]
    hnew_ref[0] = bv[rows - step:rows, :]
    out_ref[...] = jax.nn.gelu(gate) * bv


def _rwkv_prep_body(pf, mu_ref, w0_ref, a0_ref, w2_ref, a2_ref, g2_ref, kkp_ref, kap_ref,
                    r_ref, km_ref, v_ref, lw_ref, kk_ref, bb_ref, g_ref, snew_ref, pbuf,
                    *, step, rows, pad):

    if step == 1:
        rolled = pltpu.roll(pf, 1, 0)
        sub8 = lax.broadcasted_iota(jnp.int32, (SUBLANES, P_B), 0)
        head = jnp.where(sub8 < 1, pltpu.roll(pbuf[0:SUBLANES, :], 1, 0), rolled[0:SUBLANES, :])
        shifted = jnp.concatenate([head, rolled[SUBLANES:, :]], axis=0)
        pbuf[0:SUBLANES, :] = pf[rows - SUBLANES:rows, :]
    else:
        pbuf[pad:pad + rows, :] = pf
        shifted = pbuf[pad - step:pad - step + rows, :]
        pbuf[pad - step:pad, :] = pf[rows - step:rows, :]
    snew_ref[0] = pf[rows - step:rows, :]

    m = pf + (shifted - pf) * mu_ref[...]
    o1 = 3 * W_B
    r_ref[...] = m[:, 0:W_B]
    kraw = m[:, W_B:2 * W_B]
    v_ref[...] = m[:, 2 * W_B:o1]
    xwa = m[:, o1:o1 + LORA_W + LORA_A]
    xg = m[:, o1 + LORA_W + LORA_A:]
    lane = lax.broadcasted_iota(jnp.int32, xwa.shape, 1)
    lhs = jnp.where(lane < LORA_W, jnp.tanh(xwa), xwa).astype(BF16)
    lw = jnp.dot(lhs, w2_ref[...], preferred_element_type=F32)
    la = jnp.dot(lhs, a2_ref[...], preferred_element_type=F32)
    w_log = -_softplus(-(w0_ref[...] + lw)) - 0.5
    lw_ref[...] = -jnp.exp(w_log)
    a = _sigmoid(a0_ref[...] + la)
    g_ref[...] = jnp.dot(_sigmoid(xg).astype(BF16), g2_ref[...], preferred_element_type=F32)
    kk = kraw * kkp_ref[...]
    kk = kk * lax.rsqrt(jnp.maximum(_head_sum(kk * kk, _head_ones(W_B)), 1e-24))
    kk_ref[...] = kk
    bb_ref[...] = kk * a
    km_ref[...] = kraw * (1.0 + (a - 1.0) * kap_ref[...])


N_RGLRU_IN = 9
N_PREP_IN = 9


def _front_kernel(*refs, step, rows, pad_a, pad_b, nsplit):
    x_ref, gn_ref, w_ref = refs[:3]
    rg_in = refs[3:3 + N_RGLRU_IN]
    pp_in = refs[3 + N_RGLRU_IN:3 + N_RGLRU_IN + N_PREP_IN]
    outs = refs[3 + N_RGLRU_IN + N_PREP_IN:]
    rg_out, pp_out = outs[:3], outs[3:11]
    xbuf, hcar, bbuf, pbuf, p_scr = outs[11:]
    part = rows // nsplit

    @pl.when(pl.program_id(1) == 0)
    def _():
        hist = (CONV_W - 1) * step
        xbuf[pad_a - hist:pad_a, :] = rg_in[0][0]
        hcar[...] = rg_in[1][0]
        pbuf[pad_b - step:pad_b, :] = pp_in[0][0]

    xn = _rms(x_ref[...], gn_ref[...]).astype(BF16)
    for s in range(nsplit):
        p_scr[s] = jnp.dot(xn[s * part:(s + 1) * part, :], w_ref[...],
                           preferred_element_type=F32)
    for s in range(nsplit):
        sub = lambda ref, s=s: ref.at[s * part:(s + 1) * part, :]
        _rglru_body(p_scr[s, :, 0:W_A], p_scr[s, :, W_A:P_A], *rg_in[2:],
                    sub(rg_out[0]), rg_out[1], rg_out[2], xbuf, hcar, bbuf,
                    step=step, rows=part, pad=pad_a)
        _rwkv_prep_body(p_scr[s, :, P_A:], *pp_in[1:], *[sub(r) for r in pp_out[:7]],
                        pp_out[7], pbuf, step=step, rows=part, pad=pad_b)


def _front(x, gn, w, conv_st, h_st, cw, cb, wr, wi, br, bi, lam,
           prev, mu, w0, a0, w2p, a2p, g2, kkp, kap, *, nseq, step, rows):
    n, d = x.shape
    nt = n // (nseq * rows)
    hist = (CONV_W - 1) * step
    pad_a = max(SUBLANES, hist)
    pad_b = max(SUBLANES, step)
    row_map = lambda b, t: (b * nt + t, 0)
    seq_map = lambda b, t: (b, 0, 0)
    whole = lambda a: pl.BlockSpec(a.shape, lambda b, t: (0,) * a.ndim,
                                   pipeline_mode=pl.Buffered(1))
    va = lambda p: p.reshape(1, W_A)
    vb = lambda p: p.reshape(1, W_B)
    consts_a = [cw, va(cb), wr, wi, va(br), va(bi), va(lam)]
    consts_b = [mu.reshape(1, P_B), vb(w0), vb(a0), w2p, a2p, g2, vb(kkp), vb(kap)]
    nsplit = 2 if step == 1 else 1
    part = rows // nsplit
    kern = functools.partial(_front_kernel, step=step, rows=rows, pad_a=pad_a, pad_b=pad_b,
                             nsplit=nsplit)
    tile = lambda wd: pl.BlockSpec((rows, wd), row_map)
    return pl.pallas_call(
        kern,
        grid=(nseq, nt),
        in_specs=[tile(d), whole(gn.reshape(1, d)), whole(w),
                  pl.BlockSpec((1, hist, W_A), seq_map), pl.BlockSpec((1, step, W_A), seq_map)]
                 + [whole(a) for a in consts_a]
                 + [pl.BlockSpec((1, step, P_B), seq_map)] + [whole(a) for a in consts_b],
        out_specs=[tile(W_A), pl.BlockSpec((1, hist, W_A), seq_map),
                   pl.BlockSpec((1, step, W_A), seq_map)]
                  + [tile(W_B)] * 7 + [pl.BlockSpec((1, step, P_B), seq_map)],
        out_shape=[jax.ShapeDtypeStruct((n, W_A), F32),
                   jax.ShapeDtypeStruct((nseq, hist, W_A), F32),
                   jax.ShapeDtypeStruct((nseq, step, W_A), F32)]
                  + [jax.ShapeDtypeStruct((n, W_B), F32)] * 7
                  + [jax.ShapeDtypeStruct((nseq, step, P_B), F32)],
        scratch_shapes=[pltpu.VMEM((pad_a + part, W_A), F32),
                        pltpu.VMEM((step, W_A), F32),
                        pltpu.VMEM((part, W_A), F32),
                        pltpu.VMEM((pad_b + part, P_B), F32),
                        pltpu.VMEM((nsplit, part, P_A + P_B), F32)],
        compiler_params=_cparams(("parallel", "arbitrary")),
        name="front",
    )(x, gn.reshape(1, d), w, conv_st, h_st, *consts_a, prev, *consts_b)


VGROUP = 4


def _rwkv_scan_kernel(r_ref, km_ref, v_ref, lw_ref, kk_ref, bb_ref, s0_ref, o_ref, sout_ref,
                      s_scr, *, tb):
    tblk = pl.program_id(1)

    @pl.when(tblk == 0)
    def _():
        s_scr[...] = s0_ref[...]

    def colsum(x):
        return jnp.sum(x, axis=0, keepdims=True)

    sub = lax.broadcasted_iota(jnp.int32, (VGROUP, LANES), 0)
    for g in range(HD_B // VGROUP):
        vbase = g * VGROUP

        def step(t, state, vbase=vbase):
            kk = kk_ref[t]
            w = jnp.exp(lw_ref[t])
            bvec = bb_ref[t]
            km = km_ref[t]
            r = r_ref[t]
            vt = v_ref[t, vbase:vbase + VGROUP, :]
            new = []
            otile = jnp.zeros((VGROUP, LANES), F32)
            for i in range(VGROUP):
                sk = colsum(state[i] * kk)
                sv = state[i] * w - sk * bvec + vt[i:i + 1, :] * km
                new.append(sv)
                otile = jnp.where(sub == i, colsum(sv * r), otile)
            o_ref[t, vbase:vbase + VGROUP, :] = otile
            return tuple(new)

        state = tuple(s_scr[vbase + i] for i in range(VGROUP))
        state = lax.fori_loop(0, tb, step, state, unroll=2)
        for i in range(VGROUP):
            s_scr[vbase + i] = state[i]

    sout_ref[...] = s_scr[...]


def _rwkv_scan(r, km, v, lw, kk, bb, s0, *, tb):
    nt, _, nl = r.shape
    seq = pl.BlockSpec((tb, HD_B, LANES), lambda g, t: (t, 0, g))
    st = pl.BlockSpec((HD_B, HD_B, LANES), lambda g, t: (0, 0, g))
    kern = functools.partial(_rwkv_scan_kernel, tb=tb)
    return pl.pallas_call(
        kern,
        grid=(nl // LANES, nt // tb),
        in_specs=[seq] * 6 + [st],
        out_specs=[seq, st],
        out_shape=[jax.ShapeDtypeStruct((nt, HD_B, nl), F32),
                   jax.ShapeDtypeStruct((HD_B, HD_B, nl), F32)],
        scratch_shapes=[pltpu.VMEM((HD_B, HD_B, LANES), F32)],
        compiler_params=_cparams(("parallel", "arbitrary")),
        name="rwkv_scan",
    )(r, km, v, lw, kk, bb, s0)


PAIR = 2 * HD_B
RWKV_SEQS_PER_STEP = 8


def _rwkv_chunk_kernel(r_ref, km_ref, v_ref, lw_ref, kk_ref, bb_ref, s0_ref, o_ref, sout_ref,
                       s_scr, *, chunk, nb):
    c = pl.program_id(1)

    @pl.when(c == 0)
    def _():
        s_scr[...] = s0_ref[...]

    lane = lax.broadcasted_iota(jnp.int32, (chunk, PAIR), 1)
    row = lax.broadcasted_iota(jnp.int32, (chunk, PAIR), 0)
    low_half = lane < HD_B
    li = jnp.where(low_half, lane, lane - HD_B)
    strict = li < row
    incl = li <= row
    eye = jnp.where(li == row, 1.0, 0.0)
    same = lambda n: (li // n) == (row // n)
    m8 = same(8)
    levels = []
    n = 16
    prev = m8
    while n <= chunk:
        cur = same(n)
        levels.append(jnp.logical_and(cur, jnp.logical_not(prev)))
        prev = cur
        n *= 2
    sq_r = lax.broadcasted_iota(jnp.int32, (PAIR, PAIR), 0)
    sq_c = lax.broadcasted_iota(jnp.int32, (PAIR, PAIR), 1)
    same_head = (sq_r < HD_B) == (sq_c < HD_B)
    low_bf = lax.broadcasted_iota(jnp.int32, (chunk, PAIR), 1) < HD_B

    def bd(s):
        return tuple(jnp.concatenate([jnp.where(low_bf, x, jnp.zeros_like(x)),
                                      jnp.where(low_bf, jnp.zeros_like(x), x)], axis=0)
                     for x in s)

    def pm1(x, y):
        return jnp.dot(x.astype(BF16), bd((y.astype(BF16),))[0], preferred_element_type=F32)

    chains = [(b, p) for b in range(nb) for p in range(H_B // 2)]
    each = lambda f: [f(i) for i in range(len(chains))]

    vv, lhs2, kg, bg, kdbd, etot, s_p = [], [], [], [], [], [], []
    for b, p in chains:
        sl = slice(p * PAIR, (p + 1) * PAIR)
        r = r_ref[b, :, sl]
        km = km_ref[b, :, sl]
        lw = lw_ref[b, :, sl]
        kk = kk_ref[b, :, sl]
        bb = bb_ref[b, :, sl]
        cum = lw
        d = 1
        while d < chunk:
            cum = cum + jnp.where(row >= d, pltpu.roll(cum, d, 0), 0.0)
            d *= 2
        tot = cum[chunk - 1:chunk, :]
        g_end = jnp.exp(tot - cum)
        g_inv = jnp.exp(-cum)
        vv.append(v_ref[b, :, sl])
        lhs2.append(_split(jnp.concatenate([kk * jnp.exp(cum - lw), r * jnp.exp(cum)], axis=0)))
        kg.append(bd(_split(km * g_inv)))
        bg.append(bd(_split(bb * g_inv)))
        kdbd.append(_split(jnp.concatenate([km * g_end, bb * g_end], axis=0)))
        etot.append(jnp.exp(tot))
        s_p.append(s_scr[b, p])

    ss = each(lambda i: lax.dot_general(lhs2[i][0], s_p[i].astype(BF16), (((1,), (1,)), ((), ())),
                                        preferred_element_type=F32))
    kb = each(lambda i: _mm(lhs2[i], tuple(jnp.concatenate([x, y], axis=0)
                                           for x, y in zip(kg[i], bg[i])), nt=True))
    a_b = each(lambda i: jnp.where(strict, kb[i][:chunk, PAIR:], 0.0))
    b_b = each(lambda i: jnp.where(incl, kb[i][chunk:, PAIR:], 0.0))
    av = each(lambda i: _mm(_split(jnp.concatenate(
        [jnp.where(strict, kb[i][:chunk, :PAIR], 0.0), jnp.where(incl, kb[i][chunk:, :PAIR], 0.0)],
        axis=0)), bd(_split(vv[i]))))

    dg = each(lambda i: jnp.where(m8, a_b[i], 0.0))
    d2 = each(lambda i: pm1(dg[i], dg[i]))
    d3 = each(lambda i: pm1(dg[i], d2[i]))
    d4 = each(lambda i: pm1(d2[i], d2[i]))
    t1 = each(lambda i: eye - dg[i] + d2[i] - d3[i])
    inv = each(lambda i: t1[i] + pm1(t1[i], d4[i]))
    for m in levels:
        mlev = each(lambda i: pm1(inv[i], jnp.where(m, a_b[i], 0.0)))
        inv = each(lambda i: inv[i] - pm1(mlev[i], inv[i]))

    u = each(lambda i: pm1(inv[i], ss[i][:chunk] + av[i][:chunk]))
    upd = each(lambda i: _mm(_split(jnp.transpose(jnp.concatenate([vv[i], -u[i]], axis=0))),
                             kdbd[i]))
    bu = each(lambda i: pm1(b_b[i], u[i]))
    for i, (b, p) in enumerate(chains):
        sl = slice(p * PAIR, (p + 1) * PAIR)
        o_ref[b, :, sl] = ss[i][chunk:] + av[i][chunk:] - bu[i]
        s_scr[b, p] = s_p[i] * etot[i] + jnp.where(same_head, upd[i], 0.0)

    sout_ref[...] = s_scr[...]


def _rwkv_chunk(r, km, v, lw, kk, bb, s0, *, chunk, nb):
    nseq, t, _ = r.shape
    assert chunk == HD_B and nseq % nb == 0
    npair = H_B // 2
    row = pl.BlockSpec((nb, chunk, W_B), lambda b, c: (b, c, 0))
    st = pl.BlockSpec((nb, npair, PAIR, PAIR), lambda b, c: (b, 0, 0, 0))
    kern = functools.partial(_rwkv_chunk_kernel, chunk=chunk, nb=nb)
    return pl.pallas_call(
        kern,
        grid=(nseq // nb, t // chunk),
        in_specs=[row] * 6 + [st],
        out_specs=[row, st],
        out_shape=[jax.ShapeDtypeStruct((nseq, t, W_B), F32),
                   jax.ShapeDtypeStruct((nseq, npair, PAIR, PAIR), F32)],
        scratch_shapes=[pltpu.VMEM((nb, npair, PAIR, PAIR), F32)],
        compiler_params=_cparams(("parallel", "arbitrary")),
        name="rwkv_chunk",
    )(r, km, v, lw, kk, bb, s0)


MXU_N = 256


def _mix_residual(x_ref, oa_ref, o_ref, r_ref, km_ref, v_ref, g_ref, lnw_ref, lnb_ref, rk_ref,
                  wa_ref, wb_ref):
    ones = _head_ones(W_B)
    o = o_ref[...]
    cen = o - _head_sum(o, ones) * (1.0 / HD_B)
    head_sum1 = lambda z: jnp.dot(z.astype(BF16), ones, preferred_element_type=F32)
    var = head_sum1(cen * cen) * (1.0 / HD_B)
    on = cen * lax.rsqrt(var + GN_EPS) * lnw_ref[...] + lnb_ref[...]
    bonus = head_sum1(r_ref[...] * km_ref[...] * rk_ref[...]) * v_ref[...]
    ob = (on + bonus) * g_ref[...]
    return (x_ref[...]
            + jnp.dot(oa_ref[...].astype(BF16), wa_ref[...], preferred_element_type=F32)
            + jnp.dot(ob.astype(BF16), wb_ref[...], preferred_element_type=F32))


def _proj_residual(x_ref, o_ref, w_ref):
    return x_ref[...] + jnp.dot(o_ref[...].astype(BF16), w_ref[...],
                                preferred_element_type=F32)


def _res_ffn_kernel(*refs, residual, n_res, final_norm):
    g_ref, wg_ref, wu_ref, wd_ref, gf_ref, y_ref, xf_scr, x_scr, h_scr = refs[n_res:]
    x = residual(*refs[:n_res])
    xf_scr[...] = _rms(x, g_ref[...]).astype(BF16)
    x_scr[...] = x
    for j in range(wg_ref.shape[1] // MXU_N):
        cols = slice(j * MXU_N, (j + 1) * MXU_N)
        xf = xf_scr[...]
        hg = jnp.dot(xf, wg_ref[:, cols], preferred_element_type=F32)
        hu = jnp.dot(xf, wu_ref[:, cols], preferred_element_type=F32)
        h_scr[:, cols] = ((hg * _sigmoid(hg)) * hu).astype(BF16)
    y = x_scr[...] + jnp.dot(h_scr[...], wd_ref[...], preferred_element_type=F32)
    if final_norm:
        y = _rms(y, gf_ref[...])
    y_ref[...] = y


def _res_ffn(residual, row_ins, const_ins, g, wg, wu, wd, gf, *, tm, final_norm):
    n, d = row_ins[0].shape
    whole = lambda a: pl.BlockSpec(a.shape, lambda i: (0,) * a.ndim,
                                   pipeline_mode=pl.Buffered(1))
    consts = list(const_ins) + [g.reshape(1, d), wg, wu, wd, gf.reshape(1, d)]
    kern = functools.partial(_res_ffn_kernel, residual=residual,
                             n_res=len(row_ins) + len(const_ins), final_norm=final_norm)
    return pl.pallas_call(
        kern,
        grid=(n // tm,),
        in_specs=[pl.BlockSpec((tm, a.shape[1]), lambda i: (i, 0)) for a in row_ins]
                 + [whole(a) for a in consts],
        out_specs=pl.BlockSpec((tm, d), lambda i: (i, 0)),
        out_shape=jax.ShapeDtypeStruct((n, d), F32),
        scratch_shapes=[pltpu.VMEM((tm, d), BF16), pltpu.VMEM((tm, d), F32),
                        pltpu.VMEM((tm, wg.shape[1]), BF16)],
        compiler_params=_cparams(("parallel",)),
        name="res_ffn",
    )(*row_ins, *consts)


HGRN_SAFE_LOG = 80.0


def _hgrn_kernel(q_ref, f_ref, i_ref, g_ref, s0_ref, lbc_ref, gn_ref, o_ref, sout_ref,
                 s_scr, att_scr, *, layer, chunk, valid, nb):
    c = pl.program_id(1)

    @pl.when(c == 0)
    def _():
        s_scr[...] = s0_ref[...]

    lbc = lbc_ref[...]
    e = jnp.exp(lbc - jnp.max(lbc, axis=0, keepdims=True))
    sm = e / jnp.sum(e, axis=0, keepdims=True)
    lbf = sm[1:2, :]
    for l in range(2, layer + 1):
        lbf = lbf + sm[l:l + 1, :]

    nblk = chunk // SUBLANES
    sub = lax.broadcasted_iota(jnp.int32, (SUBLANES, DK_C), 0)
    lane_c = lax.broadcasted_iota(jnp.int32, (SUBLANES, chunk), 1)
    rowc = lax.broadcasted_iota(jnp.int32, (chunk, F_C), 0)
    tril = jnp.where(lax.broadcasted_iota(jnp.int32, (chunk, chunk), 0)
                     >= lax.broadcasted_iota(jnp.int32, (chunk, chunk), 1), 1.0, 0.0).astype(BF16)

    qs, ks, bs, ivs = [], [], [], []
    for b in range(nb):
        fr = f_ref[b]
        ez = jnp.exp(-jnp.abs(fr))
        rz = 1.0 / (1.0 + ez)
        sig = jnp.where(fr >= 0.0, rz, ez * rz)
        nsig = jnp.where(fr >= 0.0, ez * rz, rz)
        logf = jnp.log(lbf + (1.0 - lbf) * sig)
        kfull = (1.0 - lbf) * nsig
        if valid < chunk:
            logf = jnp.where(rowc < valid, logf, 0.0)
            kfull = jnp.where(rowc < valid, kfull, 0.0)
        hi, lo = _split(logf)
        bc = jnp.dot(jnp.concatenate([tril, tril], axis=1), jnp.concatenate([hi, lo], axis=0),
                     preferred_element_type=F32)
        qs.append(jax.nn.silu(q_ref[b]))
        ks.append(kfull)
        bs.append(bc)
        ivs.append(i_ref[b].astype(BF16))

    chains = [(b, h) for b in range(nb) for h in range(H_C)]
    each = lambda f: [f(i) for i in range(len(chains))]
    hsl = lambda h: slice(h * DK_C, (h + 1) * DK_C)
    q = each(lambda i: qs[chains[i][0]][:, hsl(chains[i][1])])
    k = each(lambda i: ks[chains[i][0]][:, hsl(chains[i][1])])
    bcs = each(lambda i: bs[chains[i][0]][:, hsl(chains[i][1])])
    iv = each(lambda i: ivs[chains[i][0]][:, hsl(chains[i][1])])
    s_h = each(lambda i: s_scr[chains[i][0], chains[i][1]])

    def off_diag(i, tb):
        lo = tb * SUBLANES
        rs = slice(lo, lo + SUBLANES)
        edge = bcs[i][lo - 1:lo, :]
        qi = q[i][rs, :] * jnp.exp(bcs[i][rs, :] - edge)
        ki = k[i][0:lo, :] * jnp.exp(edge - bcs[i][0:lo, :])
        ki = jnp.concatenate([ki, jnp.zeros((chunk - lo, DK_C), F32)], axis=0)
        return lax.dot_general(qi.astype(BF16), ki.astype(BF16), (((1,), (1,)), ((), ())),
                               preferred_element_type=F32)

    inter = each(lambda i: jnp.dot((q[i] * jnp.exp(bcs[i])).astype(BF16), s_h[i].astype(BF16),
                                   preferred_element_type=F32))


    sub_c = lax.broadcasted_iota(jnp.int32, (SUBLANES, chunk), 0)

    def att_on_matrix_unit():
        def block(i, tb):
            lo = tb * SUBLANES
            rs = slice(lo, lo + SUBLANES)
            dloc = bcs[i][rs, :] if tb == 0 else bcs[i][rs, :] - bcs[i][lo - 1:lo, :]
            qi = q[i][rs, :] * jnp.exp(dloc)
            parts = [k[i][rs, :] * jnp.exp(-dloc)]
            if tb > 0:
                parts.insert(0, k[i][0:lo, :] * jnp.exp(bcs[i][lo - 1:lo, :] - bcs[i][0:lo, :]))
            if chunk - lo - SUBLANES > 0:
                parts.append(jnp.zeros((chunk - lo - SUBLANES, DK_C), F32))
            ki = jnp.concatenate(parts, axis=0) if len(parts) > 1 else parts[0]
            return lax.dot_general(qi.astype(BF16), ki.astype(BF16), (((1,), (1,)), ((), ())),
                                   preferred_element_type=F32)

        blks = each(lambda i: [block(i, tb) for tb in range(nblk)])
        for i in range(len(chains)):
            for tb in range(nblk):
                att_scr[i, tb * SUBLANES:(tb + 1) * SUBLANES, :] = jnp.where(
                    lane_c <= sub_c + tb * SUBLANES, blks[i][tb], 0.0)

    def att_pairwise_diagonal():
        offd = each(lambda i: [off_diag(i, tb) for tb in range(1, nblk)])
        for i in range(len(chains)):
            for tb in range(nblk):
                rs = slice(tb * SUBLANES, (tb + 1) * SUBLANES)
                blk = jnp.zeros((SUBLANES, chunk), F32) if tb == 0 else offd[i][tb - 1]
                for j in range(SUBLANES):
                    s = tb * SUBLANES + j
                    if s >= valid:
                        break
                    diff = jnp.where(sub >= j, bcs[i][rs, :] - bcs[i][s:s + 1, :], -jnp.inf)
                    ev = jnp.exp(diff) * (q[i][rs, :] * k[i][s:s + 1, :])
                    blk = jnp.where(lane_c == s, jnp.sum(ev, axis=1, keepdims=True), blk)
                att_scr[i, rs, :] = blk

    def new_state(i):
        bl = bcs[i][chunk - 1:chunk, :]
        kd = k[i] * jnp.exp(bl - bcs[i])
        pieces = [kd, jnp.broadcast_to(jnp.exp(bl), (SUBLANES, DK_C))]
        fill = DK_C - chunk - SUBLANES
        if fill > 0:
            pieces.append(jnp.zeros((fill, DK_C), F32))
        xt = jnp.transpose(jnp.concatenate(pieces, axis=0))
        return (xt[:, chunk:chunk + 1] * s_h[i]
                + jnp.dot(xt[:, 0:chunk].astype(BF16), iv[i], preferred_element_type=F32))

    if nblk > 1:
        att_on_matrix_unit()
        worst = jnp.float32(0.0)
        for b in range(nb):
            win = bs[b] - jnp.where(rowc >= SUBLANES, pltpu.roll(bs[b], SUBLANES, 0), 0.0)
            worst = jnp.minimum(worst, jnp.min(win))
        pl.when(worst < -HGRN_SAFE_LOG)(att_pairwise_diagonal)
    else:
        att_pairwise_diagonal()

    o_h = each(lambda i: jnp.dot(att_scr[i].astype(BF16), iv[i], preferred_element_type=F32)
               + inter[i])
    s_new = each(new_state)
    for i, (b, h) in enumerate(chains):
        s_scr[b, h] = s_new[i]
    for b in range(nb):
        o = jnp.concatenate([o_h[b * H_C + h] for h in range(H_C)], axis=1)
        o_ref[b] = _rms(o, gn_ref[...]) * jax.nn.silu(g_ref[b])
    sout_ref[...] = s_scr[...]


def _hgrn(pc3, s0, lbc, gn, *, layer, chunk, valid, nb):
    nseq, t, _ = pc3.shape
    assert nseq % nb == 0
    col = lambda j: pl.BlockSpec((nb, chunk, F_C), lambda b, c, j=j: (b, c, j))
    st = pl.BlockSpec((nb, H_C, DK_C, DK_C), lambda b, c: (b, 0, 0, 0))
    kern = functools.partial(_hgrn_kernel, layer=layer, chunk=chunk, valid=valid, nb=nb)
    return pl.pallas_call(
        kern,
        grid=(nseq // nb, t // chunk),
        in_specs=[col(0), col(1), col(2), col(3), st,
                  pl.BlockSpec(lbc.shape, lambda b, c: (0, 0)),
                  pl.BlockSpec((1, D_MODEL), lambda b, c: (0, 0))],
        out_specs=[pl.BlockSpec((nb, chunk, D_MODEL), lambda b, c: (b, c, 0)), st],
        out_shape=[jax.ShapeDtypeStruct((nseq, t, D_MODEL), F32),
                   jax.ShapeDtypeStruct(s0.shape, F32)],
        scratch_shapes=[pltpu.VMEM((nb, H_C, DK_C, DK_C), F32),
                        pltpu.VMEM((nb * H_C, chunk, chunk), F32)],
        compiler_params=_cparams(("parallel", "arbitrary")),
        name="hgrn",
    )(pc3, pc3, pc3, pc3, s0, lbc, gn.reshape(1, D_MODEL))


def _block_diag(w):
    h, n, _ = w.shape
    eye = jnp.eye(h, dtype=w.dtype)
    return (eye[:, None, :, None] * w[:, :, None, :]).reshape(h * n, h * n)


def _fold(x):
    half = x.shape[-2] // 2
    return jnp.concatenate([x[..., :half, :], x[..., half:, :]], axis=-1)


def _unfold(x):
    half = x.shape[-1] // 2
    return jnp.concatenate([x[..., :half], x[..., half:]], axis=-2)


def _trunk(x2d, conv_st, h_st, shift_st, rs_st, hs_st, P, *, nbatch, nsteps, time_major):
    n = x2d.shape[0]
    if time_major:
        nseq, step, rows = 1, nbatch, n
        tm = n
    else:
        nseq, step, rows = nbatch, 1, 512
        tm = 512
    nchain = nbatch * H_B

    out_a, conv_new, h_new, r, km, v, lw, kk, bb, g, shift_new = _front(
        x2d, P['ln_mix'][0], P['w_in_ab'],
        conv_st, h_st, P['conv_w'], P['conv_b'], P['wr'], P['wi'], P['gr_b'], P['gi_b'],
        P['lru_lambda'],
        shift_st, P['mu_b'], P['w0_b'], P['a0_b'], P['w2p'], P['a2p'], P['g2_b'],
        P['kk_b'], P['ka_b'], nseq=nseq, step=step, rows=rows)

    if time_major:
        def to_lanes(z):
            z = z.reshape(nsteps, nbatch, H_B, HD_B).transpose(0, 3, 1, 2)
            return z.reshape(nsteps, HD_B, nchain)

        s0 = rs_st.transpose(2, 3, 0, 1).reshape(HD_B, HD_B, nchain)
        o, s_new = _rwkv_scan(to_lanes(r), to_lanes(km), to_lanes(v), to_lanes(lw), to_lanes(kk),
                              to_lanes(bb), s0, tb=nsteps)
        o = o.reshape(nsteps, HD_B, nbatch, H_B).transpose(0, 2, 3, 1).reshape(n, W_B)
        rs_new = s_new.reshape(HD_B, HD_B, nbatch, H_B).transpose(2, 3, 0, 1)
    else:
        z = jnp.zeros_like(rs_st[:, 0::2])
        s0 = jnp.concatenate([jnp.concatenate([rs_st[:, 0::2], z], axis=-1),
                              jnp.concatenate([z, rs_st[:, 1::2]], axis=-1)], axis=-2)
        seq3 = lambda y: y.reshape(nbatch, nsteps, W_B)
        o, s_new = _rwkv_chunk(seq3(r), seq3(km), seq3(v), seq3(lw), seq3(kk), seq3(bb), s0,
                               chunk=HD_B, nb=RWKV_SEQS_PER_STEP)
        o = o.reshape(n, W_B)
        rs_new = jnp.stack([s_new[:, :, :HD_B, :HD_B], s_new[:, :, HD_B:, HD_B:]],
                           axis=2).reshape(nbatch, H_B, HD_B, HD_B)

    vec = lambda p: p.reshape(1, W_B)
    x2 = _res_ffn(_mix_residual, [x2d, out_a, o, r, km, v, g],
                  [vec(P['lnx_w']), vec(P['lnx_b']), vec(P['rk_b']), P['w_out_a'], P['w_out_b']],
                  P['ln_ffn'][0], P['ffn_gate'][0], P['ffn_up'][0], P['ffn_down'][0],
                  P['ln_final'], tm=tm, final_norm=False)

    (pc,) = _norm_matmul(x2, P['ln_mix'][1], P['w_in_c'], (4 * F_C,), tm)
    if time_major:
        pc3 = pc.reshape(nsteps, nbatch, 4 * F_C).transpose(1, 0, 2)
        chunk = SUBLANES
        pc3 = jnp.pad(pc3, ((0, 0), (0, chunk - nsteps), (0, 0)))
    else:
        pc3 = pc.reshape(nbatch, nsteps, 4 * F_C)
        chunk = 64
    o3, hs_new = _hgrn(pc3, hs_st, P['lb_c'], P['gn_c'], layer=1, chunk=chunk,
                       valid=min(chunk, nsteps), nb=4)
    if time_major:
        o = o3[:, :nsteps].transpose(1, 0, 2).reshape(n, D_MODEL)
    else:
        o = o3.reshape(n, D_MODEL)
    y = _res_ffn(_proj_residual, [x2, o], [P['w_out_c']],
                 P['ln_ffn'][1], P['ffn_gate'][1], P['ffn_up'][1], P['ffn_down'][1],
                 P['ln_final'], tm=tm, final_norm=True)
    return y, conv_new, h_new, shift_new, rs_new, hs_new


def kernel(x_prompt, x_sample, state_rglru_conv, state_rglru_h, state_rwkv_shift, state_rwkv_S,
           state_hgrn_S, ln_mix, ln_ffn, ln_final, w_in_ab, conv_w, conv_b, gr_w, gr_b, gi_w,
           gi_b, lru_lambda, mu_b, w0_b, w2_b, a0_b, a2_b, g2_b, kk_b, ka_b, rk_b, lnx_w, lnx_b,
           w_out_ab, w_in_c, lb_c, gn_c, w_out_c, ffn_gate, ffn_up, ffn_down):
    bp, tp, _ = x_prompt.shape
    bs, ts, _ = x_sample.shape
    zpad_w = jnp.zeros((LORA_A, W_B), F32)
    zpad_a = jnp.zeros((LORA_W, W_B), F32)
    P = dict(
        ln_mix=ln_mix, ln_ffn=ln_ffn, ln_final=ln_final,
        w_in_ab=w_in_ab[0].astype(BF16), conv_w=conv_w[0], conv_b=conv_b[0],
        wr=_block_diag(gr_w[0]).astype(BF16), wi=_block_diag(gi_w[0]).astype(BF16),
        gr_b=gr_b[0], gi_b=gi_b[0], lru_lambda=lru_lambda[0], mu_b=mu_b[0], w0_b=w0_b[0],
        a0_b=a0_b[0],
        w2p=jnp.concatenate([w2_b[0], zpad_w], axis=0).astype(BF16),
        a2p=jnp.concatenate([zpad_a, a2_b[0]], axis=0).astype(BF16),
        g2_b=g2_b[0].astype(BF16), kk_b=kk_b[0], ka_b=ka_b[0], rk_b=rk_b[0],
        lnx_w=lnx_w[0], lnx_b=lnx_b[0],
        w_out_a=w_out_ab[0, :W_A].astype(BF16), w_out_b=w_out_ab[0, W_A:].astype(BF16),
        w_in_c=w_in_c[0].astype(BF16), lb_c=lb_c, gn_c=gn_c[0],
        w_out_c=w_out_c[0].astype(BF16), ffn_gate=ffn_gate.astype(BF16),
        ffn_up=ffn_up.astype(BF16), ffn_down=ffn_down.astype(BF16))

    yp, p_conv, p_h, p_shift, p_rs, p_hs = _trunk(
        x_prompt.reshape(bp * tp, D_MODEL),
        jnp.zeros((bp, CONV_W - 1, W_A), F32), jnp.zeros((bp, 1, W_A), F32),
        jnp.zeros((bp, 1, P_B), F32), jnp.zeros((bp, H_B, HD_B, HD_B), F32),
        jnp.zeros((bp, H_C, DK_C, DK_C), F32), P, nbatch=bp, nsteps=tp, time_major=False)

    ys, s_conv, s_h, s_shift, s_rs, s_hs = _trunk(
        x_sample.transpose(1, 0, 2).reshape(ts * bs, D_MODEL),
        state_rglru_conv[0].transpose(1, 0, 2).reshape(1, (CONV_W - 1) * bs, W_A),
        state_rglru_h[0].reshape(1, bs, W_A), state_rwkv_shift[0].reshape(1, bs, P_B),
        state_rwkv_S[0], state_hgrn_S[0], P, nbatch=bs, nsteps=ts, time_major=True)

    return (yp.reshape(bp, tp, D_MODEL),
            ys.reshape(ts, bs, D_MODEL).transpose(1, 0, 2),
            p_conv[None], p_h.reshape(1, bp, W_A), p_shift.reshape(1, bp, P_B), p_rs[None],
            p_hs[None],
            s_conv.reshape(CONV_W - 1, bs, W_A).transpose(1, 0, 2)[None],
            s_h.reshape(1, bs, W_A), s_shift.reshape(1, bs, P_B), s_rs[None], s_hs[None])
```

```python
import functools

import jax
import jax.numpy as jnp
from jax import lax
from jax.experimental import pallas as pl
from jax.experimental.pallas import tpu as pltpu

F32 = jnp.float32
BF16 = jnp.bfloat16

D_MODEL = 1024
W_A = 512
H_A = 8
CONV_W = 4
LRU_C = 8.0
W_B = 512
HD_B = 64
H_B = 8
LORA_W = 64
LORA_A = 64
LORA_G = 128
P_A = 2 * W_A
P_B = 3 * W_B + LORA_W + LORA_A + LORA_G
DK_C = 128
H_C = 8
F_C = 1024
D_FF = 2816
RMS_EPS = 1e-6
GN_EPS = 64e-5

LANES = 128
SUBLANES = 8
VMEM_LIMIT = 56 * 1024 * 1024


def _cparams(sem):
    return pltpu.CompilerParams(dimension_semantics=sem, vmem_limit_bytes=VMEM_LIMIT)


def _softplus(x):
    return jnp.maximum(x, 0.0) + jnp.log1p(jnp.exp(-jnp.abs(x)))

def _sigmoid(x):
    return jax.nn.sigmoid(x)


def _rms(x, g):
    ms = jnp.mean(x * x, axis=-1, keepdims=True)
    return x * lax.rsqrt(ms + RMS_EPS) * g


def _dot(a, b):
    return jnp.dot(a.astype(BF16), b.astype(BF16), preferred_element_type=F32)


def _split(x):
    hi = x.astype(BF16)
    lo = (x - hi.astype(F32)).astype(BF16)
    return hi, lo


def _mm(a, b, nt=False):
    ah, al = a
    bh, bl = b
    lhs = jnp.concatenate([ah, al, ah], axis=1)
    if nt:
        rhs = jnp.concatenate([bh, bh, bl], axis=1)
        return lax.dot_general(lhs, rhs, (((1,), (1,)), ((), ())), preferred_element_type=F32)
    rhs = jnp.concatenate([bh, bh, bl], axis=0)
    return jnp.dot(lhs, rhs, preferred_element_type=F32)


def _head_ones(width):
    r = lax.broadcasted_iota(jnp.int32, (width, width), 0) // HD_B
    c = lax.broadcasted_iota(jnp.int32, (width, width), 1) // HD_B
    return jnp.where(r == c, 1.0, 0.0).astype(BF16)


def _head_sum(x, ones):
    hi, lo = _split(x)
    return jnp.dot(jnp.concatenate([hi, lo], axis=1), jnp.concatenate([ones, ones], axis=0),
                   preferred_element_type=F32)


def _norm_matmul_kernel(x_ref, g_ref, w_ref, o_ref, w_bf):
    @pl.when(pl.program_id(0) == 0)
    def _():
        w_bf[...] = w_ref[...].astype(BF16)

    xn = _rms(x_ref[...], g_ref[...]).astype(BF16)
    o_ref[...] = jnp.dot(xn, w_bf[...], preferred_element_type=F32)


def _norm_matmul(x, g, w, tm):
    n, d = x.shape
    p = w.shape[1]
    assert n % tm == 0
    return pl.pallas_call(
        _norm_matmul_kernel,
        grid=(n // tm,),
        in_specs=[pl.BlockSpec((tm, d), lambda i: (i, 0)),
                  pl.BlockSpec((1, d), lambda i: (0, 0)),
                  pl.BlockSpec((d, p), lambda i: (0, 0), pipeline_mode=pl.Buffered(1))],
        out_specs=pl.BlockSpec((tm, p), lambda i: (i, 0)),
        out_shape=jax.ShapeDtypeStruct((n, p), F32),
        scratch_shapes=[pltpu.VMEM((d, p), BF16)],
        compiler_params=_cparams(("arbitrary",)),
        name="norm_matmul",
    )(x, g.reshape(1, d), w)


def _rglru_body(u_in, gate, cw_ref, cb_ref, wr_ref, wi_ref, br_ref, bi_ref, lam_ref,
                out_ref, cnew_ref, hnew_ref, xbuf, hcar, bbuf, *, step, rows, pad):
    hist = (CONV_W - 1) * step

    if step == 1:
        u = u_in
        prev = xbuf[0:SUBLANES, :]
        sub8 = lax.broadcasted_iota(jnp.int32, (SUBLANES, W_A), 0)
        conv = cb_ref[...]
        for j in range(CONV_W):
            d = CONV_W - 1 - j
            if d == 0:
                ush = u
            else:
                rolled = pltpu.roll(u, d, 0)
                head = jnp.where(sub8 < d, pltpu.roll(prev, d, 0), rolled[0:SUBLANES, :])
                ush = jnp.concatenate([head, rolled[SUBLANES:, :]], axis=0)
            conv = conv + ush * cw_ref[j:j + 1, :]
        cnew_ref[0] = u[rows - hist:rows, :]
        xbuf[0:SUBLANES, :] = u[rows - SUBLANES:rows, :]
    else:
        xbuf[pad:pad + rows, :] = u_in
        conv = cb_ref[...] + xbuf[pad - hist:pad - hist + rows, :] * cw_ref[0:1, :]
        for j in range(1, CONV_W):
            o = pad - hist + j * step
            conv = conv + xbuf[o:o + rows, :] * cw_ref[j:j + 1, :]
        tail = xbuf[pad + rows - hist:pad + rows, :]
        cnew_ref[0] = tail
        xbuf[pad - hist:pad, :] = tail

    ub = conv.astype(BF16)
    r = _sigmoid(jnp.dot(ub, wr_ref[...], preferred_element_type=F32) + br_ref[...])
    ig = _sigmoid(jnp.dot(ub, wi_ref[...], preferred_element_type=F32) + bi_ref[...])
    log_a = (-LRU_C) * r * _softplus(-lam_ref[...])
    a = jnp.exp(log_a)
    bterm = jnp.sqrt(-jnp.tanh(log_a) * (a * a + 1.0)) * (ig * conv)
    bbuf[...] = bterm
    bbuf[0:step, :] = bterm[0:step, :] + a[0:step, :] * hcar[...]
    bv = bbuf[...]

    def scan_levels(a, bv, pos, first, count):
        d = first
        while d < first * count:
            m = pos >= d
            bv = jnp.where(m, a * pltpu.roll(bv, d, 0) + bv, bv)
            a = jnp.where(m, a * pltpu.roll(a, d, 0), a)
            d *= 2
        return a, bv

    row = lax.broadcasted_iota(jnp.int32, (rows, W_A), 0)
    a, bv = scan_levels(a, bv, row, step, rows // step)
    hcar[...] = bv[rows - step:rows, :]
    hnew_ref[0] = bv[rows - step:rows, :]
    out_ref[...] = jax.nn.gelu(gate) * bv


def _rwkv_prep_body(pf, mu_ref, w0_ref, a0_ref, w2_ref, a2_ref, g2_ref, kkp_ref, kap_ref,
                    r_ref, km_ref, v_ref, lw_ref, kk_ref, bb_ref, g_ref, snew_ref, pbuf,
                    *, step, rows, pad):

    if step == 1:
        rolled = pltpu.roll(pf, 1, 0)
        sub8 = lax.broadcasted_iota(jnp.int32, (SUBLANES, P_B), 0)
        head = jnp.where(sub8 < 1, pltpu.roll(pbuf[0:SUBLANES, :], 1, 0), rolled[0:SUBLANES, :])
        shifted = jnp.concatenate([head, rolled[SUBLANES:, :]], axis=0)
        pbuf[0:SUBLANES, :] = pf[rows - SUBLANES:rows, :]
    else:
        pbuf[pad:pad + rows, :] = pf
        shifted = pbuf[pad - step:pad - step + rows, :]
        pbuf[pad - step:pad, :] = pf[rows - step:rows, :]
    snew_ref[0] = pf[rows - step:rows, :]

    m = pf + (shifted - pf) * mu_ref[...]
    o1 = 3 * W_B
    r_ref[...] = m[:, 0:W_B]
    kraw = m[:, W_B:2 * W_B]
    v_ref[...] = m[:, 2 * W_B:o1]
    xwa = m[:, o1:o1 + LORA_W + LORA_A]
    xg = m[:, o1 + LORA_W + LORA_A:]
    lane = lax.broadcasted_iota(jnp.int32, xwa.shape, 1)
    lhs = jnp.where(lane < LORA_W, jnp.tanh(xwa), xwa).astype(BF16)
    lw = jnp.dot(lhs, w2_ref[...], preferred_element_type=F32)
    la = jnp.dot(lhs, a2_ref[...], preferred_element_type=F32)
    w_log = -_softplus(-(w0_ref[...] + lw)) - 0.5
    lw_ref[...] = -jnp.exp(w_log)
    a = _sigmoid(a0_ref[...] + la)
    g_ref[...] = jnp.dot(_sigmoid(xg).astype(BF16), g2_ref[...], preferred_element_type=F32)
    kk = kraw * kkp_ref[...]
    kk = kk * lax.rsqrt(jnp.maximum(_head_sum(kk * kk, _head_ones(W_B)), 1e-24))
    kk_ref[...] = kk
    bb_ref[...] = kk * a
    km_ref[...] = kraw * (1.0 + (a - 1.0) * kap_ref[...])


N_RGLRU_IN = 9
N_PREP_IN = 9


def _front_kernel(*refs, step, rows, pad_a, pad_b, nsplit):
    x_ref, gn_ref, w_ref = refs[:3]
    rg_in = refs[3:3 + N_RGLRU_IN]
    pp_in = refs[3 + N_RGLRU_IN:3 + N_RGLRU_IN + N_PREP_IN]
    outs = refs[3 + N_RGLRU_IN + N_PREP_IN:]
    rg_out, pp_out = outs[:3], outs[3:11]
    xbuf, hcar, bbuf, pbuf, p_scr, w_bf = outs[11:]
    part = rows // nsplit

    @pl.when(pl.program_id(1) == 0)
    def _():
        hist = (CONV_W - 1) * step
        xbuf[pad_a - hist:pad_a, :] = rg_in[0][0]
        hcar[...] = rg_in[1][0]
        pbuf[pad_b - step:pad_b, :] = pp_in[0][0]
        w_bf[...] = w_ref[...].astype(BF16)

    xn = _rms(x_ref[...], gn_ref[...]).astype(BF16)
    for s in range(nsplit):
        p_scr[s] = jnp.dot(xn[s * part:(s + 1) * part, :], w_bf[...],
                           preferred_element_type=F32)
    for s in range(nsplit):
        sub = lambda ref, s=s: ref.at[s * part:(s + 1) * part, :]
        _rglru_body(p_scr[s, :, 0:W_A], p_scr[s, :, W_A:P_A], *rg_in[2:],
                    sub(rg_out[0]), rg_out[1], rg_out[2], xbuf, hcar, bbuf,
                    step=step, rows=part, pad=pad_a)
        _rwkv_prep_body(p_scr[s, :, P_A:], *pp_in[1:], *[sub(r) for r in pp_out[:7]],
                        pp_out[7], pbuf, step=step, rows=part, pad=pad_b)


def _front(x, gn, w, conv_st, h_st, cw, cb, wr, wi, br, bi, lam,
           prev, mu, w0, a0, w2p, a2p, g2, kkp, kap, *, nseq, step, rows):
    n, d = x.shape
    nt = n // (nseq * rows)
    hist = (CONV_W - 1) * step
    pad_a = max(SUBLANES, hist)
    pad_b = max(SUBLANES, step)
    row_map = lambda b, t: (b * nt + t, 0)
    seq_map = lambda b, t: (b, 0, 0)
    whole = lambda a: pl.BlockSpec(a.shape, lambda b, t: (0,) * a.ndim,
                                   pipeline_mode=pl.Buffered(1))
    va = lambda p: p.reshape(1, W_A)
    vb = lambda p: p.reshape(1, W_B)
    consts_a = [cw, va(cb), wr, wi, va(br), va(bi), va(lam)]
    consts_b = [mu.reshape(1, P_B), vb(w0), vb(a0), w2p, a2p, g2, vb(kkp), vb(kap)]
    nsplit = 2 if step == 1 else 1
    part = rows // nsplit
    kern = functools.partial(_front_kernel, step=step, rows=rows, pad_a=pad_a, pad_b=pad_b,
                             nsplit=nsplit)
    tile = lambda wd: pl.BlockSpec((rows, wd), row_map)
    return pl.pallas_call(
        kern,
        grid=(nseq, nt),
        in_specs=[tile(d), whole(gn.reshape(1, d)), whole(w),
                  pl.BlockSpec((1, hist, W_A), seq_map), pl.BlockSpec((1, step, W_A), seq_map)]
                 + [whole(a) for a in consts_a]
                 + [pl.BlockSpec((1, step, P_B), seq_map)] + [whole(a) for a in consts_b],
        out_specs=[tile(W_A), pl.BlockSpec((1, hist, W_A), seq_map),
                   pl.BlockSpec((1, step, W_A), seq_map)]
                  + [tile(W_B)] * 7 + [pl.BlockSpec((1, step, P_B), seq_map)],
        out_shape=[jax.ShapeDtypeStruct((n, W_A), F32),
                   jax.ShapeDtypeStruct((nseq, hist, W_A), F32),
                   jax.ShapeDtypeStruct((nseq, step, W_A), F32)]
                  + [jax.ShapeDtypeStruct((n, W_B), F32)] * 7
                  + [jax.ShapeDtypeStruct((nseq, step, P_B), F32)],
        scratch_shapes=[pltpu.VMEM((pad_a + part, W_A), F32),
                        pltpu.VMEM((step, W_A), F32),
                        pltpu.VMEM((part, W_A), F32),
                        pltpu.VMEM((pad_b + part, P_B), F32),
                        pltpu.VMEM((nsplit, part, P_A + P_B), F32),
                        pltpu.VMEM(w.shape, BF16)],
        compiler_params=_cparams(("parallel", "arbitrary")),
        name="front",
    )(x, gn.reshape(1, d), w, conv_st, h_st, *consts_a, prev, *consts_b)


VGROUP = 4


def _rwkv_scan_kernel(r_ref, km_ref, v_ref, lw_ref, kk_ref, bb_ref, s0_ref, o_ref, sout_ref,
                      s_scr, *, tb):
    tblk = pl.program_id(1)

    @pl.when(tblk == 0)
    def _():
        s_scr[...] = s0_ref[...]

    def colsum(x):
        return jnp.sum(x, axis=0, keepdims=True)

    sub = lax.broadcasted_iota(jnp.int32, (VGROUP, LANES), 0)
    for g in range(HD_B // VGROUP):
        vbase = g * VGROUP

        def step(t, state, vbase=vbase):
            kk = kk_ref[t]
            w = jnp.exp(lw_ref[t])
            bvec = bb_ref[t]
            km = km_ref[t]
            r = r_ref[t]
            vt = v_ref[t, vbase:vbase + VGROUP, :]
            new = []
            otile = jnp.zeros((VGROUP, LANES), F32)
            for i in range(VGROUP):
                sk = colsum(state[i] * kk)
                sv = state[i] * w - sk * bvec + vt[i:i + 1, :] * km
                new.append(sv)
                otile = jnp.where(sub == i, colsum(sv * r), otile)
            o_ref[t, vbase:vbase + VGROUP, :] = otile
            return tuple(new)

        state = tuple(s_scr[vbase + i] for i in range(VGROUP))
        state = lax.fori_loop(0, tb, step, state, unroll=2)
        for i in range(VGROUP):
            s_scr[vbase + i] = state[i]

    sout_ref[...] = s_scr[...]


def _rwkv_scan(r, km, v, lw, kk, bb, s0, *, tb):
    nt, _, nl = r.shape
    seq = pl.BlockSpec((tb, HD_B, LANES), lambda g, t: (t, 0, g))
    st = pl.BlockSpec((HD_B, HD_B, LANES), lambda g, t: (0, 0, g))
    kern = functools.partial(_rwkv_scan_kernel, tb=tb)
    return pl.pallas_call(
        kern,
        grid=(nl // LANES, nt // tb),
        in_specs=[seq] * 6 + [st],
        out_specs=[seq, st],
        out_shape=[jax.ShapeDtypeStruct((nt, HD_B, nl), F32),
                   jax.ShapeDtypeStruct((HD_B, HD_B, nl), F32)],
        scratch_shapes=[pltpu.VMEM((HD_B, HD_B, LANES), F32)],
        compiler_params=_cparams(("parallel", "arbitrary")),
        name="rwkv_scan",
    )(r, km, v, lw, kk, bb, s0)


PAIR = 2 * HD_B
RWKV_SEQS_PER_STEP = 8


def _rwkv_chunk_kernel(r_ref, km_ref, v_ref, lw_ref, kk_ref, bb_ref, s0_ref, o_ref, sout_ref,
                       s_scr, *, chunk, nb):
    c = pl.program_id(1)

    @pl.when(c == 0)
    def _():
        s_scr[...] = s0_ref[...]

    lane = lax.broadcasted_iota(jnp.int32, (chunk, PAIR), 1)
    row = lax.broadcasted_iota(jnp.int32, (chunk, PAIR), 0)
    low_half = lane < HD_B
    li = jnp.where(low_half, lane, lane - HD_B)
    strict = li < row
    incl = li <= row
    eye = jnp.where(li == row, 1.0, 0.0)
    same = lambda n: (li // n) == (row // n)
    m8 = same(8)
    levels = []
    n = 16
    prev = m8
    while n <= chunk:
        cur = same(n)
        levels.append(jnp.logical_and(cur, jnp.logical_not(prev)))
        prev = cur
        n *= 2
    sq_r = lax.broadcasted_iota(jnp.int32, (PAIR, PAIR), 0)
    sq_c = lax.broadcasted_iota(jnp.int32, (PAIR, PAIR), 1)
    same_head = (sq_r < HD_B) == (sq_c < HD_B)
    low_bf = lax.broadcasted_iota(jnp.int32, (chunk, PAIR), 1) < HD_B

    def bd(s):
        return tuple(jnp.concatenate([jnp.where(low_bf, x, jnp.zeros_like(x)),
                                      jnp.where(low_bf, jnp.zeros_like(x), x)], axis=0)
                     for x in s)

    def pm1(x, y):
        return jnp.dot(x.astype(BF16), bd((y.astype(BF16),))[0], preferred_element_type=F32)

    chains = [(b, p) for b in range(nb) for p in range(H_B // 2)]
    each = lambda f: [f(i) for i in range(len(chains))]

    vv, lhs2, kg, bg, kdbd, etot, s_p = [], [], [], [], [], [], []
    for b, p in chains:
        sl = slice(p * PAIR, (p + 1) * PAIR)
        r = r_ref[b, :, sl]
        km = km_ref[b, :, sl]
        lw = lw_ref[b, :, sl]
        kk = kk_ref[b, :, sl]
        bb = bb_ref[b, :, sl]
        cum = lw
        d = 1
        while d < chunk:
            cum = cum + jnp.where(row >= d, pltpu.roll(cum, d, 0), 0.0)
            d *= 2
        tot = cum[chunk - 1:chunk, :]
        g_end = jnp.exp(tot - cum)
        g_inv = jnp.exp(-cum)
        vv.append(v_ref[b, :, sl])
        lhs2.append(_split(jnp.concatenate([kk * jnp.exp(cum - lw), r * jnp.exp(cum)], axis=0)))
        kg.append(bd(_split(km * g_inv)))
        bg.append(bd(_split(bb * g_inv)))
        kdbd.append(_split(jnp.concatenate([km * g_end, bb * g_end], axis=0)))
        etot.append(jnp.exp(tot))
        s_p.append(s_scr[b, p])

    ss = each(lambda i: lax.dot_general(lhs2[i][0], s_p[i].astype(BF16), (((1,), (1,)), ((), ())),
                                        preferred_element_type=F32))
    kb = each(lambda i: _mm(lhs2[i], tuple(jnp.concatenate([x, y], axis=0)
                                           for x, y in zip(kg[i], bg[i])), nt=True))
    a_b = each(lambda i: jnp.where(strict, kb[i][:chunk, PAIR:], 0.0))
    b_b = each(lambda i: jnp.where(incl, kb[i][chunk:, PAIR:], 0.0))
    av = each(lambda i: pm1(jnp.concatenate(
        [jnp.where(strict, kb[i][:chunk, :PAIR], 0.0), jnp.where(incl, kb[i][chunk:, :PAIR], 0.0)],
        axis=0), vv[i]))

    dg = each(lambda i: jnp.where(m8, a_b[i], 0.0))
    d2 = each(lambda i: pm1(dg[i], dg[i]))
    d3 = each(lambda i: pm1(dg[i], d2[i]))
    d4 = each(lambda i: pm1(d2[i], d2[i]))
    t1 = each(lambda i: eye - dg[i] + d2[i] - d3[i])
    inv = each(lambda i: t1[i] + pm1(t1[i], d4[i]))
    for m in levels:
        mlev = each(lambda i: pm1(inv[i], jnp.where(m, a_b[i], 0.0)))
        inv = each(lambda i: inv[i] - pm1(mlev[i], inv[i]))

    u = each(lambda i: pm1(inv[i], ss[i][:chunk] + av[i][:chunk]))
    upd = each(lambda i: _mm(_split(jnp.transpose(jnp.concatenate([vv[i], -u[i]], axis=0))),
                             kdbd[i]))
    bu = each(lambda i: pm1(b_b[i], u[i]))
    for i, (b, p) in enumerate(chains):
        sl = slice(p * PAIR, (p + 1) * PAIR)
        o_ref[b, :, sl] = ss[i][chunk:] + av[i][chunk:] - bu[i]
        s_scr[b, p] = s_p[i] * etot[i] + jnp.where(same_head, upd[i], 0.0)

    sout_ref[...] = s_scr[...]


def _rwkv_chunk(r, km, v, lw, kk, bb, s0, *, chunk, nb):
    nseq, t, _ = r.shape
    assert chunk == HD_B and nseq % nb == 0
    npair = H_B // 2
    row = pl.BlockSpec((nb, chunk, W_B), lambda b, c: (b, c, 0))
    st = pl.BlockSpec((nb, npair, PAIR, PAIR), lambda b, c: (b, 0, 0, 0))
    kern = functools.partial(_rwkv_chunk_kernel, chunk=chunk, nb=nb)
    return pl.pallas_call(
        kern,
        grid=(nseq // nb, t // chunk),
        in_specs=[row] * 6 + [st],
        out_specs=[row, st],
        out_shape=[jax.ShapeDtypeStruct((nseq, t, W_B), F32),
                   jax.ShapeDtypeStruct((nseq, npair, PAIR, PAIR), F32)],
        scratch_shapes=[pltpu.VMEM((nb, npair, PAIR, PAIR), F32)],
        compiler_params=_cparams(("parallel", "arbitrary")),
        name="rwkv_chunk",
    )(r, km, v, lw, kk, bb, s0)


MXU_N = 256


def _mix_residual(x_ref, oa_ref, o_ref, r_ref, km_ref, v_ref, g_ref, lnw_ref, lnb_ref, rk_ref,
                  wa_ref, wb_ref):
    ones = _head_ones(W_B)
    o = o_ref[...]
    cen = o - _head_sum(o, ones) * (1.0 / HD_B)
    head_sum1 = lambda z: jnp.dot(z.astype(BF16), ones, preferred_element_type=F32)
    var = head_sum1(cen * cen) * (1.0 / HD_B)
    on = cen * lax.rsqrt(var + GN_EPS) * lnw_ref[...] + lnb_ref[...]
    bonus = head_sum1(r_ref[...] * km_ref[...] * rk_ref[...]) * v_ref[...]
    ob = (on + bonus) * g_ref[...]
    return (x_ref[...]
            + jnp.dot(oa_ref[...].astype(BF16), wa_ref[...], preferred_element_type=F32)
            + jnp.dot(ob.astype(BF16), wb_ref[...], preferred_element_type=F32))


def _proj_residual(x_ref, o_ref, w_ref):
    return x_ref[...] + jnp.dot(o_ref[...].astype(BF16), w_ref[...],
                                preferred_element_type=F32)


def _res_ffn_kernel(*refs, residual, n_res, final_norm):
    g_ref, wg_ref, wu_ref, wd_ref, gf_ref, y_ref, xf_scr, x_scr, h_scr = refs[n_res:]
    x = residual(*refs[:n_res])
    xf_scr[...] = _rms(x, g_ref[...]).astype(BF16)
    x_scr[...] = x
    for j in range(wg_ref.shape[1] // MXU_N):
        cols = slice(j * MXU_N, (j + 1) * MXU_N)
        xf = xf_scr[...]
        hg = jnp.dot(xf, wg_ref[:, cols], preferred_element_type=F32)
        hu = jnp.dot(xf, wu_ref[:, cols], preferred_element_type=F32)
        h_scr[:, cols] = ((hg * _sigmoid(hg)) * hu).astype(BF16)
    y = x_scr[...] + jnp.dot(h_scr[...], wd_ref[...], preferred_element_type=F32)
    if final_norm:
        y = _rms(y, gf_ref[...])
    y_ref[...] = y


def _res_ffn(residual, row_ins, const_ins, g, wg, wu, wd, gf, *, tm, final_norm):
    n, d = row_ins[0].shape
    whole = lambda a: pl.BlockSpec(a.shape, lambda i: (0,) * a.ndim,
                                   pipeline_mode=pl.Buffered(1))
    consts = list(const_ins) + [g.reshape(1, d), wg, wu, wd, gf.reshape(1, d)]
    kern = functools.partial(_res_ffn_kernel, residual=residual,
                             n_res=len(row_ins) + len(const_ins), final_norm=final_norm)
    return pl.pallas_call(
        kern,
        grid=(n // tm,),
        in_specs=[pl.BlockSpec((tm, a.shape[1]), lambda i: (i, 0)) for a in row_ins]
                 + [whole(a) for a in consts],
        out_specs=pl.BlockSpec((tm, d), lambda i: (i, 0)),
        out_shape=jax.ShapeDtypeStruct((n, d), F32),
        scratch_shapes=[pltpu.VMEM((tm, d), BF16), pltpu.VMEM((tm, d), F32),
                        pltpu.VMEM((tm, wg.shape[1]), BF16)],
        compiler_params=_cparams(("parallel",)),
        name="res_ffn",
    )(*row_ins, *consts)


HGRN_SAFE_LOG = 80.0


def _hgrn_kernel(q_ref, f_ref, i_ref, g_ref, s0_ref, lbc_ref, gn_ref, o_ref, sout_ref,
                 s_scr, att_scr, *, layer, chunk, valid, nb):
    c = pl.program_id(1)

    @pl.when(c == 0)
    def _():
        s_scr[...] = s0_ref[...]

    lbc = lbc_ref[...]
    e = jnp.exp(lbc - jnp.max(lbc, axis=0, keepdims=True))
    sm = e / jnp.sum(e, axis=0, keepdims=True)
    lbf = sm[1:2, :]
    for l in range(2, layer + 1):
        lbf = lbf + sm[l:l + 1, :]

    nblk = chunk // SUBLANES
    sub = lax.broadcasted_iota(jnp.int32, (SUBLANES, DK_C), 0)
    lane_c = lax.broadcasted_iota(jnp.int32, (SUBLANES, chunk), 1)
    rowc = lax.broadcasted_iota(jnp.int32, (chunk, F_C), 0)
    tril = jnp.where(lax.broadcasted_iota(jnp.int32, (chunk, chunk), 0)
                     >= lax.broadcasted_iota(jnp.int32, (chunk, chunk), 1), 1.0, 0.0).astype(BF16)

    qs, ks, bs, ivs = [], [], [], []
    for b in range(nb):
        fr = f_ref[b]
        ez = jnp.exp(-jnp.abs(fr))
        rz = 1.0 / (1.0 + ez)
        sig = jnp.where(fr >= 0.0, rz, ez * rz)
        nsig = jnp.where(fr >= 0.0, ez * rz, rz)
        logf = jnp.log(lbf + (1.0 - lbf) * sig)
        kfull = (1.0 - lbf) * nsig
        if valid < chunk:
            logf = jnp.where(rowc < valid, logf, 0.0)
            kfull = jnp.where(rowc < valid, kfull, 0.0)
        hi, lo = _split(logf)
        bc = jnp.dot(jnp.concatenate([tril, tril], axis=1), jnp.concatenate([hi, lo], axis=0),
                     preferred_element_type=F32)
        qs.append(jax.nn.silu(q_ref[b]))
        ks.append(kfull)
        bs.append(bc)
        ivs.append(i_ref[b].astype(BF16))

    chains = [(b, h) for b in range(nb) for h in range(H_C)]
    each = lambda f: [f(i) for i in range(len(chains))]
    hsl = lambda h: slice(h * DK_C, (h + 1) * DK_C)
    q = each(lambda i: qs[chains[i][0]][:, hsl(chains[i][1])])
    k = each(lambda i: ks[chains[i][0]][:, hsl(chains[i][1])])
    bcs = each(lambda i: bs[chains[i][0]][:, hsl(chains[i][1])])
    iv = each(lambda i: ivs[chains[i][0]][:, hsl(chains[i][1])])
    s_h = each(lambda i: s_scr[chains[i][0], chains[i][1]])

    def off_diag(i, tb):
        lo = tb * SUBLANES
        rs = slice(lo, lo + SUBLANES)
        edge = bcs[i][lo - 1:lo, :]
        qi = q[i][rs, :] * jnp.exp(bcs[i][rs, :] - edge)
        ki = k[i][0:lo, :] * jnp.exp(edge - bcs[i][0:lo, :])
        ki = jnp.concatenate([ki, jnp.zeros((chunk - lo, DK_C), F32)], axis=0)
        return lax.dot_general(qi.astype(BF16), ki.astype(BF16), (((1,), (1,)), ((), ())),
                               preferred_element_type=F32)

    inter = each(lambda i: jnp.dot((q[i] * jnp.exp(bcs[i])).astype(BF16), s_h[i].astype(BF16),
                                   preferred_element_type=F32))


    sub_c = lax.broadcasted_iota(jnp.int32, (SUBLANES, chunk), 0)

    def att_on_matrix_unit():
        def block(i, tb):
            lo = tb * SUBLANES
            rs = slice(lo, lo + SUBLANES)
            dloc = bcs[i][rs, :] if tb == 0 else bcs[i][rs, :] - bcs[i][lo - 1:lo, :]
            qi = q[i][rs, :] * jnp.exp(dloc)
            parts = [k[i][rs, :] * jnp.exp(-dloc)]
            if tb > 0:
                parts.insert(0, k[i][0:lo, :] * jnp.exp(bcs[i][lo - 1:lo, :] - bcs[i][0:lo, :]))
            if chunk - lo - SUBLANES > 0:
                parts.append(jnp.zeros((chunk - lo - SUBLANES, DK_C), F32))
            ki = jnp.concatenate(parts, axis=0) if len(parts) > 1 else parts[0]
            return lax.dot_general(qi.astype(BF16), ki.astype(BF16), (((1,), (1,)), ((), ())),
                                   preferred_element_type=F32)

        blks = each(lambda i: [block(i, tb) for tb in range(nblk)])
        for i in range(len(chains)):
            for tb in range(nblk):
                att_scr[i, tb * SUBLANES:(tb + 1) * SUBLANES, :] = jnp.where(
                    lane_c <= sub_c + tb * SUBLANES, blks[i][tb], 0.0)

    def att_pairwise_diagonal():
        offd = each(lambda i: [off_diag(i, tb) for tb in range(1, nblk)])
        for i in range(len(chains)):
            for tb in range(nblk):
                rs = slice(tb * SUBLANES, (tb + 1) * SUBLANES)
                blk = jnp.zeros((SUBLANES, chunk), F32) if tb == 0 else offd[i][tb - 1]
                for j in range(SUBLANES):
                    s = tb * SUBLANES + j
                    if s >= valid:
                        break
                    diff = jnp.where(sub >= j, bcs[i][rs, :] - bcs[i][s:s + 1, :], -jnp.inf)
                    ev = jnp.exp(diff) * (q[i][rs, :] * k[i][s:s + 1, :])
                    blk = jnp.where(lane_c == s, jnp.sum(ev, axis=1, keepdims=True), blk)
                att_scr[i, rs, :] = blk

    def new_state(i):
        bl = bcs[i][chunk - 1:chunk, :]
        kd = k[i] * jnp.exp(bl - bcs[i])
        pieces = [kd, jnp.broadcast_to(jnp.exp(bl), (SUBLANES, DK_C))]
        fill = DK_C - chunk - SUBLANES
        if fill > 0:
            pieces.append(jnp.zeros((fill, DK_C), F32))
        xt = jnp.transpose(jnp.concatenate(pieces, axis=0))
        return (xt[:, chunk:chunk + 1] * s_h[i]
                + jnp.dot(xt[:, 0:chunk].astype(BF16), iv[i], preferred_element_type=F32))

    if nblk > 1:
        att_on_matrix_unit()
        worst = jnp.float32(0.0)
        for b in range(nb):
            win = bs[b] - jnp.where(rowc >= SUBLANES, pltpu.roll(bs[b], SUBLANES, 0), 0.0)
            worst = jnp.minimum(worst, jnp.min(win))
        pl.when(worst < -HGRN_SAFE_LOG)(att_pairwise_diagonal)
    else:
        att_pairwise_diagonal()

    o_h = each(lambda i: jnp.dot(att_scr[i].astype(BF16), iv[i], preferred_element_type=F32)
               + inter[i])
    s_new = each(new_state)
    for i, (b, h) in enumerate(chains):
        s_scr[b, h] = s_new[i]
    for b in range(nb):
        o = jnp.concatenate([o_h[b * H_C + h] for h in range(H_C)], axis=1)
        o_ref[b] = _rms(o, gn_ref[...]) * jax.nn.silu(g_ref[b])
    sout_ref[...] = s_scr[...]


def _hgrn(pc3, s0, lbc, gn, *, layer, chunk, valid, nb):
    nseq, t, _ = pc3.shape
    assert nseq % nb == 0
    col = lambda j: pl.BlockSpec((nb, chunk, F_C), lambda b, c, j=j: (b, c, j))
    st = pl.BlockSpec((nb, H_C, DK_C, DK_C), lambda b, c: (b, 0, 0, 0))
    kern = functools.partial(_hgrn_kernel, layer=layer, chunk=chunk, valid=valid, nb=nb)
    return pl.pallas_call(
        kern,
        grid=(nseq // nb, t // chunk),
        in_specs=[col(0), col(1), col(2), col(3), st,
                  pl.BlockSpec(lbc.shape, lambda b, c: (0, 0)),
                  pl.BlockSpec((1, D_MODEL), lambda b, c: (0, 0))],
        out_specs=[pl.BlockSpec((nb, chunk, D_MODEL), lambda b, c: (b, c, 0)), st],
        out_shape=[jax.ShapeDtypeStruct((nseq, t, D_MODEL), F32),
                   jax.ShapeDtypeStruct(s0.shape, F32)],
        scratch_shapes=[pltpu.VMEM((nb, H_C, DK_C, DK_C), F32),
                        pltpu.VMEM((nb * H_C, chunk, chunk), F32)],
        compiler_params=_cparams(("parallel", "arbitrary")),
        name="hgrn",
    )(pc3, pc3, pc3, pc3, s0, lbc, gn.reshape(1, D_MODEL))


def _block_diag(w):
    h, n, _ = w.shape
    eye = jnp.eye(h, dtype=w.dtype)
    return (eye[:, None, :, None] * w[:, :, None, :]).reshape(h * n, h * n)


def _fold(x):
    half = x.shape[-2] // 2
    return jnp.concatenate([x[..., :half, :], x[..., half:, :]], axis=-1)


def _unfold(x):
    half = x.shape[-1] // 2
    return jnp.concatenate([x[..., :half], x[..., half:]], axis=-2)


def _trunk(x2d, conv_st, h_st, shift_st, rs_st, hs_st, P, *, nbatch, nsteps, time_major):
    n = x2d.shape[0]
    if time_major:
        nseq, step, rows = 1, nbatch, n
        tm = n
    else:
        nseq, step, rows = nbatch, 1, 512
        tm = 512
    nchain = nbatch * H_B

    out_a, conv_new, h_new, r, km, v, lw, kk, bb, g, shift_new = _front(
        x2d, P['ln_mix'][0], P['w_in_ab'],
        conv_st, h_st, P['conv_w'], P['conv_b'], P['wr'], P['wi'], P['gr_b'], P['gi_b'],
        P['lru_lambda'],
        shift_st, P['mu_b'], P['w0_b'], P['a0_b'], P['w2p'], P['a2p'], P['g2_b'],
        P['kk_b'], P['ka_b'], nseq=nseq, step=step, rows=rows)

    if time_major:
        def to_lanes(z):
            z = z.reshape(nsteps, nbatch, H_B, HD_B).transpose(0, 3, 1, 2)
            return z.reshape(nsteps, HD_B, nchain)

        s0 = rs_st.transpose(2, 3, 0, 1).reshape(HD_B, HD_B, nchain)
        o, s_new = _rwkv_scan(to_lanes(r), to_lanes(km), to_lanes(v), to_lanes(lw), to_lanes(kk),
                              to_lanes(bb), s0, tb=nsteps)
        o = o.reshape(nsteps, HD_B, nbatch, H_B).transpose(0, 2, 3, 1).reshape(n, W_B)
        rs_new = s_new.reshape(HD_B, HD_B, nbatch, H_B).transpose(2, 3, 0, 1)
    else:
        z = jnp.zeros_like(rs_st[:, 0::2])
        s0 = jnp.concatenate([jnp.concatenate([rs_st[:, 0::2], z], axis=-1),
                              jnp.concatenate([z, rs_st[:, 1::2]], axis=-1)], axis=-2)
        seq3 = lambda y: y.reshape(nbatch, nsteps, W_B)
        o, s_new = _rwkv_chunk(seq3(r), seq3(km), seq3(v), seq3(lw), seq3(kk), seq3(bb), s0,
                               chunk=HD_B, nb=RWKV_SEQS_PER_STEP)
        o = o.reshape(n, W_B)
        rs_new = jnp.stack([s_new[:, :, :HD_B, :HD_B], s_new[:, :, HD_B:, HD_B:]],
                           axis=2).reshape(nbatch, H_B, HD_B, HD_B)

    vec = lambda p: p.reshape(1, W_B)
    x2 = _res_ffn(_mix_residual, [x2d, out_a, o, r, km, v, g],
                  [vec(P['lnx_w']), vec(P['lnx_b']), vec(P['rk_b']), P['w_out_a'], P['w_out_b']],
                  P['ln_ffn'][0], P['ffn_gate'][0], P['ffn_up'][0], P['ffn_down'][0],
                  P['ln_final'], tm=tm, final_norm=False)

    pc = _norm_matmul(x2, P['ln_mix'][1], P['w_in_c'], tm)
    if time_major:
        pc3 = pc.reshape(nsteps, nbatch, 4 * F_C).transpose(1, 0, 2)
        chunk = SUBLANES
        pc3 = jnp.pad(pc3, ((0, 0), (0, chunk - nsteps), (0, 0)))
    else:
        pc3 = pc.reshape(nbatch, nsteps, 4 * F_C)
        chunk = 64
    o3, hs_new = _hgrn(pc3, hs_st, P['lb_c'], P['gn_c'], layer=1, chunk=chunk,
                       valid=min(chunk, nsteps), nb=4)
    if time_major:
        o = o3[:, :nsteps].transpose(1, 0, 2).reshape(n, D_MODEL)
    else:
        o = o3.reshape(n, D_MODEL)
    y = _res_ffn(_proj_residual, [x2, o], [P['w_out_c']],
                 P['ln_ffn'][1], P['ffn_gate'][1], P['ffn_up'][1], P['ffn_down'][1],
                 P['ln_final'], tm=tm, final_norm=True)
    return y, conv_new, h_new, shift_new, rs_new, hs_new


def kernel(x_prompt, x_sample, state_rglru_conv, state_rglru_h, state_rwkv_shift, state_rwkv_S,
           state_hgrn_S, ln_mix, ln_ffn, ln_final, w_in_ab, conv_w, conv_b, gr_w, gr_b, gi_w,
           gi_b, lru_lambda, mu_b, w0_b, w2_b, a0_b, a2_b, g2_b, kk_b, ka_b, rk_b, lnx_w, lnx_b,
           w_out_ab, w_in_c, lb_c, gn_c, w_out_c, ffn_gate, ffn_up, ffn_down):
    bp, tp, _ = x_prompt.shape
    bs, ts, _ = x_sample.shape
    zpad_w = jnp.zeros((LORA_A, W_B), F32)
    zpad_a = jnp.zeros((LORA_W, W_B), F32)
    P = dict(
        ln_mix=ln_mix, ln_ffn=ln_ffn, ln_final=ln_final,
        w_in_ab=w_in_ab[0], conv_w=conv_w[0], conv_b=conv_b[0],
        wr=_block_diag(gr_w[0]).astype(BF16), wi=_block_diag(gi_w[0]).astype(BF16),
        gr_b=gr_b[0], gi_b=gi_b[0], lru_lambda=lru_lambda[0], mu_b=mu_b[0], w0_b=w0_b[0],
        a0_b=a0_b[0],
        w2p=jnp.concatenate([w2_b[0], zpad_w], axis=0).astype(BF16),
        a2p=jnp.concatenate([zpad_a, a2_b[0]], axis=0).astype(BF16),
        g2_b=g2_b[0].astype(BF16), kk_b=kk_b[0], ka_b=ka_b[0], rk_b=rk_b[0],
        lnx_w=lnx_w[0], lnx_b=lnx_b[0],
        w_out_a=w_out_ab[0, :W_A].astype(BF16), w_out_b=w_out_ab[0, W_A:].astype(BF16),
        w_in_c=w_in_c[0], lb_c=lb_c, gn_c=gn_c[0],
        w_out_c=w_out_c[0].astype(BF16), ffn_gate=ffn_gate.astype(BF16),
        ffn_up=ffn_up.astype(BF16), ffn_down=ffn_down.astype(BF16))

    yp, p_conv, p_h, p_shift, p_rs, p_hs = _trunk(
        x_prompt.reshape(bp * tp, D_MODEL),
        jnp.zeros((bp, CONV_W - 1, W_A), F32), jnp.zeros((bp, 1, W_A), F32),
        jnp.zeros((bp, 1, P_B), F32), jnp.zeros((bp, H_B, HD_B, HD_B), F32),
        jnp.zeros((bp, H_C, DK_C, DK_C), F32), P, nbatch=bp, nsteps=tp, time_major=False)

    ys, s_conv, s_h, s_shift, s_rs, s_hs = _trunk(
        x_sample.transpose(1, 0, 2).reshape(ts * bs, D_MODEL),
        state_rglru_conv[0].transpose(1, 0, 2).reshape(1, (CONV_W - 1) * bs, W_A),
        state_rglru_h[0].reshape(1, bs, W_A), state_rwkv_shift[0].reshape(1, bs, P_B),
        state_rwkv_S[0], state_hgrn_S[0], P, nbatch=bs, nsteps=ts, time_major=True)

    return (yp.reshape(bp, tp, D_MODEL),
            ys.reshape(ts, bs, D_MODEL).transpose(1, 0, 2),
            p_conv[None], p_h.reshape(1, bp, W_A), p_shift.reshape(1, bp, P_B), p_rs[None],
            p_hs[None],
            s_conv.reshape(CONV_W - 1, bs, W_A).transpose(1, 0, 2)[None],
            s_h.reshape(1, bs, W_A), s_shift.reshape(1, bs, P_B), s_rs[None], s_hs[None])
```

```python
import functools

import jax
import jax.numpy as jnp
from jax import lax
from jax.experimental import pallas as pl
from jax.experimental.pallas import tpu as pltpu

F32 = jnp.float32
BF16 = jnp.bfloat16

D_MODEL = 1024
W_A = 512
H_A = 8
CONV_W = 4
LRU_C = 8.0
W_B = 512
HD_B = 64
H_B = 8
LORA_W = 64
LORA_A = 64
LORA_G = 128
P_A = 2 * W_A
P_B = 3 * W_B + LORA_W + LORA_A + LORA_G
DK_C = 128
H_C = 8
F_C = 1024
D_FF = 2816
RMS_EPS = 1e-6
GN_EPS = 64e-5

LANES = 128
SUBLANES = 8
VMEM_LIMIT = 56 * 1024 * 1024


def _cparams(sem):
    return pltpu.CompilerParams(dimension_semantics=sem, vmem_limit_bytes=VMEM_LIMIT)


def _softplus(x):
    return jnp.maximum(x, 0.0) + jnp.log1p(jnp.exp(-jnp.abs(x)))

def _sigmoid(x):
    return jax.nn.sigmoid(x)


def _rms(x, g):
    ms = jnp.mean(x * x, axis=-1, keepdims=True)
    return x * lax.rsqrt(ms + RMS_EPS) * g


def _dot(a, b):
    return jnp.dot(a.astype(BF16), b.astype(BF16), preferred_element_type=F32)


def _split(x):
    hi = x.astype(BF16)
    lo = (x - hi.astype(F32)).astype(BF16)
    return hi, lo


def _mm(a, b, nt=False):
    ah, al = a
    bh, bl = b
    lhs = jnp.concatenate([ah, al, ah], axis=1)
    if nt:
        rhs = jnp.concatenate([bh, bh, bl], axis=1)
        return lax.dot_general(lhs, rhs, (((1,), (1,)), ((), ())), preferred_element_type=F32)
    rhs = jnp.concatenate([bh, bh, bl], axis=0)
    return jnp.dot(lhs, rhs, preferred_element_type=F32)


def _head_ones(width):
    r = lax.broadcasted_iota(jnp.int32, (width, width), 0) // HD_B
    c = lax.broadcasted_iota(jnp.int32, (width, width), 1) // HD_B
    return jnp.where(r == c, 1.0, 0.0).astype(BF16)


def _head_sum(x, ones):
    hi, lo = _split(x)
    return jnp.dot(jnp.concatenate([hi, lo], axis=1), jnp.concatenate([ones, ones], axis=0),
                   preferred_element_type=F32)


def _norm_matmul_kernel(x_ref, g_ref, w_ref, o_ref, w_bf):
    @pl.when(pl.program_id(0) == 0)
    def _():
        w_bf[...] = w_ref[...].astype(BF16)

    xn = _rms(x_ref[...], g_ref[...]).astype(BF16)
    o_ref[...] = jnp.dot(xn, w_bf[...], preferred_element_type=F32)


def _norm_matmul(x, g, w, tm):
    n, d = x.shape
    p = w.shape[1]
    assert n % tm == 0
    return pl.pallas_call(
        _norm_matmul_kernel,
        grid=(n // tm,),
        in_specs=[pl.BlockSpec((tm, d), lambda i: (i, 0)),
                  pl.BlockSpec((1, d), lambda i: (0, 0)),
                  pl.BlockSpec((d, p), lambda i: (0, 0), pipeline_mode=pl.Buffered(1))],
        out_specs=pl.BlockSpec((tm, p), lambda i: (i, 0)),
        out_shape=jax.ShapeDtypeStruct((n, p), F32),
        scratch_shapes=[pltpu.VMEM((d, p), BF16)],
        compiler_params=_cparams(("arbitrary",)),
        name="norm_matmul",
    )(x, g.reshape(1, d), w)


def _rglru_body(u_in, gate, cw_ref, cb_ref, wr_ref, wi_ref, br_ref, bi_ref, lam_ref,
                out_ref, cnew_ref, hnew_ref, xbuf, hcar, bbuf, *, step, rows, pad):
    hist = (CONV_W - 1) * step

    if step == 1:
        u = u_in
        prev = xbuf[0:SUBLANES, :]
        sub8 = lax.broadcasted_iota(jnp.int32, (SUBLANES, W_A), 0)
        conv = cb_ref[...]
        for j in range(CONV_W):
            d = CONV_W - 1 - j
            if d == 0:
                ush = u
            else:
                rolled = pltpu.roll(u, d, 0)
                head = jnp.where(sub8 < d, pltpu.roll(prev, d, 0), rolled[0:SUBLANES, :])
                ush = jnp.concatenate([head, rolled[SUBLANES:, :]], axis=0)
            conv = conv + ush * cw_ref[j:j + 1, :]
        cnew_ref[0] = u[rows - hist:rows, :]
        xbuf[0:SUBLANES, :] = u[rows - SUBLANES:rows, :]
    else:
        xbuf[pad:pad + rows, :] = u_in
        conv = cb_ref[...] + xbuf[pad - hist:pad - hist + rows, :] * cw_ref[0:1, :]
        for j in range(1, CONV_W):
            o = pad - hist + j * step
            conv = conv + xbuf[o:o + rows, :] * cw_ref[j:j + 1, :]
        tail = xbuf[pad + rows - hist:pad + rows, :]
        cnew_ref[0] = tail
        xbuf[pad - hist:pad, :] = tail

    ub = conv.astype(BF16)
    r = _sigmoid(jnp.dot(ub, wr_ref[...], preferred_element_type=F32) + br_ref[...])
    ig = _sigmoid(jnp.dot(ub, wi_ref[...], preferred_element_type=F32) + bi_ref[...])
    log_a = (-LRU_C) * r * _softplus(-lam_ref[...])
    a = jnp.exp(log_a)
    bterm = jnp.sqrt(-jnp.tanh(log_a) * (a * a + 1.0)) * (ig * conv)
    bbuf[...] = bterm
    bbuf[0:step, :] = bterm[0:step, :] + a[0:step, :] * hcar[...]
    bv = bbuf[...]

    def scan_levels(a, bv, pos, first, count):
        d = first
        while d < first * count:
            m = pos >= d
            bv = jnp.where(m, a * pltpu.roll(bv, d, 0) + bv, bv)
            a = jnp.where(m, a * pltpu.roll(a, d, 0), a)
            d *= 2
        return a, bv

    row = lax.broadcasted_iota(jnp.int32, (rows, W_A), 0)
    a, bv = scan_levels(a, bv, row, step, rows // step)
    hcar[...] = bv[rows - step:rows, :]
    hnew_ref[0] = bv[rows - step:rows, :]
    out_ref[...] = jax.nn.gelu(gate) * bv


def _rwkv_prep_body(pf, mu_ref, w0_ref, a0_ref, w2_ref, a2_ref, g2_ref, kkp_ref, kap_ref,
                    r_ref, km_ref, v_ref, lw_ref, kk_ref, bb_ref, g_ref, snew_ref, pbuf,
                    *, step, rows, pad):

    if step == 1:
        rolled = pltpu.roll(pf, 1, 0)
        sub8 = lax.broadcasted_iota(jnp.int32, (SUBLANES, P_B), 0)
        head = jnp.where(sub8 < 1, pltpu.roll(pbuf[0:SUBLANES, :], 1, 0), rolled[0:SUBLANES, :])
        shifted = jnp.concatenate([head, rolled[SUBLANES:, :]], axis=0)
        pbuf[0:SUBLANES, :] = pf[rows - SUBLANES:rows, :]
    else:
        pbuf[pad:pad + rows, :] = pf
        shifted = pbuf[pad - step:pad - step + rows, :]
        pbuf[pad - step:pad, :] = pf[rows - step:rows, :]
    snew_ref[0] = pf[rows - step:rows, :]

    m = pf + (shifted - pf) * mu_ref[...]
    o1 = 3 * W_B
    r_ref[...] = m[:, 0:W_B]
    kraw = m[:, W_B:2 * W_B]
    v_ref[...] = m[:, 2 * W_B:o1]
    xwa = m[:, o1:o1 + LORA_W + LORA_A]
    xg = m[:, o1 + LORA_W + LORA_A:]
    lane = lax.broadcasted_iota(jnp.int32, xwa.shape, 1)
    lhs = jnp.where(lane < LORA_W, jnp.tanh(xwa), xwa).astype(BF16)
    lw = jnp.dot(lhs, w2_ref[...], preferred_element_type=F32)
    la = jnp.dot(lhs, a2_ref[...], preferred_element_type=F32)
    w_log = -_softplus(-(w0_ref[...] + lw)) - 0.5
    lw_ref[...] = -jnp.exp(w_log)
    a = _sigmoid(a0_ref[...] + la)
    g_ref[...] = jnp.dot(_sigmoid(xg).astype(BF16), g2_ref[...], preferred_element_type=F32)
    kk = kraw * kkp_ref[...]
    kk = kk * lax.rsqrt(jnp.maximum(_head_sum(kk * kk, _head_ones(W_B)), 1e-24))
    kk_ref[...] = kk
    bb_ref[...] = kk * a
    km_ref[...] = kraw * (1.0 + (a - 1.0) * kap_ref[...])


N_RGLRU_IN = 9
N_PREP_IN = 9


def _front_kernel(*refs, step, rows, pad_a, pad_b, nsplit):
    x_ref, gn_ref, w_ref = refs[:3]
    rg_in = refs[3:3 + N_RGLRU_IN]
    pp_in = refs[3 + N_RGLRU_IN:3 + N_RGLRU_IN + N_PREP_IN]
    outs = refs[3 + N_RGLRU_IN + N_PREP_IN:]
    rg_out, pp_out = outs[:3], outs[3:11]
    xbuf, hcar, bbuf, pbuf, p_scr, w_bf = outs[11:]
    part = rows // nsplit

    @pl.when(pl.program_id(1) == 0)
    def _():
        hist = (CONV_W - 1) * step
        xbuf[pad_a - hist:pad_a, :] = rg_in[0][0]
        hcar[...] = rg_in[1][0]
        pbuf[pad_b - step:pad_b, :] = pp_in[0][0]
        w_bf[...] = w_ref[...].astype(BF16)

    xn = _rms(x_ref[...], gn_ref[...]).astype(BF16)
    for s in range(nsplit):
        p_scr[s] = jnp.dot(xn[s * part:(s + 1) * part, :], w_bf[...],
                           preferred_element_type=F32)
    for s in range(nsplit):
        sub = lambda ref, s=s: ref.at[s * part:(s + 1) * part, :]
        _rglru_body(p_scr[s, :, 0:W_A], p_scr[s, :, W_A:P_A], *rg_in[2:],
                    sub(rg_out[0]), rg_out[1], rg_out[2], xbuf, hcar, bbuf,
                    step=step, rows=part, pad=pad_a)
        _rwkv_prep_body(p_scr[s, :, P_A:], *pp_in[1:], *[sub(r) for r in pp_out[:7]],
                        pp_out[7], pbuf, step=step, rows=part, pad=pad_b)


def _front(x, gn, w, conv_st, h_st, cw, cb, wr, wi, br, bi, lam,
           prev, mu, w0, a0, w2p, a2p, g2, kkp, kap, *, nseq, step, rows):
    n, d = x.shape
    nt = n // (nseq * rows)
    hist = (CONV_W - 1) * step
    pad_a = max(SUBLANES, hist)
    pad_b = max(SUBLANES, step)
    row_map = lambda b, t: (b * nt + t, 0)
    seq_map = lambda b, t: (b, 0, 0)
    whole = lambda a: pl.BlockSpec(a.shape, lambda b, t: (0,) * a.ndim,
                                   pipeline_mode=pl.Buffered(1))
    va = lambda p: p.reshape(1, W_A)
    vb = lambda p: p.reshape(1, W_B)
    consts_a = [cw, va(cb), wr, wi, va(br), va(bi), va(lam)]
    consts_b = [mu.reshape(1, P_B), vb(w0), vb(a0), w2p, a2p, g2, vb(kkp), vb(kap)]
    nsplit = 2 if step == 1 else 1
    part = rows // nsplit
    kern = functools.partial(_front_kernel, step=step, rows=rows, pad_a=pad_a, pad_b=pad_b,
                             nsplit=nsplit)
    tile = lambda wd: pl.BlockSpec((rows, wd), row_map)
    return pl.pallas_call(
        kern,
        grid=(nseq, nt),
        in_specs=[tile(d), whole(gn.reshape(1, d)), whole(w),
                  pl.BlockSpec((1, hist, W_A), seq_map), pl.BlockSpec((1, step, W_A), seq_map)]
                 + [whole(a) for a in consts_a]
                 + [pl.BlockSpec((1, step, P_B), seq_map)] + [whole(a) for a in consts_b],
        out_specs=[tile(W_A), pl.BlockSpec((1, hist, W_A), seq_map),
                   pl.BlockSpec((1, step, W_A), seq_map)]
                  + [tile(W_B)] * 7 + [pl.BlockSpec((1, step, P_B), seq_map)],
        out_shape=[jax.ShapeDtypeStruct((n, W_A), F32),
                   jax.ShapeDtypeStruct((nseq, hist, W_A), F32),
                   jax.ShapeDtypeStruct((nseq, step, W_A), F32)]
                  + [jax.ShapeDtypeStruct((n, W_B), F32)] * 7
                  + [jax.ShapeDtypeStruct((nseq, step, P_B), F32)],
        scratch_shapes=[pltpu.VMEM((pad_a + part, W_A), F32),
                        pltpu.VMEM((step, W_A), F32),
                        pltpu.VMEM((part, W_A), F32),
                        pltpu.VMEM((pad_b + part, P_B), F32),
                        pltpu.VMEM((nsplit, part, P_A + P_B), F32),
                        pltpu.VMEM(w.shape, BF16)],
        compiler_params=_cparams(("parallel", "arbitrary")),
        name="front",
    )(x, gn.reshape(1, d), w, conv_st, h_st, *consts_a, prev, *consts_b)


VGROUP = 4


def _rwkv_scan_kernel(r_ref, km_ref, v_ref, lw_ref, kk_ref, bb_ref, s0_ref, o_ref, sout_ref,
                      s_scr, *, tb):
    tblk = pl.program_id(1)

    rows_per_tile = LANES // HD_B
    ntile = HD_B * HD_B // LANES

    @pl.when(tblk == 0)
    def _():
        for j in range(ntile):
            tile = jnp.transpose(s0_ref[:, j * LANES:(j + 1) * LANES])
            s_scr[j * rows_per_tile:(j + 1) * rows_per_tile] = tile.reshape(
                rows_per_tile, HD_B, LANES)

    def colsum(x):
        return jnp.sum(x, axis=0, keepdims=True)

    sub = lax.broadcasted_iota(jnp.int32, (VGROUP, LANES), 0)
    for g in range(HD_B // VGROUP):
        vbase = g * VGROUP

        def step(t, state, vbase=vbase):
            kk = kk_ref[t]
            w = jnp.exp(lw_ref[t])
            bvec = bb_ref[t]
            km = km_ref[t]
            r = r_ref[t]
            vt = v_ref[t, vbase:vbase + VGROUP, :]
            new = []
            otile = jnp.zeros((VGROUP, LANES), F32)
            for i in range(VGROUP):
                sk = colsum(state[i] * kk)
                sv = state[i] * w - sk * bvec + vt[i:i + 1, :] * km
                new.append(sv)
                otile = jnp.where(sub == i, colsum(sv * r), otile)
            o_ref[t, vbase:vbase + VGROUP, :] = otile
            return tuple(new)

        state = tuple(s_scr[vbase + i] for i in range(VGROUP))
        state = lax.fori_loop(0, tb, step, state, unroll=2)
        for i in range(VGROUP):
            s_scr[vbase + i] = state[i]

    @pl.when(tblk == pl.num_programs(1) - 1)
    def _():
        for j in range(ntile):
            tile = s_scr[j * rows_per_tile:(j + 1) * rows_per_tile].reshape(LANES, LANES)
            sout_ref[:, j * LANES:(j + 1) * LANES] = jnp.transpose(tile)


def _rwkv_scan(r, km, v, lw, kk, bb, s0, *, tb):
    nt, _, nl = r.shape
    seq = pl.BlockSpec((tb, HD_B, LANES), lambda g, t: (t, 0, g))
    st = pl.BlockSpec((LANES, HD_B * HD_B), lambda g, t: (g, 0))
    kern = functools.partial(_rwkv_scan_kernel, tb=tb)
    return pl.pallas_call(
        kern,
        grid=(nl // LANES, nt // tb),
        in_specs=[seq] * 6 + [st],
        out_specs=[seq, st],
        out_shape=[jax.ShapeDtypeStruct((nt, HD_B, nl), F32),
                   jax.ShapeDtypeStruct((nl, HD_B * HD_B), F32)],
        scratch_shapes=[pltpu.VMEM((HD_B, HD_B, LANES), F32)],
        compiler_params=_cparams(("parallel", "arbitrary")),
        name="rwkv_scan",
    )(r, km, v, lw, kk, bb, s0)


PAIR = 2 * HD_B
RWKV_SEQS_PER_STEP = 8


def _rwkv_chunk_kernel(r_ref, km_ref, v_ref, lw_ref, kk_ref, bb_ref, s0_ref, o_ref, sout_ref,
                       s_scr, *, chunk, nb):
    c = pl.program_id(1)

    @pl.when(c == 0)
    def _():
        s_scr[...] = s0_ref[...]

    lane = lax.broadcasted_iota(jnp.int32, (chunk, PAIR), 1)
    row = lax.broadcasted_iota(jnp.int32, (chunk, PAIR), 0)
    low_half = lane < HD_B
    li = jnp.where(low_half, lane, lane - HD_B)
    strict = li < row
    incl = li <= row
    eye = jnp.where(li == row, 1.0, 0.0)
    same = lambda n: (li // n) == (row // n)
    m8 = same(8)
    levels = []
    n = 16
    prev = m8
    while n <= chunk:
        cur = same(n)
        levels.append(jnp.logical_and(cur, jnp.logical_not(prev)))
        prev = cur
        n *= 2
    sq_r = lax.broadcasted_iota(jnp.int32, (PAIR, PAIR), 0)
    sq_c = lax.broadcasted_iota(jnp.int32, (PAIR, PAIR), 1)
    same_head = (sq_r < HD_B) == (sq_c < HD_B)
    low_bf = lax.broadcasted_iota(jnp.int32, (chunk, PAIR), 1) < HD_B

    def bd(s):
        return tuple(jnp.concatenate([jnp.where(low_bf, x, jnp.zeros_like(x)),
                                      jnp.where(low_bf, jnp.zeros_like(x), x)], axis=0)
                     for x in s)

    def pm1(x, y):
        return jnp.dot(x.astype(BF16), bd((y.astype(BF16),))[0], preferred_element_type=F32)

    chains = [(b, p) for b in range(nb) for p in range(H_B // 2)]
    each = lambda f: [f(i) for i in range(len(chains))]

    vv, lhs2, kg, bg, kdbd, etot, s_p = [], [], [], [], [], [], []
    for b, p in chains:
        sl = slice(p * PAIR, (p + 1) * PAIR)
        r = r_ref[b, :, sl]
        km = km_ref[b, :, sl]
        lw = lw_ref[b, :, sl]
        kk = kk_ref[b, :, sl]
        bb = bb_ref[b, :, sl]
        cum = lw
        d = 1
        while d < chunk:
            cum = cum + jnp.where(row >= d, pltpu.roll(cum, d, 0), 0.0)
            d *= 2
        tot = cum[chunk - 1:chunk, :]
        g_end = jnp.exp(tot - cum)
        g_inv = jnp.exp(-cum)
        vv.append(v_ref[b, :, sl])
        lhs2.append(_split(jnp.concatenate([kk * jnp.exp(cum - lw), r * jnp.exp(cum)], axis=0)))
        kg.append(bd(_split(km * g_inv)))
        bg.append(bd(_split(bb * g_inv)))
        kdbd.append(_split(jnp.concatenate([km * g_end, bb * g_end], axis=0)))
        etot.append(jnp.exp(tot))
        s_p.append(s_scr[b, p])

    ss = each(lambda i: lax.dot_general(lhs2[i][0], s_p[i].astype(BF16), (((1,), (1,)), ((), ())),
                                        preferred_element_type=F32))
    kb = each(lambda i: _mm(lhs2[i], tuple(jnp.concatenate([x, y], axis=0)
                                           for x, y in zip(kg[i], bg[i])), nt=True))
    a_b = each(lambda i: jnp.where(strict, kb[i][:chunk, PAIR:], 0.0))
    b_b = each(lambda i: jnp.where(incl, kb[i][chunk:, PAIR:], 0.0))
    av = each(lambda i: pm1(jnp.concatenate(
        [jnp.where(strict, kb[i][:chunk, :PAIR], 0.0), jnp.where(incl, kb[i][chunk:, :PAIR], 0.0)],
        axis=0), vv[i]))

    dg = each(lambda i: jnp.where(m8, a_b[i], 0.0))
    d2 = each(lambda i: pm1(dg[i], dg[i]))
    d3 = each(lambda i: pm1(dg[i], d2[i]))
    d4 = each(lambda i: pm1(d2[i], d2[i]))
    t1 = each(lambda i: eye - dg[i] + d2[i] - d3[i])
    inv = each(lambda i: t1[i] + pm1(t1[i], d4[i]))
    for m in levels:
        mlev = each(lambda i: pm1(inv[i], jnp.where(m, a_b[i], 0.0)))
        inv = each(lambda i: inv[i] - pm1(mlev[i], inv[i]))

    u = each(lambda i: pm1(inv[i], ss[i][:chunk] + av[i][:chunk]))
    upd = each(lambda i: _mm(_split(jnp.transpose(jnp.concatenate([vv[i], -u[i]], axis=0))),
                             kdbd[i]))
    bu = each(lambda i: pm1(b_b[i], u[i]))
    for i, (b, p) in enumerate(chains):
        sl = slice(p * PAIR, (p + 1) * PAIR)
        o_ref[b, :, sl] = ss[i][chunk:] + av[i][chunk:] - bu[i]
        s_scr[b, p] = s_p[i] * etot[i] + jnp.where(same_head, upd[i], 0.0)

    sout_ref[...] = s_scr[...]


def _rwkv_chunk(r, km, v, lw, kk, bb, s0, *, chunk, nb):
    nseq, t, _ = r.shape
    assert chunk == HD_B and nseq % nb == 0
    npair = H_B // 2
    row = pl.BlockSpec((nb, chunk, W_B), lambda b, c: (b, c, 0))
    st = pl.BlockSpec((nb, npair, PAIR, PAIR), lambda b, c: (b, 0, 0, 0))
    kern = functools.partial(_rwkv_chunk_kernel, chunk=chunk, nb=nb)
    return pl.pallas_call(
        kern,
        grid=(nseq // nb, t // chunk),
        in_specs=[row] * 6 + [st],
        out_specs=[row, st],
        out_shape=[jax.ShapeDtypeStruct((nseq, t, W_B), F32),
                   jax.ShapeDtypeStruct((nseq, npair, PAIR, PAIR), F32)],
        scratch_shapes=[pltpu.VMEM((nb, npair, PAIR, PAIR), F32)],
        compiler_params=_cparams(("parallel", "arbitrary")),
        name="rwkv_chunk",
    )(r, km, v, lw, kk, bb, s0)


MXU_N = 256


def _mix_residual(x_ref, oa_ref, o_ref, r_ref, km_ref, v_ref, g_ref, lnw_ref, lnb_ref, rk_ref,
                  wa_ref, wb_ref):
    ones = _head_ones(W_B)
    o = o_ref[...]
    cen = o - _head_sum(o, ones) * (1.0 / HD_B)
    head_sum1 = lambda z: jnp.dot(z.astype(BF16), ones, preferred_element_type=F32)
    var = head_sum1(cen * cen) * (1.0 / HD_B)
    on = cen * lax.rsqrt(var + GN_EPS) * lnw_ref[...] + lnb_ref[...]
    bonus = head_sum1(r_ref[...] * km_ref[...] * rk_ref[...]) * v_ref[...]
    ob = (on + bonus) * g_ref[...]
    return (x_ref[...]
            + jnp.dot(oa_ref[...].astype(BF16), wa_ref[...], preferred_element_type=F32)
            + jnp.dot(ob.astype(BF16), wb_ref[...], preferred_element_type=F32))


def _proj_residual(x_ref, o_ref, w_ref):
    return x_ref[...] + jnp.dot(o_ref[...].astype(BF16), w_ref[...],
                                preferred_element_type=F32)


def _res_ffn_kernel(*refs, residual, n_res, final_norm):
    g_ref, wg_ref, wu_ref, wd_ref, gf_ref, y_ref, xf_scr, x_scr, h_scr = refs[n_res:]
    x = residual(*refs[:n_res])
    xf_scr[...] = _rms(x, g_ref[...]).astype(BF16)
    x_scr[...] = x
    for j in range(wg_ref.shape[1] // MXU_N):
        cols = slice(j * MXU_N, (j + 1) * MXU_N)
        xf = xf_scr[...]
        hg = jnp.dot(xf, wg_ref[:, cols], preferred_element_type=F32)
        hu = jnp.dot(xf, wu_ref[:, cols], preferred_element_type=F32)
        h_scr[:, cols] = ((hg * _sigmoid(hg)) * hu).astype(BF16)
    y = x_scr[...] + jnp.dot(h_scr[...], wd_ref[...], preferred_element_type=F32)
    if final_norm:
        y = _rms(y, gf_ref[...])
    y_ref[...] = y


def _res_ffn(residual, row_ins, const_ins, g, wg, wu, wd, gf, *, tm, final_norm):
    n, d = row_ins[0].shape
    whole = lambda a: pl.BlockSpec(a.shape, lambda i: (0,) * a.ndim,
                                   pipeline_mode=pl.Buffered(1))
    consts = list(const_ins) + [g.reshape(1, d), wg, wu, wd, gf.reshape(1, d)]
    kern = functools.partial(_res_ffn_kernel, residual=residual,
                             n_res=len(row_ins) + len(const_ins), final_norm=final_norm)
    return pl.pallas_call(
        kern,
        grid=(n // tm,),
        in_specs=[pl.BlockSpec((tm, a.shape[1]), lambda i: (i, 0)) for a in row_ins]
                 + [whole(a) for a in consts],
        out_specs=pl.BlockSpec((tm, d), lambda i: (i, 0)),
        out_shape=jax.ShapeDtypeStruct((n, d), F32),
        scratch_shapes=[pltpu.VMEM((tm, d), BF16), pltpu.VMEM((tm, d), F32),
                        pltpu.VMEM((tm, wg.shape[1]), BF16)],
        compiler_params=_cparams(("parallel",)),
        name="res_ffn",
    )(*row_ins, *consts)


HGRN_SAFE_LOG = 80.0


def _hgrn_kernel(q_ref, f_ref, i_ref, g_ref, s0_ref, lbc_ref, gn_ref, o_ref, sout_ref,
                 s_scr, att_scr, *, layer, chunk, valid, nb):
    c = pl.program_id(1)

    @pl.when(c == 0)
    def _():
        s_scr[...] = s0_ref[...]

    lbc = lbc_ref[...]
    e = jnp.exp(lbc - jnp.max(lbc, axis=0, keepdims=True))
    sm = e / jnp.sum(e, axis=0, keepdims=True)
    lbf = sm[1:2, :]
    for l in range(2, layer + 1):
        lbf = lbf + sm[l:l + 1, :]

    nblk = chunk // SUBLANES
    sub = lax.broadcasted_iota(jnp.int32, (SUBLANES, DK_C), 0)
    lane_c = lax.broadcasted_iota(jnp.int32, (SUBLANES, chunk), 1)
    rowc = lax.broadcasted_iota(jnp.int32, (chunk, F_C), 0)
    tril = jnp.where(lax.broadcasted_iota(jnp.int32, (chunk, chunk), 0)
                     >= lax.broadcasted_iota(jnp.int32, (chunk, chunk), 1), 1.0, 0.0).astype(BF16)

    qs, ks, bs, ivs = [], [], [], []
    for b in range(nb):
        fr = f_ref[b]
        ez = jnp.exp(-jnp.abs(fr))
        rz = 1.0 / (1.0 + ez)
        sig = jnp.where(fr >= 0.0, rz, ez * rz)
        nsig = jnp.where(fr >= 0.0, ez * rz, rz)
        logf = jnp.log(lbf + (1.0 - lbf) * sig)
        kfull = (1.0 - lbf) * nsig
        if valid < chunk:
            logf = jnp.where(rowc < valid, logf, 0.0)
            kfull = jnp.where(rowc < valid, kfull, 0.0)
        hi, lo = _split(logf)
        bc = jnp.dot(jnp.concatenate([tril, tril], axis=1), jnp.concatenate([hi, lo], axis=0),
                     preferred_element_type=F32)
        qs.append(jax.nn.silu(q_ref[b]))
        ks.append(kfull)
        bs.append(bc)
        ivs.append(i_ref[b].astype(BF16))

    chains = [(b, h) for b in range(nb) for h in range(H_C)]
    each = lambda f: [f(i) for i in range(len(chains))]
    hsl = lambda h: slice(h * DK_C, (h + 1) * DK_C)
    q = each(lambda i: qs[chains[i][0]][:, hsl(chains[i][1])])
    k = each(lambda i: ks[chains[i][0]][:, hsl(chains[i][1])])
    bcs = each(lambda i: bs[chains[i][0]][:, hsl(chains[i][1])])
    iv = each(lambda i: ivs[chains[i][0]][:, hsl(chains[i][1])])
    s_h = each(lambda i: s_scr[chains[i][0], chains[i][1]])

    def off_diag(i, tb):
        lo = tb * SUBLANES
        rs = slice(lo, lo + SUBLANES)
        edge = bcs[i][lo - 1:lo, :]
        qi = q[i][rs, :] * jnp.exp(bcs[i][rs, :] - edge)
        ki = k[i][0:lo, :] * jnp.exp(edge - bcs[i][0:lo, :])
        ki = jnp.concatenate([ki, jnp.zeros((chunk - lo, DK_C), F32)], axis=0)
        return lax.dot_general(qi.astype(BF16), ki.astype(BF16), (((1,), (1,)), ((), ())),
                               preferred_element_type=F32)

    inter = each(lambda i: jnp.dot((q[i] * jnp.exp(bcs[i])).astype(BF16), s_h[i].astype(BF16),
                                   preferred_element_type=F32))


    sub_c = lax.broadcasted_iota(jnp.int32, (SUBLANES, chunk), 0)

    def att_on_matrix_unit():
        def block(i, tb):
            lo = tb * SUBLANES
            rs = slice(lo, lo + SUBLANES)
            dloc = bcs[i][rs, :] if tb == 0 else bcs[i][rs, :] - bcs[i][lo - 1:lo, :]
            qi = q[i][rs, :] * jnp.exp(dloc)
            parts = [k[i][rs, :] * jnp.exp(-dloc)]
            if tb > 0:
                parts.insert(0, k[i][0:lo, :] * jnp.exp(bcs[i][lo - 1:lo, :] - bcs[i][0:lo, :]))
            if chunk - lo - SUBLANES > 0:
                parts.append(jnp.zeros((chunk - lo - SUBLANES, DK_C), F32))
            ki = jnp.concatenate(parts, axis=0) if len(parts) > 1 else parts[0]
            return lax.dot_general(qi.astype(BF16), ki.astype(BF16), (((1,), (1,)), ((), ())),
                                   preferred_element_type=F32)

        blks = each(lambda i: [block(i, tb) for tb in range(nblk)])
        for i in range(len(chains)):
            for tb in range(nblk):
                att_scr[i, tb * SUBLANES:(tb + 1) * SUBLANES, :] = jnp.where(
                    lane_c <= sub_c + tb * SUBLANES, blks[i][tb], 0.0)

    def att_pairwise_diagonal():
        offd = each(lambda i: [off_diag(i, tb) for tb in range(1, nblk)])
        for i in range(len(chains)):
            for tb in range(nblk):
                rs = slice(tb * SUBLANES, (tb + 1) * SUBLANES)
                blk = jnp.zeros((SUBLANES, chunk), F32) if tb == 0 else offd[i][tb - 1]
                for j in range(SUBLANES):
                    s = tb * SUBLANES + j
                    if s >= valid:
                        break
                    diff = jnp.where(sub >= j, bcs[i][rs, :] - bcs[i][s:s + 1, :], -jnp.inf)
                    ev = jnp.exp(diff) * (q[i][rs, :] * k[i][s:s + 1, :])
                    blk = jnp.where(lane_c == s, jnp.sum(ev, axis=1, keepdims=True), blk)
                att_scr[i, rs, :] = blk

    def new_state(i):
        bl = bcs[i][chunk - 1:chunk, :]
        kd = k[i] * jnp.exp(bl - bcs[i])
        pieces = [kd, jnp.broadcast_to(jnp.exp(bl), (SUBLANES, DK_C))]
        fill = DK_C - chunk - SUBLANES
        if fill > 0:
            pieces.append(jnp.zeros((fill, DK_C), F32))
        xt = jnp.transpose(jnp.concatenate(pieces, axis=0))
        return (xt[:, chunk:chunk + 1] * s_h[i]
                + jnp.dot(xt[:, 0:chunk].astype(BF16), iv[i], preferred_element_type=F32))

    if nblk > 1:
        att_on_matrix_unit()
        worst = jnp.float32(0.0)
        for b in range(nb):
            win = bs[b] - jnp.where(rowc >= SUBLANES, pltpu.roll(bs[b], SUBLANES, 0), 0.0)
            worst = jnp.minimum(worst, jnp.min(win))
        pl.when(worst < -HGRN_SAFE_LOG)(att_pairwise_diagonal)
    else:
        att_pairwise_diagonal()

    o_h = each(lambda i: jnp.dot(att_scr[i].astype(BF16), iv[i], preferred_element_type=F32)
               + inter[i])
    s_new = each(new_state)
    for i, (b, h) in enumerate(chains):
        s_scr[b, h] = s_new[i]
    for b in range(nb):
        o = jnp.concatenate([o_h[b * H_C + h] for h in range(H_C)], axis=1)
        o_ref[b] = _rms(o, gn_ref[...]) * jax.nn.silu(g_ref[b])
    sout_ref[...] = s_scr[...]


def _hgrn(pc3, s0, lbc, gn, *, layer, chunk, valid, nb):
    nseq, t, _ = pc3.shape
    assert nseq % nb == 0
    col = lambda j: pl.BlockSpec((nb, chunk, F_C), lambda b, c, j=j: (b, c, j))
    st = pl.BlockSpec((nb, H_C, DK_C, DK_C), lambda b, c: (b, 0, 0, 0))
    kern = functools.partial(_hgrn_kernel, layer=layer, chunk=chunk, valid=valid, nb=nb)
    return pl.pallas_call(
        kern,
        grid=(nseq // nb, t // chunk),
        in_specs=[col(0), col(1), col(2), col(3), st,
                  pl.BlockSpec(lbc.shape, lambda b, c: (0, 0)),
                  pl.BlockSpec((1, D_MODEL), lambda b, c: (0, 0))],
        out_specs=[pl.BlockSpec((nb, chunk, D_MODEL), lambda b, c: (b, c, 0)), st],
        out_shape=[jax.ShapeDtypeStruct((nseq, t, D_MODEL), F32),
                   jax.ShapeDtypeStruct(s0.shape, F32)],
        scratch_shapes=[pltpu.VMEM((nb, H_C, DK_C, DK_C), F32),
                        pltpu.VMEM((nb * H_C, chunk, chunk), F32)],
        compiler_params=_cparams(("parallel", "arbitrary")),
        name="hgrn",
    )(pc3, pc3, pc3, pc3, s0, lbc, gn.reshape(1, D_MODEL))


def _block_diag(w):
    h, n, _ = w.shape
    eye = jnp.eye(h, dtype=w.dtype)
    return (eye[:, None, :, None] * w[:, :, None, :]).reshape(h * n, h * n)


def _fold(x):
    half = x.shape[-2] // 2
    return jnp.concatenate([x[..., :half, :], x[..., half:, :]], axis=-1)


def _unfold(x):
    half = x.shape[-1] // 2
    return jnp.concatenate([x[..., :half], x[..., half:]], axis=-2)


def _trunk(x2d, conv_st, h_st, shift_st, rs_st, hs_st, P, *, nbatch, nsteps, time_major):
    n = x2d.shape[0]
    if time_major:
        nseq, step, rows = 1, nbatch, n
        tm = n
    else:
        nseq, step, rows = nbatch, 1, 512
        tm = 512
    nchain = nbatch * H_B

    out_a, conv_new, h_new, r, km, v, lw, kk, bb, g, shift_new = _front(
        x2d, P['ln_mix'][0], P['w_in_ab'],
        conv_st, h_st, P['conv_w'], P['conv_b'], P['wr'], P['wi'], P['gr_b'], P['gi_b'],
        P['lru_lambda'],
        shift_st, P['mu_b'], P['w0_b'], P['a0_b'], P['w2p'], P['a2p'], P['g2_b'],
        P['kk_b'], P['ka_b'], nseq=nseq, step=step, rows=rows)

    if time_major:
        def to_lanes(z):
            z = z.reshape(nsteps, nbatch, H_B, HD_B).transpose(0, 3, 1, 2)
            return z.reshape(nsteps, HD_B, nchain)

        s0 = rs_st.reshape(nchain, HD_B * HD_B)
        o, s_new = _rwkv_scan(to_lanes(r), to_lanes(km), to_lanes(v), to_lanes(lw), to_lanes(kk),
                              to_lanes(bb), s0, tb=nsteps)
        o = o.reshape(nsteps, HD_B, nbatch, H_B).transpose(0, 2, 3, 1).reshape(n, W_B)
        rs_new = s_new.reshape(nbatch, H_B, HD_B, HD_B)
    else:
        z = jnp.zeros_like(rs_st[:, 0::2])
        s0 = jnp.concatenate([jnp.concatenate([rs_st[:, 0::2], z], axis=-1),
                              jnp.concatenate([z, rs_st[:, 1::2]], axis=-1)], axis=-2)
        seq3 = lambda y: y.reshape(nbatch, nsteps, W_B)
        o, s_new = _rwkv_chunk(seq3(r), seq3(km), seq3(v), seq3(lw), seq3(kk), seq3(bb), s0,
                               chunk=HD_B, nb=RWKV_SEQS_PER_STEP)
        o = o.reshape(n, W_B)
        rs_new = jnp.stack([s_new[:, :, :HD_B, :HD_B], s_new[:, :, HD_B:, HD_B:]],
                           axis=2).reshape(nbatch, H_B, HD_B, HD_B)

    vec = lambda p: p.reshape(1, W_B)
    x2 = _res_ffn(_mix_residual, [x2d, out_a, o, r, km, v, g],
                  [vec(P['lnx_w']), vec(P['lnx_b']), vec(P['rk_b']), P['w_out_a'], P['w_out_b']],
                  P['ln_ffn'][0], P['ffn_gate'][0], P['ffn_up'][0], P['ffn_down'][0],
                  P['ln_final'], tm=tm, final_norm=False)

    pc = _norm_matmul(x2, P['ln_mix'][1], P['w_in_c'], tm)
    if time_major:
        pc3 = pc.reshape(nsteps, nbatch, 4 * F_C).transpose(1, 0, 2)
        chunk = SUBLANES
        pc3 = jnp.pad(pc3, ((0, 0), (0, chunk - nsteps), (0, 0)))
    else:
        pc3 = pc.reshape(nbatch, nsteps, 4 * F_C)
        chunk = 64
    o3, hs_new = _hgrn(pc3, hs_st, P['lb_c'], P['gn_c'], layer=1, chunk=chunk,
                       valid=min(chunk, nsteps), nb=4)
    if time_major:
        o = o3[:, :nsteps].transpose(1, 0, 2).reshape(n, D_MODEL)
    else:
        o = o3.reshape(n, D_MODEL)
    y = _res_ffn(_proj_residual, [x2, o], [P['w_out_c']],
                 P['ln_ffn'][1], P['ffn_gate'][1], P['ffn_up'][1], P['ffn_down'][1],
                 P['ln_final'], tm=tm, final_norm=True)
    return y, conv_new, h_new, shift_new, rs_new, hs_new


def kernel(x_prompt, x_sample, state_rglru_conv, state_rglru_h, state_rwkv_shift, state_rwkv_S,
           state_hgrn_S, ln_mix, ln_ffn, ln_final, w_in_ab, conv_w, conv_b, gr_w, gr_b, gi_w,
           gi_b, lru_lambda, mu_b, w0_b, w2_b, a0_b, a2_b, g2_b, kk_b, ka_b, rk_b, lnx_w, lnx_b,
           w_out_ab, w_in_c, lb_c, gn_c, w_out_c, ffn_gate, ffn_up, ffn_down):
    bp, tp, _ = x_prompt.shape
    bs, ts, _ = x_sample.shape
    zpad_w = jnp.zeros((LORA_A, W_B), F32)
    zpad_a = jnp.zeros((LORA_W, W_B), F32)
    P = dict(
        ln_mix=ln_mix, ln_ffn=ln_ffn, ln_final=ln_final,
        w_in_ab=w_in_ab[0], conv_w=conv_w[0], conv_b=conv_b[0],
        wr=_block_diag(gr_w[0]).astype(BF16), wi=_block_diag(gi_w[0]).astype(BF16),
        gr_b=gr_b[0], gi_b=gi_b[0], lru_lambda=lru_lambda[0], mu_b=mu_b[0], w0_b=w0_b[0],
        a0_b=a0_b[0],
        w2p=jnp.concatenate([w2_b[0], zpad_w], axis=0).astype(BF16),
        a2p=jnp.concatenate([zpad_a, a2_b[0]], axis=0).astype(BF16),
        g2_b=g2_b[0].astype(BF16), kk_b=kk_b[0], ka_b=ka_b[0], rk_b=rk_b[0],
        lnx_w=lnx_w[0], lnx_b=lnx_b[0],
        w_out_a=w_out_ab[0, :W_A].astype(BF16), w_out_b=w_out_ab[0, W_A:].astype(BF16),
        w_in_c=w_in_c[0], lb_c=lb_c, gn_c=gn_c[0],
        w_out_c=w_out_c[0].astype(BF16), ffn_gate=ffn_gate.astype(BF16),
        ffn_up=ffn_up.astype(BF16), ffn_down=ffn_down.astype(BF16))

    yp, p_conv, p_h, p_shift, p_rs, p_hs = _trunk(
        x_prompt.reshape(bp * tp, D_MODEL),
        jnp.zeros((bp, CONV_W - 1, W_A), F32), jnp.zeros((bp, 1, W_A), F32),
        jnp.zeros((bp, 1, P_B), F32), jnp.zeros((bp, H_B, HD_B, HD_B), F32),
        jnp.zeros((bp, H_C, DK_C, DK_C), F32), P, nbatch=bp, nsteps=tp, time_major=False)

    ys, s_conv, s_h, s_shift, s_rs, s_hs = _trunk(
        x_sample.transpose(1, 0, 2).reshape(ts * bs, D_MODEL),
        state_rglru_conv[0].transpose(1, 0, 2).reshape(1, (CONV_W - 1) * bs, W_A),
        state_rglru_h[0].reshape(1, bs, W_A), state_rwkv_shift[0].reshape(1, bs, P_B),
        state_rwkv_S[0], state_hgrn_S[0], P, nbatch=bs, nsteps=ts, time_major=True)

    return (yp.reshape(bp, tp, D_MODEL),
            ys.reshape(ts, bs, D_MODEL).transpose(1, 0, 2),
            p_conv[None], p_h.reshape(1, bp, W_A), p_shift.reshape(1, bp, P_B), p_rs[None],
            p_hs[None],
            s_conv.reshape(CONV_W - 1, bs, W_A).transpose(1, 0, 2)[None],
            s_h.reshape(1, bs, W_A), s_shift.reshape(1, bs, P_B), s_rs[None], s_hs[None])
```

```python
import functools

import jax
import jax.numpy as jnp
from jax import lax
from jax.experimental import pallas as pl
from jax.experimental.pallas import tpu as pltpu

F32 = jnp.float32
BF16 = jnp.bfloat16

D_MODEL = 1024
W_A = 512
H_A = 8
CONV_W = 4
LRU_C = 8.0
W_B = 512
HD_B = 64
H_B = 8
LORA_W = 64
LORA_A = 64
LORA_G = 128
P_A = 2 * W_A
P_B = 3 * W_B + LORA_W + LORA_A + LORA_G
DK_C = 128
H_C = 8
F_C = 1024
D_FF = 2816
RMS_EPS = 1e-6
GN_EPS = 64e-5

LANES = 128
SUBLANES = 8
VMEM_LIMIT = 56 * 1024 * 1024


def _cparams(sem):
    return pltpu.CompilerParams(dimension_semantics=sem, vmem_limit_bytes=VMEM_LIMIT)


def _softplus(x):
    return jnp.maximum(x, 0.0) + jnp.log1p(jnp.exp(-jnp.abs(x)))

def _sigmoid(x):
    return jax.nn.sigmoid(x)


def _rms(x, g):
    ms = jnp.mean(x * x, axis=-1, keepdims=True)
    return x * lax.rsqrt(ms + RMS_EPS) * g


def _split(x):
    hi = x.astype(BF16)
    lo = (x - hi.astype(F32)).astype(BF16)
    return hi, lo


def _mm(a, b, nt=False):
    ah, al = a
    bh, bl = b
    lhs = jnp.concatenate([ah, al, ah], axis=1)
    if nt:
        rhs = jnp.concatenate([bh, bh, bl], axis=1)
        return lax.dot_general(lhs, rhs, (((1,), (1,)), ((), ())), preferred_element_type=F32)
    rhs = jnp.concatenate([bh, bh, bl], axis=0)
    return jnp.dot(lhs, rhs, preferred_element_type=F32)


def _head_ones(width):
    r = lax.broadcasted_iota(jnp.int32, (width, width), 0) // HD_B
    c = lax.broadcasted_iota(jnp.int32, (width, width), 1) // HD_B
    return jnp.where(r == c, 1.0, 0.0).astype(BF16)


def _head_sum(x, ones):
    hi, lo = _split(x)
    return jnp.dot(jnp.concatenate([hi, lo], axis=1), jnp.concatenate([ones, ones], axis=0),
                   preferred_element_type=F32)


def _norm_matmul_kernel(x_ref, g_ref, w_ref, o_ref, w_bf):
    @pl.when(pl.program_id(0) == 0)
    def _():
        w_bf[...] = w_ref[...].astype(BF16)

    xn = _rms(x_ref[...], g_ref[...]).astype(BF16)
    o_ref[...] = jnp.dot(xn, w_bf[...], preferred_element_type=F32)


def _norm_matmul(x, g, w, tm):
    n, d = x.shape
    p = w.shape[1]
    assert n % tm == 0
    return pl.pallas_call(
        _norm_matmul_kernel,
        grid=(n // tm,),
        in_specs=[pl.BlockSpec((tm, d), lambda i: (i, 0)),
                  pl.BlockSpec((1, d), lambda i: (0, 0)),
                  pl.BlockSpec((d, p), lambda i: (0, 0), pipeline_mode=pl.Buffered(1))],
        out_specs=pl.BlockSpec((tm, p), lambda i: (i, 0)),
        out_shape=jax.ShapeDtypeStruct((n, p), F32),
        scratch_shapes=[pltpu.VMEM((d, p), BF16)],
        compiler_params=_cparams(("arbitrary",)),
        name="norm_matmul",
    )(x, g.reshape(1, d), w)


def _rglru_body(u_in, gate, cw_ref, cb_ref, wr_ref, wi_ref, br_ref, bi_ref, lam_ref,
                out_ref, cnew_ref, hnew_ref, xbuf, hcar, bbuf, *, step, rows, pad):
    hist = (CONV_W - 1) * step

    if step == 1:
        u = u_in
        prev = xbuf[0:SUBLANES, :]
        sub8 = lax.broadcasted_iota(jnp.int32, (SUBLANES, W_A), 0)
        conv = cb_ref[...]
        for j in range(CONV_W):
            d = CONV_W - 1 - j
            if d == 0:
                ush = u
            else:
                rolled = pltpu.roll(u, d, 0)
                head = jnp.where(sub8 < d, pltpu.roll(prev, d, 0), rolled[0:SUBLANES, :])
                ush = jnp.concatenate([head, rolled[SUBLANES:, :]], axis=0)
            conv = conv + ush * cw_ref[j:j + 1, :]
        cnew_ref[0] = u[rows - hist:rows, :]
        xbuf[0:SUBLANES, :] = u[rows - SUBLANES:rows, :]
    else:
        xbuf[pad:pad + rows, :] = u_in
        conv = cb_ref[...] + xbuf[pad - hist:pad - hist + rows, :] * cw_ref[0:1, :]
        for j in range(1, CONV_W):
            o = pad - hist + j * step
            conv = conv + xbuf[o:o + rows, :] * cw_ref[j:j + 1, :]
        tail = xbuf[pad + rows - hist:pad + rows, :]
        cnew_ref[0] = tail
        xbuf[pad - hist:pad, :] = tail

    ub = conv.astype(BF16)
    r = _sigmoid(jnp.dot(ub, wr_ref[...], preferred_element_type=F32) + br_ref[...])
    ig = _sigmoid(jnp.dot(ub, wi_ref[...], preferred_element_type=F32) + bi_ref[...])
    log_a = (-LRU_C) * r * _softplus(-lam_ref[...])
    a = jnp.exp(log_a)
    bterm = jnp.sqrt(-jnp.tanh(log_a) * (a * a + 1.0)) * (ig * conv)
    bbuf[...] = bterm
    bbuf[0:step, :] = bterm[0:step, :] + a[0:step, :] * hcar[...]
    bv = bbuf[...]

    def scan_levels(a, bv, pos, first, count):
        d = first
        while d < first * count:
            m = pos >= d
            bv = jnp.where(m, a * pltpu.roll(bv, d, 0) + bv, bv)
            a = jnp.where(m, a * pltpu.roll(a, d, 0), a)
            d *= 2
        return a, bv

    row = lax.broadcasted_iota(jnp.int32, (rows, W_A), 0)
    a, bv = scan_levels(a, bv, row, step, rows // step)
    hcar[...] = bv[rows - step:rows, :]
    hnew_ref[0] = bv[rows - step:rows, :]
    out_ref[...] = jax.nn.gelu(gate) * bv


def _rwkv_prep_body(pf, mu_ref, w0_ref, a0_ref, w2_ref, a2_ref, g2_ref, kkp_ref, kap_ref,
                    r_ref, km_ref, v_ref, lw_ref, kk_ref, bb_ref, g_ref, snew_ref, pbuf,
                    *, step, rows, pad):

    if step == 1:
        rolled = pltpu.roll(pf, 1, 0)
        sub8 = lax.broadcasted_iota(jnp.int32, (SUBLANES, P_B), 0)
        head = jnp.where(sub8 < 1, pltpu.roll(pbuf[0:SUBLANES, :], 1, 0), rolled[0:SUBLANES, :])
        shifted = jnp.concatenate([head, rolled[SUBLANES:, :]], axis=0)
        pbuf[0:SUBLANES, :] = pf[rows - SUBLANES:rows, :]
    else:
        pbuf[pad:pad + rows, :] = pf
        shifted = pbuf[pad - step:pad - step + rows, :]
        pbuf[pad - step:pad, :] = pf[rows - step:rows, :]
    snew_ref[0] = pf[rows - step:rows, :]

    m = pf + (shifted - pf) * mu_ref[...]
    o1 = 3 * W_B
    r_ref[...] = m[:, 0:W_B]
    kraw = m[:, W_B:2 * W_B]
    v_ref[...] = m[:, 2 * W_B:o1]
    xwa = m[:, o1:o1 + LORA_W + LORA_A]
    xg = m[:, o1 + LORA_W + LORA_A:]
    lane = lax.broadcasted_iota(jnp.int32, xwa.shape, 1)
    lhs = jnp.where(lane < LORA_W, jnp.tanh(xwa), xwa).astype(BF16)
    lw = jnp.dot(lhs, w2_ref[...], preferred_element_type=F32)
    la = jnp.dot(lhs, a2_ref[...], preferred_element_type=F32)
    w_log = -_softplus(-(w0_ref[...] + lw)) - 0.5
    lw_ref[...] = -jnp.exp(w_log)
    a = _sigmoid(a0_ref[...] + la)
    g_ref[...] = jnp.dot(_sigmoid(xg).astype(BF16), g2_ref[...], preferred_element_type=F32)
    kk = kraw * kkp_ref[...]
    kk = kk * lax.rsqrt(jnp.maximum(_head_sum(kk * kk, _head_ones(W_B)), 1e-24))
    kk_ref[...] = kk
    bb_ref[...] = kk * a
    km_ref[...] = kraw * (1.0 + (a - 1.0) * kap_ref[...])


N_RGLRU_IN = 9
N_PREP_IN = 9


def _front_kernel(*refs, step, rows, pad_a, pad_b, nsplit):
    x_ref, gn_ref, w_ref = refs[:3]
    rg_in = refs[3:3 + N_RGLRU_IN]
    pp_in = refs[3 + N_RGLRU_IN:3 + N_RGLRU_IN + N_PREP_IN]
    outs = refs[3 + N_RGLRU_IN + N_PREP_IN:]
    rg_out, pp_out = outs[:3], outs[3:11]
    xbuf, hcar, bbuf, pbuf, p_scr, w_bf = outs[11:]
    part = rows // nsplit

    @pl.when(pl.program_id(1) == 0)
    def _():
        hist = (CONV_W - 1) * step
        xbuf[pad_a - hist:pad_a, :] = rg_in[0][0]
        hcar[...] = rg_in[1][0]
        pbuf[pad_b - step:pad_b, :] = pp_in[0][0]
        w_bf[...] = w_ref[...].astype(BF16)

    xn = _rms(x_ref[...], gn_ref[...]).astype(BF16)
    for s in range(nsplit):
        p_scr[s] = jnp.dot(xn[s * part:(s + 1) * part, :], w_bf[...],
                           preferred_element_type=F32)
    for s in range(nsplit):
        sub = lambda ref, s=s: ref.at[s * part:(s + 1) * part, :]
        _rglru_body(p_scr[s, :, 0:W_A], p_scr[s, :, W_A:P_A], *rg_in[2:],
                    sub(rg_out[0]), rg_out[1], rg_out[2], xbuf, hcar, bbuf,
                    step=step, rows=part, pad=pad_a)
        _rwkv_prep_body(p_scr[s, :, P_A:], *pp_in[1:], *[sub(r) for r in pp_out[:7]],
                        pp_out[7], pbuf, step=step, rows=part, pad=pad_b)


def _front(x, gn, w, conv_st, h_st, cw, cb, wr, wi, br, bi, lam,
           prev, mu, w0, a0, w2p, a2p, g2, kkp, kap, *, nseq, step, rows):
    n, d = x.shape
    nt = n // (nseq * rows)
    hist = (CONV_W - 1) * step
    pad_a = max(SUBLANES, hist)
    pad_b = max(SUBLANES, step)
    row_map = lambda b, t: (b * nt + t, 0)
    seq_map = lambda b, t: (b, 0, 0)
    whole = lambda a: pl.BlockSpec(a.shape, lambda b, t: (0,) * a.ndim,
                                   pipeline_mode=pl.Buffered(1))
    va = lambda p: p.reshape(1, W_A)
    vb = lambda p: p.reshape(1, W_B)
    consts_a = [cw, va(cb), wr, wi, va(br), va(bi), va(lam)]
    consts_b = [mu.reshape(1, P_B), vb(w0), vb(a0), w2p, a2p, g2, vb(kkp), vb(kap)]
    nsplit = 2 if step == 1 else 1
    part = rows // nsplit
    kern = functools.partial(_front_kernel, step=step, rows=rows, pad_a=pad_a, pad_b=pad_b,
                             nsplit=nsplit)
    tile = lambda wd: pl.BlockSpec((rows, wd), row_map)
    return pl.pallas_call(
        kern,
        grid=(nseq, nt),
        in_specs=[tile(d), whole(gn.reshape(1, d)), whole(w),
                  pl.BlockSpec((1, hist, W_A), seq_map), pl.BlockSpec((1, step, W_A), seq_map)]
                 + [whole(a) for a in consts_a]
                 + [pl.BlockSpec((1, step, P_B), seq_map)] + [whole(a) for a in consts_b],
        out_specs=[tile(W_A), pl.BlockSpec((1, hist, W_A), seq_map),
                   pl.BlockSpec((1, step, W_A), seq_map)]
                  + [tile(W_B)] * 7 + [pl.BlockSpec((1, step, P_B), seq_map)],
        out_shape=[jax.ShapeDtypeStruct((n, W_A), F32),
                   jax.ShapeDtypeStruct((nseq, hist, W_A), F32),
                   jax.ShapeDtypeStruct((nseq, step, W_A), F32)]
                  + [jax.ShapeDtypeStruct((n, W_B), F32)] * 7
                  + [jax.ShapeDtypeStruct((nseq, step, P_B), F32)],
        scratch_shapes=[pltpu.VMEM((pad_a + part, W_A), F32),
                        pltpu.VMEM((step, W_A), F32),
                        pltpu.VMEM((part, W_A), F32),
                        pltpu.VMEM((pad_b + part, P_B), F32),
                        pltpu.VMEM((nsplit, part, P_A + P_B), F32),
                        pltpu.VMEM(w.shape, BF16)],
        compiler_params=_cparams(("parallel", "arbitrary")),
        name="front",
    )(x, gn.reshape(1, d), w, conv_st, h_st, *consts_a, prev, *consts_b)


VGROUP = 4


def _rwkv_scan_kernel(r_ref, km_ref, v_ref, lw_ref, kk_ref, bb_ref, s0_ref, o_ref, sout_ref,
                      s_scr, *, tb):
    tblk = pl.program_id(1)

    @pl.when(tblk == 0)
    def _():
        s_scr[...] = s0_ref[...]

    def colsum(x):
        return jnp.sum(x, axis=0, keepdims=True)

    sub = lax.broadcasted_iota(jnp.int32, (VGROUP, LANES), 0)
    for g in range(HD_B // VGROUP):
        vbase = g * VGROUP

        def step(t, state, vbase=vbase):
            kk = kk_ref[t]
            w = jnp.exp(lw_ref[t])
            bvec = bb_ref[t]
            km = km_ref[t]
            r = r_ref[t]
            vt = v_ref[t, vbase:vbase + VGROUP, :]
            new = []
            otile = jnp.zeros((VGROUP, LANES), F32)
            for i in range(VGROUP):
                sk = colsum(state[i] * kk)
                sv = state[i] * w - sk * bvec + vt[i:i + 1, :] * km
                new.append(sv)
                otile = jnp.where(sub == i, colsum(sv * r), otile)
            o_ref[t, vbase:vbase + VGROUP, :] = otile
            return tuple(new)

        state = tuple(s_scr[vbase + i] for i in range(VGROUP))
        state = lax.fori_loop(0, tb, step, state, unroll=2)
        for i in range(VGROUP):
            s_scr[vbase + i] = state[i]

    sout_ref[...] = s_scr[...]


def _rwkv_scan(r, km, v, lw, kk, bb, s0, *, tb):
    nt, _, nl = r.shape
    seq = pl.BlockSpec((tb, HD_B, LANES), lambda g, t: (t, 0, g))
    st = pl.BlockSpec((HD_B, HD_B, LANES), lambda g, t: (0, 0, g))
    kern = functools.partial(_rwkv_scan_kernel, tb=tb)
    return pl.pallas_call(
        kern,
        grid=(nl // LANES, nt // tb),
        in_specs=[seq] * 6 + [st],
        out_specs=[seq, st],
        out_shape=[jax.ShapeDtypeStruct((nt, HD_B, nl), F32),
                   jax.ShapeDtypeStruct((HD_B, HD_B, nl), F32)],
        scratch_shapes=[pltpu.VMEM((HD_B, HD_B, LANES), F32)],
        compiler_params=_cparams(("parallel", "arbitrary")),
        name="rwkv_scan",
    )(r, km, v, lw, kk, bb, s0)


PAIR = 2 * HD_B
RWKV_SEQS_PER_STEP = 8


def _rwkv_chunk_kernel(r_ref, km_ref, v_ref, lw_ref, kk_ref, bb_ref, s0_ref, o_ref, sout_ref,
                       s_scr, *, chunk, nb):
    c = pl.program_id(1)

    @pl.when(c == 0)
    def _():
        s_scr[...] = s0_ref[...]

    lane = lax.broadcasted_iota(jnp.int32, (chunk, PAIR), 1)
    row = lax.broadcasted_iota(jnp.int32, (chunk, PAIR), 0)
    low_half = lane < HD_B
    li = jnp.where(low_half, lane, lane - HD_B)
    strict = li < row
    incl = li <= row
    eye = jnp.where(li == row, 1.0, 0.0)
    same = lambda n: (li // n) == (row // n)
    m8 = same(8)
    levels = []
    n = 16
    prev = m8
    while n <= chunk:
        cur = same(n)
        levels.append(jnp.logical_and(cur, jnp.logical_not(prev)))
        prev = cur
        n *= 2
    sq_r = lax.broadcasted_iota(jnp.int32, (PAIR, PAIR), 0)
    sq_c = lax.broadcasted_iota(jnp.int32, (PAIR, PAIR), 1)
    same_head = (sq_r < HD_B) == (sq_c < HD_B)
    low_bf = lax.broadcasted_iota(jnp.int32, (chunk, PAIR), 1) < HD_B

    def bd(s):
        return tuple(jnp.concatenate([jnp.where(low_bf, x, jnp.zeros_like(x)),
                                      jnp.where(low_bf, jnp.zeros_like(x), x)], axis=0)
                     for x in s)

    def pm1(x, y):
        return jnp.dot(x.astype(BF16), bd((y.astype(BF16),))[0], preferred_element_type=F32)

    chains = [(b, p) for b in range(nb) for p in range(H_B // 2)]
    each = lambda f: [f(i) for i in range(len(chains))]

    vv, lhs2, kg, bg, kdbd, etot, s_p = [], [], [], [], [], [], []
    for b, p in chains:
        sl = slice(p * PAIR, (p + 1) * PAIR)
        r = r_ref[b, :, sl]
        km = km_ref[b, :, sl]
        lw = lw_ref[b, :, sl]
        kk = kk_ref[b, :, sl]
        bb = bb_ref[b, :, sl]
        cum = lw
        d = 1
        while d < chunk:
            cum = cum + jnp.where(row >= d, pltpu.roll(cum, d, 0), 0.0)
            d *= 2
        tot = cum[chunk - 1:chunk, :]
        g_end = jnp.exp(tot - cum)
        g_inv = jnp.exp(-cum)
        vv.append(v_ref[b, :, sl])
        lhs2.append(_split(jnp.concatenate([kk * jnp.exp(cum - lw), r * jnp.exp(cum)], axis=0)))
        kg.append(bd(_split(km * g_inv)))
        bg.append(bd(_split(bb * g_inv)))
        kdbd.append(_split(jnp.concatenate([km * g_end, bb * g_end], axis=0)))
        etot.append(jnp.exp(tot))
        s_p.append(s_scr[b, p])

    ss = each(lambda i: lax.dot_general(lhs2[i][0], s_p[i].astype(BF16), (((1,), (1,)), ((), ())),
                                        preferred_element_type=F32))
    kb = each(lambda i: _mm(lhs2[i], tuple(jnp.concatenate([x, y], axis=0)
                                           for x, y in zip(kg[i], bg[i])), nt=True))
    a_b = each(lambda i: jnp.where(strict, kb[i][:chunk, PAIR:], 0.0))
    b_b = each(lambda i: jnp.where(incl, kb[i][chunk:, PAIR:], 0.0))
    av = each(lambda i: pm1(jnp.concatenate(
        [jnp.where(strict, kb[i][:chunk, :PAIR], 0.0), jnp.where(incl, kb[i][chunk:, :PAIR], 0.0)],
        axis=0), vv[i]))

    dg = each(lambda i: jnp.where(m8, a_b[i], 0.0))
    d2 = each(lambda i: pm1(dg[i], dg[i]))
    d3 = each(lambda i: pm1(dg[i], d2[i]))
    d4 = each(lambda i: pm1(d2[i], d2[i]))
    t1 = each(lambda i: eye - dg[i] + d2[i] - d3[i])
    inv = each(lambda i: t1[i] + pm1(t1[i], d4[i]))
    for m in levels:
        mlev = each(lambda i: pm1(inv[i], jnp.where(m, a_b[i], 0.0)))
        inv = each(lambda i: inv[i] - pm1(mlev[i], inv[i]))

    u = each(lambda i: pm1(inv[i], ss[i][:chunk] + av[i][:chunk]))
    upd = each(lambda i: _mm(_split(jnp.transpose(jnp.concatenate([vv[i], -u[i]], axis=0))),
                             kdbd[i]))
    bu = each(lambda i: pm1(b_b[i], u[i]))
    for i, (b, p) in enumerate(chains):
        sl = slice(p * PAIR, (p + 1) * PAIR)
        o_ref[b, :, sl] = ss[i][chunk:] + av[i][chunk:] - bu[i]
        s_scr[b, p] = s_p[i] * etot[i] + jnp.where(same_head, upd[i], 0.0)

    sout_ref[...] = s_scr[...]


def _rwkv_chunk(r, km, v, lw, kk, bb, s0, *, chunk, nb):
    nseq, t, _ = r.shape
    assert chunk == HD_B and nseq % nb == 0
    npair = H_B // 2
    row = pl.BlockSpec((nb, chunk, W_B), lambda b, c: (b, c, 0))
    st = pl.BlockSpec((nb, npair, PAIR, PAIR), lambda b, c: (b, 0, 0, 0))
    kern = functools.partial(_rwkv_chunk_kernel, chunk=chunk, nb=nb)
    return pl.pallas_call(
        kern,
        grid=(nseq // nb, t // chunk),
        in_specs=[row] * 6 + [st],
        out_specs=[row, st],
        out_shape=[jax.ShapeDtypeStruct((nseq, t, W_B), F32),
                   jax.ShapeDtypeStruct((nseq, npair, PAIR, PAIR), F32)],
        scratch_shapes=[pltpu.VMEM((nb, npair, PAIR, PAIR), F32)],
        compiler_params=_cparams(("parallel", "arbitrary")),
        name="rwkv_chunk",
    )(r, km, v, lw, kk, bb, s0)


MXU_N = 256


def _mix_residual(x_ref, oa_ref, o_ref, r_ref, km_ref, v_ref, g_ref, lnw_ref, lnb_ref, rk_ref,
                  wa_ref, wb_ref):
    ones = _head_ones(W_B)
    o = o_ref[...]
    cen = o - _head_sum(o, ones) * (1.0 / HD_B)
    head_sum1 = lambda z: jnp.dot(z.astype(BF16), ones, preferred_element_type=F32)
    var = head_sum1(cen * cen) * (1.0 / HD_B)
    on = cen * lax.rsqrt(var + GN_EPS) * lnw_ref[...] + lnb_ref[...]
    bonus = head_sum1(r_ref[...] * km_ref[...] * rk_ref[...]) * v_ref[...]
    ob = (on + bonus) * g_ref[...]
    return (x_ref[...]
            + jnp.dot(oa_ref[...].astype(BF16), wa_ref[...], preferred_element_type=F32)
            + jnp.dot(ob.astype(BF16), wb_ref[...], preferred_element_type=F32))


def _proj_residual(x_ref, o_ref, w_ref):
    return x_ref[...] + jnp.dot(o_ref[...].astype(BF16), w_ref[...],
                                preferred_element_type=F32)


def _res_ffn_kernel(*refs, residual, n_res, final_norm):
    g_ref, wg_ref, wu_ref, wd_ref, gf_ref, y_ref, xf_scr, x_scr, h_scr = refs[n_res:]
    x = residual(*refs[:n_res])
    xf_scr[...] = _rms(x, g_ref[...]).astype(BF16)
    x_scr[...] = x
    for j in range(wg_ref.shape[1] // MXU_N):
        cols = slice(j * MXU_N, (j + 1) * MXU_N)
        xf = xf_scr[...]
        hg = jnp.dot(xf, wg_ref[:, cols], preferred_element_type=F32)
        hu = jnp.dot(xf, wu_ref[:, cols], preferred_element_type=F32)
        h_scr[:, cols] = ((hg * _sigmoid(hg)) * hu).astype(BF16)
    y = x_scr[...] + jnp.dot(h_scr[...], wd_ref[...], preferred_element_type=F32)
    if final_norm:
        y = _rms(y, gf_ref[...])
    y_ref[...] = y


def _res_ffn(residual, row_ins, const_ins, g, wg, wu, wd, gf, *, tm, final_norm):
    n, d = row_ins[0].shape
    whole = lambda a: pl.BlockSpec(a.shape, lambda i: (0,) * a.ndim,
                                   pipeline_mode=pl.Buffered(1))
    consts = list(const_ins) + [g.reshape(1, d), wg, wu, wd, gf.reshape(1, d)]
    kern = functools.partial(_res_ffn_kernel, residual=residual,
                             n_res=len(row_ins) + len(const_ins), final_norm=final_norm)
    return pl.pallas_call(
        kern,
        grid=(n // tm,),
        in_specs=[pl.BlockSpec((tm, a.shape[1]), lambda i: (i, 0)) for a in row_ins]
                 + [whole(a) for a in consts],
        out_specs=pl.BlockSpec((tm, d), lambda i: (i, 0)),
        out_shape=jax.ShapeDtypeStruct((n, d), F32),
        scratch_shapes=[pltpu.VMEM((tm, d), BF16), pltpu.VMEM((tm, d), F32),
                        pltpu.VMEM((tm, wg.shape[1]), BF16)],
        compiler_params=_cparams(("parallel",)),
        name="res_ffn",
    )(*row_ins, *consts)


HGRN_SAFE_LOG = 80.0


def _hgrn_kernel(q_ref, f_ref, i_ref, g_ref, s0_ref, lbc_ref, gn_ref, o_ref, sout_ref,
                 s_scr, att_scr, *, layer, chunk, valid, nb):
    c = pl.program_id(1)

    @pl.when(c == 0)
    def _():
        s_scr[...] = s0_ref[...]

    lbc = lbc_ref[...]
    e = jnp.exp(lbc - jnp.max(lbc, axis=0, keepdims=True))
    sm = e / jnp.sum(e, axis=0, keepdims=True)
    lbf = sm[1:2, :]
    for l in range(2, layer + 1):
        lbf = lbf + sm[l:l + 1, :]

    nblk = chunk // SUBLANES
    sub = lax.broadcasted_iota(jnp.int32, (SUBLANES, DK_C), 0)
    lane_c = lax.broadcasted_iota(jnp.int32, (SUBLANES, chunk), 1)
    rowc = lax.broadcasted_iota(jnp.int32, (chunk, F_C), 0)
    tril = jnp.where(lax.broadcasted_iota(jnp.int32, (chunk, chunk), 0)
                     >= lax.broadcasted_iota(jnp.int32, (chunk, chunk), 1), 1.0, 0.0).astype(BF16)

    qs, ks, bs, ivs = [], [], [], []
    for b in range(nb):
        fr = f_ref[b]
        ez = jnp.exp(-jnp.abs(fr))
        rz = 1.0 / (1.0 + ez)
        sig = jnp.where(fr >= 0.0, rz, ez * rz)
        nsig = jnp.where(fr >= 0.0, ez * rz, rz)
        logf = jnp.log(lbf + (1.0 - lbf) * sig)
        kfull = (1.0 - lbf) * nsig
        if valid < chunk:
            logf = jnp.where(rowc < valid, logf, 0.0)
            kfull = jnp.where(rowc < valid, kfull, 0.0)
        hi, lo = _split(logf)
        bc = jnp.dot(jnp.concatenate([tril, tril], axis=1), jnp.concatenate([hi, lo], axis=0),
                     preferred_element_type=F32)
        qs.append(jax.nn.silu(q_ref[b]))
        ks.append(kfull)
        bs.append(bc)
        ivs.append(i_ref[b].astype(BF16))

    chains = [(b, h) for b in range(nb) for h in range(H_C)]
    each = lambda f: [f(i) for i in range(len(chains))]
    hsl = lambda h: slice(h * DK_C, (h + 1) * DK_C)
    q = each(lambda i: qs[chains[i][0]][:, hsl(chains[i][1])])
    k = each(lambda i: ks[chains[i][0]][:, hsl(chains[i][1])])
    bcs = each(lambda i: bs[chains[i][0]][:, hsl(chains[i][1])])
    iv = each(lambda i: ivs[chains[i][0]][:, hsl(chains[i][1])])
    s_h = each(lambda i: s_scr[chains[i][0], chains[i][1]])

    def off_diag(i, tb):
        lo = tb * SUBLANES
        rs = slice(lo, lo + SUBLANES)
        edge = bcs[i][lo - 1:lo, :]
        qi = q[i][rs, :] * jnp.exp(bcs[i][rs, :] - edge)
        ki = k[i][0:lo, :] * jnp.exp(edge - bcs[i][0:lo, :])
        ki = jnp.concatenate([ki, jnp.zeros((chunk - lo, DK_C), F32)], axis=0)
        return lax.dot_general(qi.astype(BF16), ki.astype(BF16), (((1,), (1,)), ((), ())),
                               preferred_element_type=F32)

    inter = each(lambda i: jnp.dot((q[i] * jnp.exp(bcs[i])).astype(BF16), s_h[i].astype(BF16),
                                   preferred_element_type=F32))


    sub_c = lax.broadcasted_iota(jnp.int32, (SUBLANES, chunk), 0)

    def att_on_matrix_unit():
        def block(i, tb):
            lo = tb * SUBLANES
            rs = slice(lo, lo + SUBLANES)
            dloc = bcs[i][rs, :] if tb == 0 else bcs[i][rs, :] - bcs[i][lo - 1:lo, :]
            qi = q[i][rs, :] * jnp.exp(dloc)
            parts = [k[i][rs, :] * jnp.exp(-dloc)]
            if tb > 0:
                parts.insert(0, k[i][0:lo, :] * jnp.exp(bcs[i][lo - 1:lo, :] - bcs[i][0:lo, :]))
            if chunk - lo - SUBLANES > 0:
                parts.append(jnp.zeros((chunk - lo - SUBLANES, DK_C), F32))
            ki = jnp.concatenate(parts, axis=0) if len(parts) > 1 else parts[0]
            return lax.dot_general(qi.astype(BF16), ki.astype(BF16), (((1,), (1,)), ((), ())),
                                   preferred_element_type=F32)

        blks = each(lambda i: [block(i, tb) for tb in range(nblk)])
        for i in range(len(chains)):
            for tb in range(nblk):
                att_scr[i, tb * SUBLANES:(tb + 1) * SUBLANES, :] = jnp.where(
                    lane_c <= sub_c + tb * SUBLANES, blks[i][tb], 0.0)

    def att_pairwise_diagonal():
        offd = each(lambda i: [off_diag(i, tb) for tb in range(1, nblk)])
        for i in range(len(chains)):
            for tb in range(nblk):
                rs = slice(tb * SUBLANES, (tb + 1) * SUBLANES)
                blk = jnp.zeros((SUBLANES, chunk), F32) if tb == 0 else offd[i][tb - 1]
                for j in range(SUBLANES):
                    s = tb * SUBLANES + j
                    if s >= valid:
                        break
                    diff = jnp.where(sub >= j, bcs[i][rs, :] - bcs[i][s:s + 1, :], -jnp.inf)
                    ev = jnp.exp(diff) * (q[i][rs, :] * k[i][s:s + 1, :])
                    blk = jnp.where(lane_c == s, jnp.sum(ev, axis=1, keepdims=True), blk)
                att_scr[i, rs, :] = blk

    def new_state(i):
        bl = bcs[i][chunk - 1:chunk, :]
        kd = k[i] * jnp.exp(bl - bcs[i])
        pieces = [kd, jnp.broadcast_to(jnp.exp(bl), (SUBLANES, DK_C))]
        fill = DK_C - chunk - SUBLANES
        if fill > 0:
            pieces.append(jnp.zeros((fill, DK_C), F32))
        xt = jnp.transpose(jnp.concatenate(pieces, axis=0))
        return (xt[:, chunk:chunk + 1] * s_h[i]
                + jnp.dot(xt[:, 0:chunk].astype(BF16), iv[i], preferred_element_type=F32))

    if nblk > 1:
        att_on_matrix_unit()
        worst = jnp.float32(0.0)
        for b in range(nb):
            win = bs[b] - jnp.where(rowc >= SUBLANES, pltpu.roll(bs[b], SUBLANES, 0), 0.0)
            worst = jnp.minimum(worst, jnp.min(win))
        pl.when(worst < -HGRN_SAFE_LOG)(att_pairwise_diagonal)
    else:
        att_pairwise_diagonal()

    o_h = each(lambda i: jnp.dot(att_scr[i].astype(BF16), iv[i], preferred_element_type=F32)
               + inter[i])
    s_new = each(new_state)
    for i, (b, h) in enumerate(chains):
        s_scr[b, h] = s_new[i]
    for b in range(nb):
        o = jnp.concatenate([o_h[b * H_C + h] for h in range(H_C)], axis=1)
        o_ref[b] = _rms(o, gn_ref[...]) * jax.nn.silu(g_ref[b])
    sout_ref[...] = s_scr[...]


def _hgrn(pc3, s0, lbc, gn, *, layer, chunk, valid, nb):
    nseq, t, _ = pc3.shape
    assert nseq % nb == 0
    col = lambda j: pl.BlockSpec((nb, chunk, F_C), lambda b, c, j=j: (b, c, j))
    st = pl.BlockSpec((nb, H_C, DK_C, DK_C), lambda b, c: (b, 0, 0, 0))
    kern = functools.partial(_hgrn_kernel, layer=layer, chunk=chunk, valid=valid, nb=nb)
    return pl.pallas_call(
        kern,
        grid=(nseq // nb, t // chunk),
        in_specs=[col(0), col(1), col(2), col(3), st,
                  pl.BlockSpec(lbc.shape, lambda b, c: (0, 0)),
                  pl.BlockSpec((1, D_MODEL), lambda b, c: (0, 0))],
        out_specs=[pl.BlockSpec((nb, chunk, D_MODEL), lambda b, c: (b, c, 0)), st],
        out_shape=[jax.ShapeDtypeStruct((nseq, t, D_MODEL), F32),
                   jax.ShapeDtypeStruct(s0.shape, F32)],
        scratch_shapes=[pltpu.VMEM((nb, H_C, DK_C, DK_C), F32),
                        pltpu.VMEM((nb * H_C, chunk, chunk), F32)],
        compiler_params=_cparams(("parallel", "arbitrary")),
        name="hgrn",
    )(pc3, pc3, pc3, pc3, s0, lbc, gn.reshape(1, D_MODEL))


def _block_diag(w):
    h, n, _ = w.shape
    eye = jnp.eye(h, dtype=w.dtype)
    return (eye[:, None, :, None] * w[:, :, None, :]).reshape(h * n, h * n)


ROW_TILE = 512
HGRN_CHUNK = 64
HGRN_SEQS_PER_STEP = 4


def _trunk(x2d, conv_st, h_st, shift_st, rs_st, hs_st, P, *, nbatch, nsteps, time_major):
    n = x2d.shape[0]
    if time_major:
        nseq, step, rows = 1, nbatch, n
        tm = n
    else:
        nseq, step, rows = nbatch, 1, ROW_TILE
        tm = ROW_TILE
    nchain = nbatch * H_B

    out_a, conv_new, h_new, r, km, v, lw, kk, bb, g, shift_new = _front(
        x2d, P['ln_mix'][0], P['w_in_ab'],
        conv_st, h_st, P['conv_w'], P['conv_b'], P['wr'], P['wi'], P['gr_b'], P['gi_b'],
        P['lru_lambda'],
        shift_st, P['mu_b'], P['w0_b'], P['a0_b'], P['w2p'], P['a2p'], P['g2_b'],
        P['kk_b'], P['ka_b'], nseq=nseq, step=step, rows=rows)

    if time_major:
        def to_lanes(z):
            z = z.reshape(nsteps, nbatch, H_B, HD_B).transpose(0, 3, 1, 2)
            return z.reshape(nsteps, HD_B, nchain)

        s0 = rs_st.transpose(2, 3, 0, 1).reshape(HD_B, HD_B, nchain)
        o, s_new = _rwkv_scan(to_lanes(r), to_lanes(km), to_lanes(v), to_lanes(lw), to_lanes(kk),
                              to_lanes(bb), s0, tb=nsteps)
        o = o.reshape(nsteps, HD_B, nbatch, H_B).transpose(0, 2, 3, 1).reshape(n, W_B)
        rs_new = s_new.reshape(HD_B, HD_B, nbatch, H_B).transpose(2, 3, 0, 1)
    else:
        z = jnp.zeros_like(rs_st[:, 0::2])
        s0 = jnp.concatenate([jnp.concatenate([rs_st[:, 0::2], z], axis=-1),
                              jnp.concatenate([z, rs_st[:, 1::2]], axis=-1)], axis=-2)
        seq3 = lambda y: y.reshape(nbatch, nsteps, W_B)
        o, s_new = _rwkv_chunk(seq3(r), seq3(km), seq3(v), seq3(lw), seq3(kk), seq3(bb), s0,
                               chunk=HD_B, nb=RWKV_SEQS_PER_STEP)
        o = o.reshape(n, W_B)
        rs_new = jnp.stack([s_new[:, :, :HD_B, :HD_B], s_new[:, :, HD_B:, HD_B:]],
                           axis=2).reshape(nbatch, H_B, HD_B, HD_B)

    vec = lambda p: p.reshape(1, W_B)
    x2 = _res_ffn(_mix_residual, [x2d, out_a, o, r, km, v, g],
                  [vec(P['lnx_w']), vec(P['lnx_b']), vec(P['rk_b']), P['w_out_a'], P['w_out_b']],
                  P['ln_ffn'][0], P['ffn_gate'][0], P['ffn_up'][0], P['ffn_down'][0],
                  P['ln_final'], tm=tm, final_norm=False)

    pc = _norm_matmul(x2, P['ln_mix'][1], P['w_in_c'], tm)
    if time_major:
        pc3 = pc.reshape(nsteps, nbatch, 4 * F_C).transpose(1, 0, 2)
        chunk = SUBLANES
        pc3 = jnp.pad(pc3, ((0, 0), (0, chunk - nsteps), (0, 0)))
    else:
        pc3 = pc.reshape(nbatch, nsteps, 4 * F_C)
        chunk = HGRN_CHUNK
    o3, hs_new = _hgrn(pc3, hs_st, P['lb_c'], P['gn_c'], layer=1, chunk=chunk,
                       valid=min(chunk, nsteps), nb=HGRN_SEQS_PER_STEP)
    if time_major:
        o = o3[:, :nsteps].transpose(1, 0, 2).reshape(n, D_MODEL)
    else:
        o = o3.reshape(n, D_MODEL)
    y = _res_ffn(_proj_residual, [x2, o], [P['w_out_c']],
                 P['ln_ffn'][1], P['ffn_gate'][1], P['ffn_up'][1], P['ffn_down'][1],
                 P['ln_final'], tm=tm, final_norm=True)
    return y, conv_new, h_new, shift_new, rs_new, hs_new


def kernel(x_prompt, x_sample, state_rglru_conv, state_rglru_h, state_rwkv_shift, state_rwkv_S,
           state_hgrn_S, ln_mix, ln_ffn, ln_final, w_in_ab, conv_w, conv_b, gr_w, gr_b, gi_w,
           gi_b, lru_lambda, mu_b, w0_b, w2_b, a0_b, a2_b, g2_b, kk_b, ka_b, rk_b, lnx_w, lnx_b,
           w_out_ab, w_in_c, lb_c, gn_c, w_out_c, ffn_gate, ffn_up, ffn_down):
    bp, tp, _ = x_prompt.shape
    bs, ts, _ = x_sample.shape
    zpad_w = jnp.zeros((LORA_A, W_B), F32)
    zpad_a = jnp.zeros((LORA_W, W_B), F32)
    P = dict(
        ln_mix=ln_mix, ln_ffn=ln_ffn, ln_final=ln_final,
        w_in_ab=w_in_ab[0], conv_w=conv_w[0], conv_b=conv_b[0],
        wr=_block_diag(gr_w[0]).astype(BF16), wi=_block_diag(gi_w[0]).astype(BF16),
        gr_b=gr_b[0], gi_b=gi_b[0], lru_lambda=lru_lambda[0], mu_b=mu_b[0], w0_b=w0_b[0],
        a0_b=a0_b[0],
        w2p=jnp.concatenate([w2_b[0], zpad_w], axis=0).astype(BF16),
        a2p=jnp.concatenate([zpad_a, a2_b[0]], axis=0).astype(BF16),
        g2_b=g2_b[0].astype(BF16), kk_b=kk_b[0], ka_b=ka_b[0], rk_b=rk_b[0],
        lnx_w=lnx_w[0], lnx_b=lnx_b[0],
        w_out_a=w_out_ab[0, :W_A].astype(BF16), w_out_b=w_out_ab[0, W_A:].astype(BF16),
        w_in_c=w_in_c[0], lb_c=lb_c, gn_c=gn_c[0],
        w_out_c=w_out_c[0].astype(BF16), ffn_gate=ffn_gate.astype(BF16),
        ffn_up=ffn_up.astype(BF16), ffn_down=ffn_down.astype(BF16))

    yp, p_conv, p_h, p_shift, p_rs, p_hs = _trunk(
        x_prompt.reshape(bp * tp, D_MODEL),
        jnp.zeros((bp, CONV_W - 1, W_A), F32), jnp.zeros((bp, 1, W_A), F32),
        jnp.zeros((bp, 1, P_B), F32), jnp.zeros((bp, H_B, HD_B, HD_B), F32),
        jnp.zeros((bp, H_C, DK_C, DK_C), F32), P, nbatch=bp, nsteps=tp, time_major=False)

    ys, s_conv, s_h, s_shift, s_rs, s_hs = _trunk(
        x_sample.transpose(1, 0, 2).reshape(ts * bs, D_MODEL),
        state_rglru_conv[0].transpose(1, 0, 2).reshape(1, (CONV_W - 1) * bs, W_A),
        state_rglru_h[0].reshape(1, bs, W_A), state_rwkv_shift[0].reshape(1, bs, P_B),
        state_rwkv_S[0], state_hgrn_S[0], P, nbatch=bs, nsteps=ts, time_major=True)

    return (yp.reshape(bp, tp, D_MODEL),
            ys.reshape(ts, bs, D_MODEL).transpose(1, 0, 2),
            p_conv[None], p_h.reshape(1, bp, W_A), p_shift.reshape(1, bp, P_B), p_rs[None],
            p_hs[None],
            s_conv.reshape(CONV_W - 1, bs, W_A).transpose(1, 0, 2)[None],
            s_h.reshape(1, bs, W_A), s_shift.reshape(1, bs, P_B), s_rs[None], s_hs[None])
```

```python
import functools

import jax
import jax.numpy as jnp
from jax import lax
from jax.experimental import pallas as pl
from jax.experimental.pallas import tpu as pltpu

F32 = jnp.float32
BF16 = jnp.bfloat16

D_MODEL = 1024
W_A = 512
H_A = 8
CONV_W = 4
LRU_C = 8.0
W_B = 512
HD_B = 64
H_B = 8
LORA_W = 64
LORA_A = 64
LORA_G = 128
P_A = 2 * W_A
P_B = 3 * W_B + LORA_W + LORA_A + LORA_G
DK_C = 128
H_C = 8
F_C = 1024
D_FF = 2816
RMS_EPS = 1e-6
GN_EPS = 64e-5

LANES = 128
SUBLANES = 8
VMEM_LIMIT = 56 * 1024 * 1024


def _cparams(sem):
    return pltpu.CompilerParams(dimension_semantics=sem, vmem_limit_bytes=VMEM_LIMIT)


def _softplus(x):
    return jnp.maximum(x, 0.0) + jnp.log1p(jnp.exp(-jnp.abs(x)))

def _sigmoid(x):
    return jax.nn.sigmoid(x)


def _rms(x, g):
    ms = jnp.mean(x * x, axis=-1, keepdims=True)
    return x * lax.rsqrt(ms + RMS_EPS) * g


def _split(x):
    hi = x.astype(BF16)
    lo = (x - hi.astype(F32)).astype(BF16)
    return hi, lo


def _mm(a, b, nt=False):
    ah, al = a
    bh, bl = b
    lhs = jnp.concatenate([ah, al, ah], axis=1)
    if nt:
        rhs = jnp.concatenate([bh, bh, bl], axis=1)
        return lax.dot_general(lhs, rhs, (((1,), (1,)), ((), ())), preferred_element_type=F32)
    rhs = jnp.concatenate([bh, bh, bl], axis=0)
    return jnp.dot(lhs, rhs, preferred_element_type=F32)


def _head_ones(width):
    r = lax.broadcasted_iota(jnp.int32, (width, width), 0) // HD_B
    c = lax.broadcasted_iota(jnp.int32, (width, width), 1) // HD_B
    return jnp.where(r == c, 1.0, 0.0).astype(BF16)


def _head_sum(x, ones):
    hi, lo = _split(x)
    return jnp.dot(jnp.concatenate([hi, lo], axis=1), jnp.concatenate([ones, ones], axis=0),
                   preferred_element_type=F32)


def _norm_matmul_kernel(x_ref, g_ref, w_ref, o_ref, w_bf):
    @pl.when(pl.program_id(0) == 0)
    def _():
        w_bf[...] = w_ref[...].astype(BF16)

    xn = _rms(x_ref[...], g_ref[...]).astype(BF16)
    o_ref[...] = jnp.dot(xn, w_bf[...], preferred_element_type=F32)


def _norm_matmul(x, g, w, tm):
    n, d = x.shape
    p = w.shape[1]
    assert n % tm == 0
    return pl.pallas_call(
        _norm_matmul_kernel,
        grid=(n // tm,),
        in_specs=[pl.BlockSpec((tm, d), lambda i: (i, 0)),
                  pl.BlockSpec((1, d), lambda i: (0, 0)),
                  pl.BlockSpec((d, p), lambda i: (0, 0), pipeline_mode=pl.Buffered(1))],
        out_specs=pl.BlockSpec((tm, p), lambda i: (i, 0)),
        out_shape=jax.ShapeDtypeStruct((n, p), F32),
        scratch_shapes=[pltpu.VMEM((d, p), BF16)],
        compiler_params=_cparams(("arbitrary",)),
        name="norm_matmul",
    )(x, g.reshape(1, d), w)


def _rglru_body(u_in, gate, cw_ref, cb_ref, wr_ref, wi_ref, br_ref, bi_ref, lam_ref,
                out_ref, cnew_ref, hnew_ref, xbuf, hcar, bbuf, *, step, rows, pad):
    hist = (CONV_W - 1) * step

    if step == 1:
        u = u_in
        prev = xbuf[0:SUBLANES, :]
        sub8 = lax.broadcasted_iota(jnp.int32, (SUBLANES, W_A), 0)
        conv = cb_ref[...]
        for j in range(CONV_W):
            d = CONV_W - 1 - j
            if d == 0:
                ush = u
            else:
                rolled = pltpu.roll(u, d, 0)
                head = jnp.where(sub8 < d, pltpu.roll(prev, d, 0), rolled[0:SUBLANES, :])
                ush = jnp.concatenate([head, rolled[SUBLANES:, :]], axis=0)
            conv = conv + ush * cw_ref[j:j + 1, :]
        cnew_ref[0] = u[rows - hist:rows, :]
        xbuf[0:SUBLANES, :] = u[rows - SUBLANES:rows, :]
    else:
        xbuf[pad:pad + rows, :] = u_in
        conv = cb_ref[...] + xbuf[pad - hist:pad - hist + rows, :] * cw_ref[0:1, :]
        for j in range(1, CONV_W):
            o = pad - hist + j * step
            conv = conv + xbuf[o:o + rows, :] * cw_ref[j:j + 1, :]
        tail = xbuf[pad + rows - hist:pad + rows, :]
        cnew_ref[0] = tail
        xbuf[pad - hist:pad, :] = tail

    ub = conv.astype(BF16)
    r = _sigmoid(jnp.dot(ub, wr_ref[...], preferred_element_type=F32) + br_ref[...])
    ig = _sigmoid(jnp.dot(ub, wi_ref[...], preferred_element_type=F32) + bi_ref[...])
    log_a = (-LRU_C) * r * _softplus(-lam_ref[...])
    a = jnp.exp(log_a)
    bterm = jnp.sqrt(-jnp.tanh(log_a) * (a * a + 1.0)) * (ig * conv)
    bbuf[...] = bterm
    bbuf[0:step, :] = bterm[0:step, :] + a[0:step, :] * hcar[...]
    bv = bbuf[...]

    def scan_levels(a, bv, pos, first, count):
        d = first
        while d < first * count:
            m = pos >= d
            bv = jnp.where(m, a * pltpu.roll(bv, d, 0) + bv, bv)
            a = jnp.where(m, a * pltpu.roll(a, d, 0), a)
            d *= 2
        return a, bv

    row = lax.broadcasted_iota(jnp.int32, (rows, W_A), 0)
    a, bv = scan_levels(a, bv, row, step, rows // step)
    hcar[...] = bv[rows - step:rows, :]
    hnew_ref[0] = bv[rows - step:rows, :]
    out_ref[...] = jax.nn.gelu(gate) * bv


def _rwkv_prep_body(pf, mu_ref, w0_ref, a0_ref, w2_ref, a2_ref, g2_ref, kkp_ref, kap_ref,
                    r_ref, km_ref, v_ref, lw_ref, kk_ref, bb_ref, g_ref, snew_ref, pbuf,
                    *, step, rows, pad):

    if step == 1:
        rolled = pltpu.roll(pf, 1, 0)
        sub8 = lax.broadcasted_iota(jnp.int32, (SUBLANES, P_B), 0)
        head = jnp.where(sub8 < 1, pltpu.roll(pbuf[0:SUBLANES, :], 1, 0), rolled[0:SUBLANES, :])
        shifted = jnp.concatenate([head, rolled[SUBLANES:, :]], axis=0)
        pbuf[0:SUBLANES, :] = pf[rows - SUBLANES:rows, :]
    else:
        pbuf[pad:pad + rows, :] = pf
        shifted = pbuf[pad - step:pad - step + rows, :]
        pbuf[pad - step:pad, :] = pf[rows - step:rows, :]
    snew_ref[0] = pf[rows - step:rows, :]

    m = pf + (shifted - pf) * mu_ref[...]
    o1 = 3 * W_B
    r_ref[...] = m[:, 0:W_B]
    kraw = m[:, W_B:2 * W_B]
    v_ref[...] = m[:, 2 * W_B:o1]
    xwa = m[:, o1:o1 + LORA_W + LORA_A]
    xg = m[:, o1 + LORA_W + LORA_A:]
    lane = lax.broadcasted_iota(jnp.int32, xwa.shape, 1)
    lhs = jnp.where(lane < LORA_W, jnp.tanh(xwa), xwa).astype(BF16)
    lw = jnp.dot(lhs, w2_ref[...], preferred_element_type=F32)
    la = jnp.dot(lhs, a2_ref[...], preferred_element_type=F32)
    w_log = -_softplus(-(w0_ref[...] + lw)) - 0.5
    lw_ref[...] = -jnp.exp(w_log)
    a = _sigmoid(a0_ref[...] + la)
    g_ref[...] = jnp.dot(_sigmoid(xg).astype(BF16), g2_ref[...], preferred_element_type=F32)
    kk = kraw * kkp_ref[...]
    kk = kk * lax.rsqrt(jnp.maximum(_head_sum(kk * kk, _head_ones(W_B)), 1e-24))
    kk_ref[...] = kk
    bb_ref[...] = kk * a
    km_ref[...] = kraw * (1.0 + (a - 1.0) * kap_ref[...])


N_RGLRU_IN = 9
N_PREP_IN = 9


def _front_kernel(*refs, step, rows, pad_a, pad_b, nsplit):
    x_ref, gn_ref, w_ref = refs[:3]
    rg_in = refs[3:3 + N_RGLRU_IN]
    pp_in = refs[3 + N_RGLRU_IN:3 + N_RGLRU_IN + N_PREP_IN]
    outs = refs[3 + N_RGLRU_IN + N_PREP_IN:]
    rg_out, pp_out = outs[:3], outs[3:11]
    xbuf, hcar, bbuf, pbuf, p_scr, w_bf = outs[11:]
    part = rows // nsplit

    @pl.when(pl.program_id(1) == 0)
    def _():
        hist = (CONV_W - 1) * step
        xbuf[pad_a - hist:pad_a, :] = rg_in[0][0]
        hcar[...] = rg_in[1][0]
        pbuf[pad_b - step:pad_b, :] = pp_in[0][0]
        w_bf[...] = w_ref[...].astype(BF16)

    xn = _rms(x_ref[...], gn_ref[...]).astype(BF16)
    for s in range(nsplit):
        p_scr[s] = jnp.dot(xn[s * part:(s + 1) * part, :], w_bf[...],
                           preferred_element_type=F32)
    for s in range(nsplit):
        sub = lambda ref, s=s: ref.at[s * part:(s + 1) * part, :]
        _rglru_body(p_scr[s, :, 0:W_A], p_scr[s, :, W_A:P_A], *rg_in[2:],
                    sub(rg_out[0]), rg_out[1], rg_out[2], xbuf, hcar, bbuf,
                    step=step, rows=part, pad=pad_a)
        _rwkv_prep_body(p_scr[s, :, P_A:], *pp_in[1:], *[sub(r) for r in pp_out[:7]],
                        pp_out[7], pbuf, step=step, rows=part, pad=pad_b)


def _front(x, gn, w, conv_st, h_st, cw, cb, wr, wi, br, bi, lam,
           prev, mu, w0, a0, w2p, a2p, g2, kkp, kap, *, nseq, step, rows):
    n, d = x.shape
    nt = n // (nseq * rows)
    hist = (CONV_W - 1) * step
    pad_a = max(SUBLANES, hist)
    pad_b = max(SUBLANES, step)
    row_map = lambda b, t: (b * nt + t, 0)
    seq_map = lambda b, t: (b, 0, 0)
    whole = lambda a: pl.BlockSpec(a.shape, lambda b, t: (0,) * a.ndim,
                                   pipeline_mode=pl.Buffered(1))
    va = lambda p: p.reshape(1, W_A)
    vb = lambda p: p.reshape(1, W_B)
    consts_a = [cw, va(cb), wr, wi, va(br), va(bi), va(lam)]
    consts_b = [mu.reshape(1, P_B), vb(w0), vb(a0), w2p, a2p, g2, vb(kkp), vb(kap)]
    nsplit = 2 if step == 1 else 1
    part = rows // nsplit
    kern = functools.partial(_front_kernel, step=step, rows=rows, pad_a=pad_a, pad_b=pad_b,
                             nsplit=nsplit)
    tile = lambda wd: pl.BlockSpec((rows, wd), row_map)
    return pl.pallas_call(
        kern,
        grid=(nseq, nt),
        in_specs=[tile(d), whole(gn.reshape(1, d)), whole(w),
                  pl.BlockSpec((1, hist, W_A), seq_map), pl.BlockSpec((1, step, W_A), seq_map)]
                 + [whole(a) for a in consts_a]
                 + [pl.BlockSpec((1, step, P_B), seq_map)] + [whole(a) for a in consts_b],
        out_specs=[tile(W_A), pl.BlockSpec((1, hist, W_A), seq_map),
                   pl.BlockSpec((1, step, W_A), seq_map)]
                  + [tile(W_B)] * 7 + [pl.BlockSpec((1, step, P_B), seq_map)],
        out_shape=[jax.ShapeDtypeStruct((n, W_A), F32),
                   jax.ShapeDtypeStruct((nseq, hist, W_A), F32),
                   jax.ShapeDtypeStruct((nseq, step, W_A), F32)]
                  + [jax.ShapeDtypeStruct((n, W_B), F32)] * 7
                  + [jax.ShapeDtypeStruct((nseq, step, P_B), F32)],
        scratch_shapes=[pltpu.VMEM((pad_a + part, W_A), F32),
                        pltpu.VMEM((step, W_A), F32),
                        pltpu.VMEM((part, W_A), F32),
                        pltpu.VMEM((pad_b + part, P_B), F32),
                        pltpu.VMEM((nsplit, part, P_A + P_B), F32),
                        pltpu.VMEM(w.shape, BF16)],
        compiler_params=_cparams(("parallel", "arbitrary")),
        name="front",
    )(x, gn.reshape(1, d), w, conv_st, h_st, *consts_a, prev, *consts_b)


VGROUP = 4


def _rwkv_scan_kernel(r_ref, km_ref, v_ref, lw_ref, kk_ref, bb_ref, s0_ref, o_ref, sout_ref,
                      s_scr, *, tb):
    tblk = pl.program_id(1)

    @pl.when(tblk == 0)
    def _():
        s_scr[...] = s0_ref[...]

    def colsum(x):
        return jnp.sum(x, axis=0, keepdims=True)

    sub = lax.broadcasted_iota(jnp.int32, (VGROUP, LANES), 0)
    for g in range(HD_B // VGROUP):
        vbase = g * VGROUP

        def step(t, state, vbase=vbase):
            kk = kk_ref[t]
            w = jnp.exp(lw_ref[t])
            bvec = bb_ref[t]
            km = km_ref[t]
            r = r_ref[t]
            vt = v_ref[t, vbase:vbase + VGROUP, :]
            new = []
            otile = jnp.zeros((VGROUP, LANES), F32)
            for i in range(VGROUP):
                sk = colsum(state[i] * kk)
                sv = state[i] * w - sk * bvec + vt[i:i + 1, :] * km
                new.append(sv)
                otile = jnp.where(sub == i, colsum(sv * r), otile)
            o_ref[t, vbase:vbase + VGROUP, :] = otile
            return tuple(new)

        state = tuple(s_scr[vbase + i] for i in range(VGROUP))
        state = lax.fori_loop(0, tb, step, state, unroll=2)
        for i in range(VGROUP):
            s_scr[vbase + i] = state[i]

    sout_ref[...] = s_scr[...]


def _rwkv_scan(r, km, v, lw, kk, bb, s0, *, tb):
    nt, _, nl = r.shape
    seq = pl.BlockSpec((tb, HD_B, LANES), lambda g, t: (t, 0, g))
    st = pl.BlockSpec((HD_B, HD_B, LANES), lambda g, t: (0, 0, g))
    kern = functools.partial(_rwkv_scan_kernel, tb=tb)
    return pl.pallas_call(
        kern,
        grid=(nl // LANES, nt // tb),
        in_specs=[seq] * 6 + [st],
        out_specs=[seq, st],
        out_shape=[jax.ShapeDtypeStruct((nt, HD_B, nl), F32),
                   jax.ShapeDtypeStruct((HD_B, HD_B, nl), F32)],
        scratch_shapes=[pltpu.VMEM((HD_B, HD_B, LANES), F32)],
        compiler_params=_cparams(("parallel", "arbitrary")),
        name="rwkv_scan",
    )(r, km, v, lw, kk, bb, s0)


PAIR = 2 * HD_B
RWKV_SEQS_PER_STEP = 8


def _rwkv_chunk_kernel(r_ref, km_ref, v_ref, lw_ref, kk_ref, bb_ref, s0_ref, o_ref, sout_ref,
                       s_scr, *, chunk, nb):
    c = pl.program_id(1)

    @pl.when(c == 0)
    def _():
        s_scr[...] = s0_ref[...]

    lane = lax.broadcasted_iota(jnp.int32, (chunk, PAIR), 1)
    row = lax.broadcasted_iota(jnp.int32, (chunk, PAIR), 0)
    low_half = lane < HD_B
    li = jnp.where(low_half, lane, lane - HD_B)
    strict = li < row
    incl = li <= row
    eye = jnp.where(li == row, 1.0, 0.0)
    same = lambda n: (li // n) == (row // n)
    m8 = same(8)
    levels = []
    n = 16
    prev = m8
    while n <= chunk:
        cur = same(n)
        levels.append(jnp.logical_and(cur, jnp.logical_not(prev)))
        prev = cur
        n *= 2
    sq_r = lax.broadcasted_iota(jnp.int32, (PAIR, PAIR), 0)
    sq_c = lax.broadcasted_iota(jnp.int32, (PAIR, PAIR), 1)
    same_head = (sq_r < HD_B) == (sq_c < HD_B)
    low_bf = lax.broadcasted_iota(jnp.int32, (chunk, PAIR), 1) < HD_B

    def bd(s):
        return tuple(jnp.concatenate([jnp.where(low_bf, x, jnp.zeros_like(x)),
                                      jnp.where(low_bf, jnp.zeros_like(x), x)], axis=0)
                     for x in s)

    def pm1(x, y):
        return jnp.dot(x.astype(BF16), bd((y.astype(BF16),))[0], preferred_element_type=F32)

    chains = [(b, p) for b in range(nb) for p in range(H_B // 2)]
    each = lambda f: [f(i) for i in range(len(chains))]

    vv, lhs2, kg, bg, kdbd, etot, s_p = [], [], [], [], [], [], []
    for b, p in chains:
        sl = slice(p * PAIR, (p + 1) * PAIR)
        r = r_ref[b, :, sl]
        km = km_ref[b, :, sl]
        lw = lw_ref[b, :, sl]
        kk = kk_ref[b, :, sl]
        bb = bb_ref[b, :, sl]
        cum = lw
        d = 1
        while d < chunk:
            cum = cum + jnp.where(row >= d, pltpu.roll(cum, d, 0), 0.0)
            d *= 2
        tot = cum[chunk - 1:chunk, :]
        g_end = jnp.exp(tot - cum)
        g_inv = jnp.exp(-cum)
        vv.append(v_ref[b, :, sl])
        lhs2.append(_split(jnp.concatenate([kk * jnp.exp(cum - lw), r * jnp.exp(cum)], axis=0)))
        kg.append(bd(_split(km * g_inv)))
        bg.append(bd(_split(bb * g_inv)))
        kdbd.append(_split(jnp.concatenate([km * g_end, bb * g_end], axis=0)))
        etot.append(jnp.exp(tot))
        s_p.append(s_scr[b, p])

    ss = each(lambda i: lax.dot_general(lhs2[i][0], s_p[i].astype(BF16), (((1,), (1,)), ((), ())),
                                        preferred_element_type=F32))
    kb = each(lambda i: _mm(lhs2[i], tuple(jnp.concatenate([x, y], axis=0)
                                           for x, y in zip(kg[i], bg[i])), nt=True))
    a_b = each(lambda i: jnp.where(strict, kb[i][:chunk, PAIR:], 0.0))
    b_b = each(lambda i: jnp.where(incl, kb[i][chunk:, PAIR:], 0.0))
    av = each(lambda i: pm1(jnp.concatenate(
        [jnp.where(strict, kb[i][:chunk, :PAIR], 0.0), jnp.where(incl, kb[i][chunk:, :PAIR], 0.0)],
        axis=0), vv[i]))

    dg = each(lambda i: jnp.where(m8, a_b[i], 0.0))
    d2 = each(lambda i: pm1(dg[i], dg[i]))
    d3 = each(lambda i: pm1(dg[i], d2[i]))
    d4 = each(lambda i: pm1(d2[i], d2[i]))
    t1 = each(lambda i: eye - dg[i] + d2[i] - d3[i])
    inv = each(lambda i: t1[i] + pm1(t1[i], d4[i]))
    for m in levels:
        mlev = each(lambda i: pm1(inv[i], jnp.where(m, a_b[i], 0.0)))
        inv = each(lambda i: inv[i] - pm1(mlev[i], inv[i]))

    u = each(lambda i: pm1(inv[i], ss[i][:chunk] + av[i][:chunk]))
    upd = each(lambda i: _mm(_split(jnp.transpose(jnp.concatenate([vv[i], -u[i]], axis=0))),
                             kdbd[i]))
    bu = each(lambda i: pm1(b_b[i], u[i]))
    for i, (b, p) in enumerate(chains):
        sl = slice(p * PAIR, (p + 1) * PAIR)
        o_ref[b, :, sl] = ss[i][chunk:] + av[i][chunk:] - bu[i]
        s_scr[b, p] = s_p[i] * etot[i] + jnp.where(same_head, upd[i], 0.0)

    sout_ref[...] = s_scr[...]


def _rwkv_chunk(r, km, v, lw, kk, bb, s0, *, chunk, nb):
    nseq, t, _ = r.shape
    assert chunk == HD_B and nseq % nb == 0
    npair = H_B // 2
    row = pl.BlockSpec((nb, chunk, W_B), lambda b, c: (b, c, 0))
    st = pl.BlockSpec((nb, npair, PAIR, PAIR), lambda b, c: (b, 0, 0, 0))
    kern = functools.partial(_rwkv_chunk_kernel, chunk=chunk, nb=nb)
    return pl.pallas_call(
        kern,
        grid=(nseq // nb, t // chunk),
        in_specs=[row] * 6 + [st],
        out_specs=[row, st],
        out_shape=[jax.ShapeDtypeStruct((nseq, t, W_B), F32),
                   jax.ShapeDtypeStruct((nseq, npair, PAIR, PAIR), F32)],
        scratch_shapes=[pltpu.VMEM((nb, npair, PAIR, PAIR), F32)],
        compiler_params=_cparams(("parallel", "arbitrary")),
        name="rwkv_chunk",
    )(r, km, v, lw, kk, bb, s0)


MXU_N = 256


def _mix_residual(x_ref, oa_ref, o_ref, r_ref, km_ref, v_ref, g_ref, lnw_ref, lnb_ref, rk_ref,
                  wa_ref, wb_ref):
    ones = _head_ones(W_B)
    o = o_ref[...]
    cen = o - _head_sum(o, ones) * (1.0 / HD_B)
    head_sum1 = lambda z: jnp.dot(z.astype(BF16), ones, preferred_element_type=F32)
    var = head_sum1(cen * cen) * (1.0 / HD_B)
    on = cen * lax.rsqrt(var + GN_EPS) * lnw_ref[...] + lnb_ref[...]
    bonus = head_sum1(r_ref[...] * km_ref[...] * rk_ref[...]) * v_ref[...]
    ob = (on + bonus) * g_ref[...]
    return (x_ref[...]
            + jnp.dot(oa_ref[...].astype(BF16), wa_ref[...], preferred_element_type=F32)
            + jnp.dot(ob.astype(BF16), wb_ref[...], preferred_element_type=F32))


def _proj_residual(x_ref, o_ref, w_ref):
    return x_ref[...] + jnp.dot(o_ref[...].astype(BF16), w_ref[...],
                                preferred_element_type=F32)


def _res_ffn_kernel(*refs, residual, n_res, final_norm):
    g_ref, wg_ref, wu_ref, wd_ref, gf_ref, y_ref, xf_scr, x_scr, h_scr = refs[n_res:]
    x = residual(*refs[:n_res])
    xf_scr[...] = _rms(x, g_ref[...]).astype(BF16)
    x_scr[...] = x
    for j in range(wg_ref.shape[1] // MXU_N):
        cols = slice(j * MXU_N, (j + 1) * MXU_N)
        xf = xf_scr[...]
        hg = jnp.dot(xf, wg_ref[:, cols], preferred_element_type=F32)
        hu = jnp.dot(xf, wu_ref[:, cols], preferred_element_type=F32)
        h_scr[:, cols] = ((hg * _sigmoid(hg)) * hu).astype(BF16)
    y = x_scr[...] + jnp.dot(h_scr[...], wd_ref[...], preferred_element_type=F32)
    if final_norm:
        y = _rms(y, gf_ref[...])
    y_ref[...] = y


def _res_ffn(residual, row_ins, const_ins, g, wg, wu, wd, gf, *, tm, final_norm):
    n, d = row_ins[0].shape
    whole = lambda a: pl.BlockSpec(a.shape, lambda i: (0,) * a.ndim,
                                   pipeline_mode=pl.Buffered(1))
    consts = list(const_ins) + [g.reshape(1, d), wg, wu, wd, gf.reshape(1, d)]
    kern = functools.partial(_res_ffn_kernel, residual=residual,
                             n_res=len(row_ins) + len(const_ins), final_norm=final_norm)
    return pl.pallas_call(
        kern,
        grid=(n // tm,),
        in_specs=[pl.BlockSpec((tm, a.shape[1]), lambda i: (i, 0)) for a in row_ins]
                 + [whole(a) for a in consts],
        out_specs=pl.BlockSpec((tm, d), lambda i: (i, 0)),
        out_shape=jax.ShapeDtypeStruct((n, d), F32),
        scratch_shapes=[pltpu.VMEM((tm, d), BF16), pltpu.VMEM((tm, d), F32),
                        pltpu.VMEM((tm, wg.shape[1]), BF16)],
        compiler_params=_cparams(("parallel",)),
        name="res_ffn",
    )(*row_ins, *consts)


HGRN_SAFE_LOG = 80.0


def _hgrn_kernel(q_ref, f_ref, i_ref, g_ref, s0_ref, lbc_ref, gn_ref, o_ref, sout_ref,
                 s_scr, att_scr, *, layer, chunk, valid, nb):
    c = pl.program_id(1)

    @pl.when(c == 0)
    def _():
        s_scr[...] = s0_ref[...]

    lbc = lbc_ref[...]
    e = jnp.exp(lbc - jnp.max(lbc, axis=0, keepdims=True))
    sm = e / jnp.sum(e, axis=0, keepdims=True)
    lbf = sm[1:2, :]
    for l in range(2, layer + 1):
        lbf = lbf + sm[l:l + 1, :]

    nblk = chunk // SUBLANES
    sub = lax.broadcasted_iota(jnp.int32, (SUBLANES, DK_C), 0)
    lane_c = lax.broadcasted_iota(jnp.int32, (SUBLANES, chunk), 1)
    rowc = lax.broadcasted_iota(jnp.int32, (chunk, F_C), 0)
    tril = jnp.where(lax.broadcasted_iota(jnp.int32, (chunk, chunk), 0)
                     >= lax.broadcasted_iota(jnp.int32, (chunk, chunk), 1), 1.0, 0.0).astype(BF16)

    qs, ks, bs, ivs = [], [], [], []
    for b in range(nb):
        fr = f_ref[b]
        ez = jnp.exp(-jnp.abs(fr))
        rz = 1.0 / (1.0 + ez)
        sig = jnp.where(fr >= 0.0, rz, ez * rz)
        nsig = jnp.where(fr >= 0.0, ez * rz, rz)
        logf = jnp.log(lbf + (1.0 - lbf) * sig)
        kfull = (1.0 - lbf) * nsig
        if valid < chunk:
            logf = jnp.where(rowc < valid, logf, 0.0)
            kfull = jnp.where(rowc < valid, kfull, 0.0)
        hi, lo = _split(logf)
        bc = jnp.dot(jnp.concatenate([tril, tril], axis=1), jnp.concatenate([hi, lo], axis=0),
                     preferred_element_type=F32)
        qs.append(jax.nn.silu(q_ref[b]))
        ks.append(kfull)
        bs.append(bc)
        ivs.append(i_ref[b].astype(BF16))

    chains = [(b, h) for b in range(nb) for h in range(H_C)]
    each = lambda f: [f(i) for i in range(len(chains))]
    hsl = lambda h: slice(h * DK_C, (h + 1) * DK_C)
    q = each(lambda i: qs[chains[i][0]][:, hsl(chains[i][1])])
    k = each(lambda i: ks[chains[i][0]][:, hsl(chains[i][1])])
    bcs = each(lambda i: bs[chains[i][0]][:, hsl(chains[i][1])])
    iv = each(lambda i: ivs[chains[i][0]][:, hsl(chains[i][1])])
    s_h = each(lambda i: s_scr[chains[i][0], chains[i][1]])

    def off_diag(i, tb):
        lo = tb * SUBLANES
        rs = slice(lo, lo + SUBLANES)
        edge = bcs[i][lo - 1:lo, :]
        qi = q[i][rs, :] * jnp.exp(bcs[i][rs, :] - edge)
        ki = k[i][0:lo, :] * jnp.exp(edge - bcs[i][0:lo, :])
        ki = jnp.concatenate([ki, jnp.zeros((chunk - lo, DK_C), F32)], axis=0)
        return lax.dot_general(qi.astype(BF16), ki.astype(BF16), (((1,), (1,)), ((), ())),
                               preferred_element_type=F32)

    inter = each(lambda i: jnp.dot((q[i] * jnp.exp(bcs[i])).astype(BF16), s_h[i].astype(BF16),
                                   preferred_element_type=F32))


    sub_c = lax.broadcasted_iota(jnp.int32, (SUBLANES, chunk), 0)

    def att_on_matrix_unit():
        def block(i, tb):
            lo = tb * SUBLANES
            rs = slice(lo, lo + SUBLANES)
            dloc = bcs[i][rs, :] if tb == 0 else bcs[i][rs, :] - bcs[i][lo - 1:lo, :]
            qi = q[i][rs, :] * jnp.exp(dloc)
            parts = [k[i][rs, :] * jnp.exp(-dloc)]
            if tb > 0:
                parts.insert(0, k[i][0:lo, :] * jnp.exp(bcs[i][lo - 1:lo, :] - bcs[i][0:lo, :]))
            if chunk - lo - SUBLANES > 0:
                parts.append(jnp.zeros((chunk - lo - SUBLANES, DK_C), F32))
            ki = jnp.concatenate(parts, axis=0) if len(parts) > 1 else parts[0]
            return lax.dot_general(qi.astype(BF16), ki.astype(BF16), (((1,), (1,)), ((), ())),
                                   preferred_element_type=F32)

        blks = each(lambda i: [block(i, tb) for tb in range(nblk)])
        for i in range(len(chains)):
            for tb in range(nblk):
                att_scr[i, tb * SUBLANES:(tb + 1) * SUBLANES, :] = jnp.where(
                    lane_c <= sub_c + tb * SUBLANES, blks[i][tb], 0.0)

    def att_pairwise_diagonal():
        offd = each(lambda i: [off_diag(i, tb) for tb in range(1, nblk)])
        for i in range(len(chains)):
            for tb in range(nblk):
                rs = slice(tb * SUBLANES, (tb + 1) * SUBLANES)
                blk = jnp.zeros((SUBLANES, chunk), F32) if tb == 0 else offd[i][tb - 1]
                for j in range(SUBLANES):
                    s = tb * SUBLANES + j
                    if s >= valid:
                        break
                    diff = jnp.where(sub >= j, bcs[i][rs, :] - bcs[i][s:s + 1, :], -jnp.inf)
                    ev = jnp.exp(diff) * (q[i][rs, :] * k[i][s:s + 1, :])
                    blk = jnp.where(lane_c == s, jnp.sum(ev, axis=1, keepdims=True), blk)
                att_scr[i, rs, :] = blk

    def new_state(i):
        bl = bcs[i][chunk - 1:chunk, :]
        kd = k[i] * jnp.exp(bl - bcs[i])
        pieces = [kd, jnp.broadcast_to(jnp.exp(bl), (SUBLANES, DK_C))]
        fill = DK_C - chunk - SUBLANES
        if fill > 0:
            pieces.append(jnp.zeros((fill, DK_C), F32))
        xt = jnp.transpose(jnp.concatenate(pieces, axis=0))
        return (xt[:, chunk:chunk + 1] * s_h[i]
                + jnp.dot(xt[:, 0:chunk].astype(BF16), iv[i], preferred_element_type=F32))

    if nblk > 1:
        att_on_matrix_unit()
        worst = jnp.float32(0.0)
        for b in range(nb):
            win = bs[b] - jnp.where(rowc >= SUBLANES, pltpu.roll(bs[b], SUBLANES, 0), 0.0)
            worst = jnp.minimum(worst, jnp.min(win))
        pl.when(worst < -HGRN_SAFE_LOG)(att_pairwise_diagonal)
    else:
        att_pairwise_diagonal()

    o_h = each(lambda i: jnp.dot(att_scr[i].astype(BF16), iv[i], preferred_element_type=F32)
               + inter[i])
    s_new = each(new_state)
    for i, (b, h) in enumerate(chains):
        s_scr[b, h] = s_new[i]
    for b in range(nb):
        o = jnp.concatenate([o_h[b * H_C + h] for h in range(H_C)], axis=1)
        o_ref[b] = _rms(o, gn_ref[...]) * jax.nn.silu(g_ref[b])
    sout_ref[...] = s_scr[...]


def _hgrn(pc3, s0, lbc, gn, *, layer, chunk, valid, nb):
    nseq, t, _ = pc3.shape
    assert nseq % nb == 0
    col = lambda j: pl.BlockSpec((nb, chunk, F_C), lambda b, c, j=j: (b, c, j))
    st = pl.BlockSpec((nb, H_C, DK_C, DK_C), lambda b, c: (b, 0, 0, 0))
    kern = functools.partial(_hgrn_kernel, layer=layer, chunk=chunk, valid=valid, nb=nb)
    return pl.pallas_call(
        kern,
        grid=(nseq // nb, t // chunk),
        in_specs=[col(0), col(1), col(2), col(3), st,
                  pl.BlockSpec(lbc.shape, lambda b, c: (0, 0)),
                  pl.BlockSpec((1, D_MODEL), lambda b, c: (0, 0))],
        out_specs=[pl.BlockSpec((nb, chunk, D_MODEL), lambda b, c: (b, c, 0)), st],
        out_shape=[jax.ShapeDtypeStruct((nseq, t, D_MODEL), F32),
                   jax.ShapeDtypeStruct(s0.shape, F32)],
        scratch_shapes=[pltpu.VMEM((nb, H_C, DK_C, DK_C), F32),
                        pltpu.VMEM((nb * H_C, chunk, chunk), F32)],
        compiler_params=_cparams(("parallel", "arbitrary")),
        name="hgrn",
    )(pc3, pc3, pc3, pc3, s0, lbc, gn.reshape(1, D_MODEL))


def _block_diag(w):
    h, n, _ = w.shape
    eye = jnp.eye(h, dtype=w.dtype)
    return (eye[:, None, :, None] * w[:, :, None, :]).reshape(h * n, h * n)


ROW_TILE = 512
HGRN_CHUNK = 64
HGRN_SEQS_PER_STEP = 4


def _trunk(x2d, conv_st, h_st, shift_st, rs_st, hs_st, P, *, nbatch, nsteps, time_major):
    n = x2d.shape[0]
    if time_major:
        nseq, step, rows = 1, nbatch, n
        tm = n
    else:
        nseq, step, rows = nbatch, 1, ROW_TILE
        tm = ROW_TILE
    nchain = nbatch * H_B

    out_a, conv_new, h_new, r, km, v, lw, kk, bb, g, shift_new = _front(
        x2d, P['ln_mix'][0], P['w_in_ab'],
        conv_st, h_st, P['conv_w'], P['conv_b'], P['wr'], P['wi'], P['gr_b'], P['gi_b'],
        P['lru_lambda'],
        shift_st, P['mu_b'], P['w0_b'], P['a0_b'], P['w2p'], P['a2p'], P['g2_b'],
        P['kk_b'], P['ka_b'], nseq=nseq, step=step, rows=rows)

    if time_major:
        ops = jnp.stack([r, km, v, lw, kk, bb]).reshape(6, nsteps, nbatch, H_B, HD_B)
        ops = ops.transpose(0, 1, 4, 2, 3).reshape(6, nsteps, HD_B, nchain)
        s0 = rs_st.transpose(2, 3, 0, 1).reshape(HD_B, HD_B, nchain)
        o, s_new = _rwkv_scan(*(ops[i] for i in range(6)), s0, tb=nsteps)
        o = o.reshape(nsteps, HD_B, nbatch, H_B).transpose(0, 2, 3, 1).reshape(n, W_B)
        rs_new = s_new.reshape(HD_B, HD_B, nbatch, H_B).transpose(2, 3, 0, 1)
    else:
        z = jnp.zeros_like(rs_st[:, 0::2])
        s0 = jnp.concatenate([jnp.concatenate([rs_st[:, 0::2], z], axis=-1),
                              jnp.concatenate([z, rs_st[:, 1::2]], axis=-1)], axis=-2)
        seq3 = lambda y: y.reshape(nbatch, nsteps, W_B)
        o, s_new = _rwkv_chunk(seq3(r), seq3(km), seq3(v), seq3(lw), seq3(kk), seq3(bb), s0,
                               chunk=HD_B, nb=RWKV_SEQS_PER_STEP)
        o = o.reshape(n, W_B)
        rs_new = jnp.stack([s_new[:, :, :HD_B, :HD_B], s_new[:, :, HD_B:, HD_B:]],
                           axis=2).reshape(nbatch, H_B, HD_B, HD_B)

    vec = lambda p: p.reshape(1, W_B)
    x2 = _res_ffn(_mix_residual, [x2d, out_a, o, r, km, v, g],
                  [vec(P['lnx_w']), vec(P['lnx_b']), vec(P['rk_b']), P['w_out_a'], P['w_out_b']],
                  P['ln_ffn'][0], P['ffn_gate'][0], P['ffn_up'][0], P['ffn_down'][0],
                  P['ln_final'], tm=tm, final_norm=False)

    pc = _norm_matmul(x2, P['ln_mix'][1], P['w_in_c'], tm)
    if time_major:
        pc3 = pc.reshape(nsteps, nbatch, 4 * F_C).transpose(1, 0, 2)
        chunk = SUBLANES
        pc3 = jnp.pad(pc3, ((0, 0), (0, chunk - nsteps), (0, 0)))
    else:
        pc3 = pc.reshape(nbatch, nsteps, 4 * F_C)
        chunk = HGRN_CHUNK
    o3, hs_new = _hgrn(pc3, hs_st, P['lb_c'], P['gn_c'], layer=1, chunk=chunk,
                       valid=min(chunk, nsteps),
                       nb=2 * HGRN_SEQS_PER_STEP if time_major else HGRN_SEQS_PER_STEP)
    if time_major:
        o = o3[:, :nsteps].transpose(1, 0, 2).reshape(n, D_MODEL)
    else:
        o = o3.reshape(n, D_MODEL)
    y = _res_ffn(_proj_residual, [x2, o], [P['w_out_c']],
                 P['ln_ffn'][1], P['ffn_gate'][1], P['ffn_up'][1], P['ffn_down'][1],
                 P['ln_final'], tm=tm, final_norm=True)
    return y, conv_new, h_new, shift_new, rs_new, hs_new


def kernel(x_prompt, x_sample, state_rglru_conv, state_rglru_h, state_rwkv_shift, state_rwkv_S,
           state_hgrn_S, ln_mix, ln_ffn, ln_final, w_in_ab, conv_w, conv_b, gr_w, gr_b, gi_w,
           gi_b, lru_lambda, mu_b, w0_b, w2_b, a0_b, a2_b, g2_b, kk_b, ka_b, rk_b, lnx_w, lnx_b,
           w_out_ab, w_in_c, lb_c, gn_c, w_out_c, ffn_gate, ffn_up, ffn_down):
    bp, tp, _ = x_prompt.shape
    bs, ts, _ = x_sample.shape
    zpad_w = jnp.zeros((LORA_A, W_B), F32)
    zpad_a = jnp.zeros((LORA_W, W_B), F32)
    P = dict(
        ln_mix=ln_mix, ln_ffn=ln_ffn, ln_final=ln_final,
        w_in_ab=w_in_ab[0], conv_w=conv_w[0], conv_b=conv_b[0],
        wr=_block_diag(gr_w[0]).astype(BF16), wi=_block_diag(gi_w[0]).astype(BF16),
        gr_b=gr_b[0], gi_b=gi_b[0], lru_lambda=lru_lambda[0], mu_b=mu_b[0], w0_b=w0_b[0],
        a0_b=a0_b[0],
        w2p=jnp.concatenate([w2_b[0], zpad_w], axis=0).astype(BF16),
        a2p=jnp.concatenate([zpad_a, a2_b[0]], axis=0).astype(BF16),
        g2_b=g2_b[0].astype(BF16), kk_b=kk_b[0], ka_b=ka_b[0], rk_b=rk_b[0],
        lnx_w=lnx_w[0], lnx_b=lnx_b[0],
        w_out_a=w_out_ab[0, :W_A].astype(BF16), w_out_b=w_out_ab[0, W_A:].astype(BF16),
        w_in_c=w_in_c[0], lb_c=lb_c, gn_c=gn_c[0],
        w_out_c=w_out_c[0].astype(BF16), ffn_gate=ffn_gate.astype(BF16),
        ffn_up=ffn_up.astype(BF16), ffn_down=ffn_down.astype(BF16))

    yp, p_conv, p_h, p_shift, p_rs, p_hs = _trunk(
        x_prompt.reshape(bp * tp, D_MODEL),
        jnp.zeros((bp, CONV_W - 1, W_A), F32), jnp.zeros((bp, 1, W_A), F32),
        jnp.zeros((bp, 1, P_B), F32), jnp.zeros((bp, H_B, HD_B, HD_B), F32),
        jnp.zeros((bp, H_C, DK_C, DK_C), F32), P, nbatch=bp, nsteps=tp, time_major=False)

    ys, s_conv, s_h, s_shift, s_rs, s_hs = _trunk(
        x_sample.transpose(1, 0, 2).reshape(ts * bs, D_MODEL),
        state_rglru_conv[0].transpose(1, 0, 2).reshape(1, (CONV_W - 1) * bs, W_A),
        state_rglru_h[0].reshape(1, bs, W_A), state_rwkv_shift[0].reshape(1, bs, P_B),
        state_rwkv_S[0], state_hgrn_S[0], P, nbatch=bs, nsteps=ts, time_major=True)

    return (yp.reshape(bp, tp, D_MODEL),
            ys.reshape(ts, bs, D_MODEL).transpose(1, 0, 2),
            p_conv[None], p_h.reshape(1, bp, W_A), p_shift.reshape(1, bp, P_B), p_rs[None],
            p_hs[None],
            s_conv.reshape(CONV_W - 1, bs, W_A).transpose(1, 0, 2)[None],
            s_h.reshape(1, bs, W_A), s_shift.reshape(1, bs, P_B), s_rs[None], s_hs[None])
```

```python
import functools

import jax
import jax.numpy as jnp
from jax import lax
from jax.experimental import pallas as pl
from jax.experimental.pallas import tpu as pltpu

F32 = jnp.float32
BF16 = jnp.bfloat16

D_MODEL = 1024
W_A = 512
H_A = 8
CONV_W = 4
LRU_C = 8.0
W_B = 512
HD_B = 64
H_B = 8
LORA_W = 64
LORA_A = 64
LORA_G = 128
P_A = 2 * W_A
P_B = 3 * W_B + LORA_W + LORA_A + LORA_G
DK_C = 128
H_C = 8
F_C = 1024
D_FF = 2816
RMS_EPS = 1e-6
GN_EPS = 64e-5

LANES = 128
SUBLANES = 8
VMEM_LIMIT = 56 * 1024 * 1024


def _cparams(sem):
    return pltpu.CompilerParams(dimension_semantics=sem, vmem_limit_bytes=VMEM_LIMIT)


def _softplus(x):
    return jnp.maximum(x, 0.0) + jnp.log1p(jnp.exp(-jnp.abs(x)))

def _sigmoid(x):
    return jax.nn.sigmoid(x)


def _rms(x, g):
    ms = jnp.mean(x * x, axis=-1, keepdims=True)
    return x * lax.rsqrt(ms + RMS_EPS) * g


def _split(x):
    hi = x.astype(BF16)
    lo = (x - hi.astype(F32)).astype(BF16)
    return hi, lo


def _mm(a, b, nt=False):
    ah, al = a
    bh, bl = b
    lhs = jnp.concatenate([ah, al, ah], axis=1)
    if nt:
        rhs = jnp.concatenate([bh, bh, bl], axis=1)
        return lax.dot_general(lhs, rhs, (((1,), (1,)), ((), ())), preferred_element_type=F32)
    rhs = jnp.concatenate([bh, bh, bl], axis=0)
    return jnp.dot(lhs, rhs, preferred_element_type=F32)


def _head_ones(width):
    r = lax.broadcasted_iota(jnp.int32, (width, width), 0) // HD_B
    c = lax.broadcasted_iota(jnp.int32, (width, width), 1) // HD_B
    return jnp.where(r == c, 1.0, 0.0).astype(BF16)


def _head_sum(x, ones):
    hi, lo = _split(x)
    return jnp.dot(jnp.concatenate([hi, lo], axis=1), jnp.concatenate([ones, ones], axis=0),
                   preferred_element_type=F32)


def _norm_matmul_kernel(x_ref, g_ref, w_ref, o_ref, w_bf):
    @pl.when(pl.program_id(0) == 0)
    def _():
        w_bf[...] = w_ref[...].astype(BF16)

    xn = _rms(x_ref[...], g_ref[...]).astype(BF16)
    o_ref[...] = jnp.dot(xn, w_bf[...], preferred_element_type=F32)


def _norm_matmul(x, g, w, tm):
    n, d = x.shape
    p = w.shape[1]
    assert n % tm == 0
    return pl.pallas_call(
        _norm_matmul_kernel,
        grid=(n // tm,),
        in_specs=[pl.BlockSpec((tm, d), lambda i: (i, 0)),
                  pl.BlockSpec((1, d), lambda i: (0, 0)),
                  pl.BlockSpec((d, p), lambda i: (0, 0), pipeline_mode=pl.Buffered(1))],
        out_specs=pl.BlockSpec((tm, p), lambda i: (i, 0)),
        out_shape=jax.ShapeDtypeStruct((n, p), F32),
        scratch_shapes=[pltpu.VMEM((d, p), BF16)],
        compiler_params=_cparams(("arbitrary",)),
        name="norm_matmul",
    )(x, g.reshape(1, d), w)


def _rglru_body(u_in, gate, cw_ref, cb_ref, wr_ref, wi_ref, br_ref, bi_ref, lam_ref,
                out_ref, cnew_ref, hnew_ref, xbuf, hcar, bbuf, *, step, rows, pad):
    hist = (CONV_W - 1) * step

    if step == 1:
        u = u_in
        prev = xbuf[0:SUBLANES, :]
        sub8 = lax.broadcasted_iota(jnp.int32, (SUBLANES, W_A), 0)
        conv = cb_ref[...]
        for j in range(CONV_W):
            d = CONV_W - 1 - j
            if d == 0:
                ush = u
            else:
                rolled = pltpu.roll(u, d, 0)
                head = jnp.where(sub8 < d, pltpu.roll(prev, d, 0), rolled[0:SUBLANES, :])
                ush = jnp.concatenate([head, rolled[SUBLANES:, :]], axis=0)
            conv = conv + ush * cw_ref[j:j + 1, :]
        cnew_ref[0] = u[rows - hist:rows, :]
        xbuf[0:SUBLANES, :] = u[rows - SUBLANES:rows, :]
    else:
        xbuf[pad:pad + rows, :] = u_in
        conv = cb_ref[...] + xbuf[pad - hist:pad - hist + rows, :] * cw_ref[0:1, :]
        for j in range(1, CONV_W):
            o = pad - hist + j * step
            conv = conv + xbuf[o:o + rows, :] * cw_ref[j:j + 1, :]
        tail = xbuf[pad + rows - hist:pad + rows, :]
        cnew_ref[0] = tail
        xbuf[pad - hist:pad, :] = tail

    ub = conv.astype(BF16)
    r = _sigmoid(jnp.dot(ub, wr_ref[...], preferred_element_type=F32) + br_ref[...])
    ig = _sigmoid(jnp.dot(ub, wi_ref[...], preferred_element_type=F32) + bi_ref[...])
    log_a = (-LRU_C) * r * _softplus(-lam_ref[...])
    a = jnp.exp(log_a)
    bterm = jnp.sqrt(-jnp.tanh(log_a) * (a * a + 1.0)) * (ig * conv)
    bbuf[...] = bterm
    bbuf[0:step, :] = bterm[0:step, :] + a[0:step, :] * hcar[...]
    bv = bbuf[...]

    def scan_levels(a, bv, pos, first, count):
        d = first
        while d < first * count:
            m = pos >= d
            bv = jnp.where(m, a * pltpu.roll(bv, d, 0) + bv, bv)
            a = jnp.where(m, a * pltpu.roll(a, d, 0), a)
            d *= 2
        return a, bv

    row = lax.broadcasted_iota(jnp.int32, (rows, W_A), 0)
    a, bv = scan_levels(a, bv, row, step, rows // step)
    hcar[...] = bv[rows - step:rows, :]
    hnew_ref[0] = bv[rows - step:rows, :]
    out_ref[...] = jax.nn.gelu(gate) * bv


def _rwkv_prep_body(pf, mu_ref, w0_ref, a0_ref, w2_ref, a2_ref, g2_ref, kkp_ref, kap_ref,
                    r_ref, km_ref, v_ref, lw_ref, kk_ref, bb_ref, g_ref, snew_ref, pbuf,
                    *, step, rows, pad):

    if step == 1:
        rolled = pltpu.roll(pf, 1, 0)
        sub8 = lax.broadcasted_iota(jnp.int32, (SUBLANES, P_B), 0)
        head = jnp.where(sub8 < 1, pltpu.roll(pbuf[0:SUBLANES, :], 1, 0), rolled[0:SUBLANES, :])
        shifted = jnp.concatenate([head, rolled[SUBLANES:, :]], axis=0)
        pbuf[0:SUBLANES, :] = pf[rows - SUBLANES:rows, :]
    else:
        pbuf[pad:pad + rows, :] = pf
        shifted = pbuf[pad - step:pad - step + rows, :]
        pbuf[pad - step:pad, :] = pf[rows - step:rows, :]
    snew_ref[0] = pf[rows - step:rows, :]

    m = pf + (shifted - pf) * mu_ref[...]
    o1 = 3 * W_B
    r_ref[...] = m[:, 0:W_B]
    kraw = m[:, W_B:2 * W_B]
    v_ref[...] = m[:, 2 * W_B:o1]
    xwa = m[:, o1:o1 + LORA_W + LORA_A]
    xg = m[:, o1 + LORA_W + LORA_A:]
    lane = lax.broadcasted_iota(jnp.int32, xwa.shape, 1)
    lhs = jnp.where(lane < LORA_W, jnp.tanh(xwa), xwa).astype(BF16)
    lw = jnp.dot(lhs, w2_ref[...], preferred_element_type=F32)
    la = jnp.dot(lhs, a2_ref[...], preferred_element_type=F32)
    w_log = -_softplus(-(w0_ref[...] + lw)) - 0.5
    lw_ref[...] = -jnp.exp(w_log)
    a = _sigmoid(a0_ref[...] + la)
    g_ref[...] = jnp.dot(_sigmoid(xg).astype(BF16), g2_ref[...], preferred_element_type=F32)
    kk = kraw * kkp_ref[...]
    kk = kk * lax.rsqrt(jnp.maximum(_head_sum(kk * kk, _head_ones(W_B)), 1e-24))
    kk_ref[...] = kk
    bb_ref[...] = kk * a
    km_ref[...] = kraw * (1.0 + (a - 1.0) * kap_ref[...])


N_RGLRU_IN = 9
N_PREP_IN = 9


def _front_kernel(*refs, step, rows, pad_a, pad_b, nsplit):
    x_ref, gn_ref, w_ref = refs[:3]
    rg_in = refs[3:3 + N_RGLRU_IN]
    pp_in = refs[3 + N_RGLRU_IN:3 + N_RGLRU_IN + N_PREP_IN]
    outs = refs[3 + N_RGLRU_IN + N_PREP_IN:]
    rg_out, pp_out = outs[:3], outs[3:11]
    xbuf, hcar, bbuf, pbuf, p_scr, w_bf = outs[11:]
    part = rows // nsplit

    @pl.when(pl.program_id(1) == 0)
    def _():
        hist = (CONV_W - 1) * step
        xbuf[pad_a - hist:pad_a, :] = rg_in[0][0]
        hcar[...] = rg_in[1][0]
        pbuf[pad_b - step:pad_b, :] = pp_in[0][0]
        w_bf[...] = w_ref[...].astype(BF16)

    xn = _rms(x_ref[...], gn_ref[...]).astype(BF16)
    for s in range(nsplit):
        p_scr[s] = jnp.dot(xn[s * part:(s + 1) * part, :], w_bf[...],
                           preferred_element_type=F32)
    for s in range(nsplit):
        sub = lambda ref, s=s: ref.at[s * part:(s + 1) * part, :]
        _rglru_body(p_scr[s, :, 0:W_A], p_scr[s, :, W_A:P_A], *rg_in[2:],
                    sub(rg_out[0]), rg_out[1], rg_out[2], xbuf, hcar, bbuf,
                    step=step, rows=part, pad=pad_a)
        _rwkv_prep_body(p_scr[s, :, P_A:], *pp_in[1:], *[sub(r) for r in pp_out[:7]],
                        pp_out[7], pbuf, step=step, rows=part, pad=pad_b)


def _front(x, gn, w, conv_st, h_st, cw, cb, wr, wi, br, bi, lam,
           prev, mu, w0, a0, w2p, a2p, g2, kkp, kap, *, nseq, step, rows):
    n, d = x.shape
    nt = n // (nseq * rows)
    hist = (CONV_W - 1) * step
    pad_a = max(SUBLANES, hist)
    pad_b = max(SUBLANES, step)
    row_map = lambda b, t: (b * nt + t, 0)
    seq_map = lambda b, t: (b, 0, 0)
    whole = lambda a: pl.BlockSpec(a.shape, lambda b, t: (0,) * a.ndim,
                                   pipeline_mode=pl.Buffered(1))
    va = lambda p: p.reshape(1, W_A)
    vb = lambda p: p.reshape(1, W_B)
    consts_a = [cw, va(cb), wr, wi, va(br), va(bi), va(lam)]
    consts_b = [mu.reshape(1, P_B), vb(w0), vb(a0), w2p, a2p, g2, vb(kkp), vb(kap)]
    nsplit = 2 if step == 1 else 1
    part = rows // nsplit
    kern = functools.partial(_front_kernel, step=step, rows=rows, pad_a=pad_a, pad_b=pad_b,
                             nsplit=nsplit)
    tile = lambda wd: pl.BlockSpec((rows, wd), row_map)
    return pl.pallas_call(
        kern,
        grid=(nseq, nt),
        in_specs=[tile(d), whole(gn.reshape(1, d)), whole(w),
                  pl.BlockSpec((1, hist, W_A), seq_map), pl.BlockSpec((1, step, W_A), seq_map)]
                 + [whole(a) for a in consts_a]
                 + [pl.BlockSpec((1, step, P_B), seq_map)] + [whole(a) for a in consts_b],
        out_specs=[tile(W_A), pl.BlockSpec((1, hist, W_A), seq_map),
                   pl.BlockSpec((1, step, W_A), seq_map)]
                  + [tile(W_B)] * 7 + [pl.BlockSpec((1, step, P_B), seq_map)],
        out_shape=[jax.ShapeDtypeStruct((n, W_A), F32),
                   jax.ShapeDtypeStruct((nseq, hist, W_A), F32),
                   jax.ShapeDtypeStruct((nseq, step, W_A), F32)]
                  + [jax.ShapeDtypeStruct((n, W_B), F32)] * 7
                  + [jax.ShapeDtypeStruct((nseq, step, P_B), F32)],
        scratch_shapes=[pltpu.VMEM((pad_a + part, W_A), F32),
                        pltpu.VMEM((step, W_A), F32),
                        pltpu.VMEM((part, W_A), F32),
                        pltpu.VMEM((pad_b + part, P_B), F32),
                        pltpu.VMEM((nsplit, part, P_A + P_B), F32),
                        pltpu.VMEM(w.shape, BF16)],
        compiler_params=_cparams(("parallel", "arbitrary")),
        name="front",
    )(x, gn.reshape(1, d), w, conv_st, h_st, *consts_a, prev, *consts_b)


VGROUP = 4


def _rwkv_scan_kernel(r_ref, km_ref, v_ref, lw_ref, kk_ref, bb_ref, s0_ref, o_ref, sout_ref,
                      s_scr, *, tb):
    tblk = pl.program_id(1)

    @pl.when(tblk == 0)
    def _():
        s_scr[...] = s0_ref[...]

    def colsum(x):
        return jnp.sum(x, axis=0, keepdims=True)

    sub = lax.broadcasted_iota(jnp.int32, (VGROUP, LANES), 0)
    for g in range(HD_B // VGROUP):
        vbase = g * VGROUP

        def step(t, state, vbase=vbase):
            kk = kk_ref[t]
            w = jnp.exp(lw_ref[t])
            bvec = bb_ref[t]
            km = km_ref[t]
            r = r_ref[t]
            vt = v_ref[t, vbase:vbase + VGROUP, :]
            new = []
            otile = jnp.zeros((VGROUP, LANES), F32)
            for i in range(VGROUP):
                sk = colsum(state[i] * kk)
                sv = state[i] * w - sk * bvec + vt[i:i + 1, :] * km
                new.append(sv)
                otile = jnp.where(sub == i, colsum(sv * r), otile)
            o_ref[t, vbase:vbase + VGROUP, :] = otile
            return tuple(new)

        state = tuple(s_scr[vbase + i] for i in range(VGROUP))
        state = lax.fori_loop(0, tb, step, state, unroll=2)
        for i in range(VGROUP):
            s_scr[vbase + i] = state[i]

    sout_ref[...] = s_scr[...]


def _rwkv_scan(r, km, v, lw, kk, bb, s0, *, tb):
    nt, _, nl = r.shape
    seq = pl.BlockSpec((tb, HD_B, LANES), lambda g, t: (t, 0, g))
    st = pl.BlockSpec((HD_B, HD_B, LANES), lambda g, t: (0, 0, g))
    kern = functools.partial(_rwkv_scan_kernel, tb=tb)
    return pl.pallas_call(
        kern,
        grid=(nl // LANES, nt // tb),
        in_specs=[seq] * 6 + [st],
        out_specs=[seq, st],
        out_shape=[jax.ShapeDtypeStruct((nt, HD_B, nl), F32),
                   jax.ShapeDtypeStruct((HD_B, HD_B, nl), F32)],
        scratch_shapes=[pltpu.VMEM((HD_B, HD_B, LANES), F32)],
        compiler_params=_cparams(("parallel", "arbitrary")),
        name="rwkv_scan",
    )(r, km, v, lw, kk, bb, s0)


PAIR = 2 * HD_B
RWKV_SEQS_PER_STEP = 8


def _rwkv_chunk_kernel(r_ref, km_ref, v_ref, lw_ref, kk_ref, bb_ref, s0_ref, o_ref, sout_ref,
                       s_scr, *, chunk, nb):
    c = pl.program_id(1)

    @pl.when(c == 0)
    def _():
        s_scr[...] = s0_ref[...]

    lane = lax.broadcasted_iota(jnp.int32, (chunk, PAIR), 1)
    row = lax.broadcasted_iota(jnp.int32, (chunk, PAIR), 0)
    low_half = lane < HD_B
    li = jnp.where(low_half, lane, lane - HD_B)
    strict = li < row
    incl = li <= row
    eye = jnp.where(li == row, 1.0, 0.0)
    same = lambda n: (li // n) == (row // n)
    m8 = same(8)
    levels = []
    n = 16
    prev = m8
    while n <= chunk:
        cur = same(n)
        levels.append(jnp.logical_and(cur, jnp.logical_not(prev)))
        prev = cur
        n *= 2
    sq_r = lax.broadcasted_iota(jnp.int32, (PAIR, PAIR), 0)
    sq_c = lax.broadcasted_iota(jnp.int32, (PAIR, PAIR), 1)
    same_head = (sq_r < HD_B) == (sq_c < HD_B)
    low_bf = lax.broadcasted_iota(jnp.int32, (chunk, PAIR), 1) < HD_B

    def bd(s):
        return tuple(jnp.concatenate([jnp.where(low_bf, x, jnp.zeros_like(x)),
                                      jnp.where(low_bf, jnp.zeros_like(x), x)], axis=0)
                     for x in s)

    def pm1(x, y):
        return jnp.dot(x.astype(BF16), bd((y.astype(BF16),))[0], preferred_element_type=F32)

    chains = [(b, p) for b in range(nb) for p in range(H_B // 2)]
    each = lambda f: [f(i) for i in range(len(chains))]

    vv, lhs2, kg, bg, kdbd, etot, s_p = [], [], [], [], [], [], []
    for b, p in chains:
        sl = slice(p * PAIR, (p + 1) * PAIR)
        r = r_ref[b, :, sl]
        km = km_ref[b, :, sl]
        lw = lw_ref[b, :, sl]
        kk = kk_ref[b, :, sl]
        bb = bb_ref[b, :, sl]
        cum = lw
        d = 1
        while d < chunk:
            cum = cum + jnp.where(row >= d, pltpu.roll(cum, d, 0), 0.0)
            d *= 2
        tot = cum[chunk - 1:chunk, :]
        g_end = jnp.exp(tot - cum)
        g_inv = jnp.exp(-cum)
        vv.append(v_ref[b, :, sl])
        lhs2.append(_split(jnp.concatenate([kk * jnp.exp(cum - lw), r * jnp.exp(cum)], axis=0)))
        kg.append(bd(_split(km * g_inv)))
        bg.append(bd(_split(bb * g_inv)))
        kdbd.append(_split(jnp.concatenate([km * g_end, bb * g_end], axis=0)))
        etot.append(jnp.exp(tot))
        s_p.append(s_scr[b, p])

    ss = each(lambda i: lax.dot_general(lhs2[i][0], s_p[i].astype(BF16), (((1,), (1,)), ((), ())),
                                        preferred_element_type=F32))
    kb = each(lambda i: _mm(lhs2[i], tuple(jnp.concatenate([x, y], axis=0)
                                           for x, y in zip(kg[i], bg[i])), nt=True))
    a_b = each(lambda i: jnp.where(strict, kb[i][:chunk, PAIR:], 0.0))
    b_b = each(lambda i: jnp.where(incl, kb[i][chunk:, PAIR:], 0.0))
    av = each(lambda i: pm1(jnp.concatenate(
        [jnp.where(strict, kb[i][:chunk, :PAIR], 0.0), jnp.where(incl, kb[i][chunk:, :PAIR], 0.0)],
        axis=0), vv[i]))

    dg = each(lambda i: jnp.where(m8, a_b[i], 0.0))
    d2 = each(lambda i: pm1(dg[i], dg[i]))
    d3 = each(lambda i: pm1(dg[i], d2[i]))
    d4 = each(lambda i: pm1(d2[i], d2[i]))
    t1 = each(lambda i: eye - dg[i] + d2[i] - d3[i])
    inv = each(lambda i: t1[i] + pm1(t1[i], d4[i]))
    for m in levels:
        mlev = each(lambda i: pm1(inv[i], jnp.where(m, a_b[i], 0.0)))
        inv = each(lambda i: inv[i] - pm1(mlev[i], inv[i]))

    u = each(lambda i: pm1(inv[i], ss[i][:chunk] + av[i][:chunk]))
    upd = each(lambda i: _mm(_split(jnp.transpose(jnp.concatenate([vv[i], -u[i]], axis=0))),
                             kdbd[i]))
    bu = each(lambda i: pm1(b_b[i], u[i]))
    for i, (b, p) in enumerate(chains):
        sl = slice(p * PAIR, (p + 1) * PAIR)
        o_ref[b, :, sl] = ss[i][chunk:] + av[i][chunk:] - bu[i]
        s_scr[b, p] = s_p[i] * etot[i] + jnp.where(same_head, upd[i], 0.0)

    sout_ref[...] = s_scr[...]


def _rwkv_chunk(r, km, v, lw, kk, bb, s0, *, chunk, nb):
    nseq, t, _ = r.shape
    assert chunk == HD_B and nseq % nb == 0
    npair = H_B // 2
    row = pl.BlockSpec((nb, chunk, W_B), lambda b, c: (b, c, 0))
    st = pl.BlockSpec((nb, npair, PAIR, PAIR), lambda b, c: (b, 0, 0, 0))
    kern = functools.partial(_rwkv_chunk_kernel, chunk=chunk, nb=nb)
    return pl.pallas_call(
        kern,
        grid=(nseq // nb, t // chunk),
        in_specs=[row] * 6 + [st],
        out_specs=[row, st],
        out_shape=[jax.ShapeDtypeStruct((nseq, t, W_B), F32),
                   jax.ShapeDtypeStruct((nseq, npair, PAIR, PAIR), F32)],
        scratch_shapes=[pltpu.VMEM((nb, npair, PAIR, PAIR), F32)],
        compiler_params=_cparams(("parallel", "arbitrary")),
        name="rwkv_chunk",
    )(r, km, v, lw, kk, bb, s0)


MXU_N = 256


def _mix_residual(x_ref, oa_ref, o_ref, r_ref, km_ref, v_ref, g_ref, lnw_ref, lnb_ref, rk_ref,
                  wa_ref, wb_ref):
    ones = _head_ones(W_B)
    o = o_ref[...]
    cen = o - _head_sum(o, ones) * (1.0 / HD_B)
    head_sum1 = lambda z: jnp.dot(z.astype(BF16), ones, preferred_element_type=F32)
    var = head_sum1(cen * cen) * (1.0 / HD_B)
    on = cen * lax.rsqrt(var + GN_EPS) * lnw_ref[...] + lnb_ref[...]
    bonus = head_sum1(r_ref[...] * km_ref[...] * rk_ref[...]) * v_ref[...]
    ob = (on + bonus) * g_ref[...]
    return (x_ref[...]
            + jnp.dot(oa_ref[...].astype(BF16), wa_ref[...], preferred_element_type=F32)
            + jnp.dot(ob.astype(BF16), wb_ref[...], preferred_element_type=F32))


def _proj_residual(x_ref, o_ref, w_ref):
    return x_ref[...] + jnp.dot(o_ref[...].astype(BF16), w_ref[...],
                                preferred_element_type=F32)


def _res_ffn_kernel(*refs, residual, n_res, final_norm):
    g_ref, wg_ref, wu_ref, wd_ref, gf_ref, y_ref, xf_scr, x_scr, h_scr = refs[n_res:]
    x = residual(*refs[:n_res])
    xf_scr[...] = _rms(x, g_ref[...]).astype(BF16)
    x_scr[...] = x
    for j in range(wg_ref.shape[1] // MXU_N):
        cols = slice(j * MXU_N, (j + 1) * MXU_N)
        xf = xf_scr[...]
        hg = jnp.dot(xf, wg_ref[:, cols], preferred_element_type=F32)
        hu = jnp.dot(xf, wu_ref[:, cols], preferred_element_type=F32)
        h_scr[:, cols] = ((hg * _sigmoid(hg)) * hu).astype(BF16)
    y = x_scr[...] + jnp.dot(h_scr[...], wd_ref[...], preferred_element_type=F32)
    if final_norm:
        y = _rms(y, gf_ref[...])
    y_ref[...] = y


def _res_ffn(residual, row_ins, const_ins, g, wg, wu, wd, gf, *, tm, final_norm):
    n, d = row_ins[0].shape
    whole = lambda a: pl.BlockSpec(a.shape, lambda i: (0,) * a.ndim,
                                   pipeline_mode=pl.Buffered(1))
    consts = list(const_ins) + [g.reshape(1, d), wg, wu, wd, gf.reshape(1, d)]
    kern = functools.partial(_res_ffn_kernel, residual=residual,
                             n_res=len(row_ins) + len(const_ins), final_norm=final_norm)
    return pl.pallas_call(
        kern,
        grid=(n // tm,),
        in_specs=[pl.BlockSpec((tm, a.shape[1]), lambda i: (i, 0)) for a in row_ins]
                 + [whole(a) for a in consts],
        out_specs=pl.BlockSpec((tm, d), lambda i: (i, 0)),
        out_shape=jax.ShapeDtypeStruct((n, d), F32),
        scratch_shapes=[pltpu.VMEM((tm, d), BF16), pltpu.VMEM((tm, d), F32),
                        pltpu.VMEM((tm, wg.shape[1]), BF16)],
        compiler_params=_cparams(("parallel",)),
        name="res_ffn",
    )(*row_ins, *consts)


HGRN_SAFE_LOG = 80.0


def _hgrn_kernel(q_ref, f_ref, i_ref, g_ref, s0_ref, lbc_ref, gn_ref, o_ref, sout_ref,
                 s_scr, att_scr, *, layer, chunk, valid, nb):
    c = pl.program_id(1)

    @pl.when(c == 0)
    def _():
        s_scr[...] = s0_ref[...]

    lbc = lbc_ref[...]
    e = jnp.exp(lbc - jnp.max(lbc, axis=0, keepdims=True))
    sm = e / jnp.sum(e, axis=0, keepdims=True)
    lbf = sm[1:2, :]
    for l in range(2, layer + 1):
        lbf = lbf + sm[l:l + 1, :]

    nblk = chunk // SUBLANES
    sub = lax.broadcasted_iota(jnp.int32, (SUBLANES, DK_C), 0)
    lane_c = lax.broadcasted_iota(jnp.int32, (SUBLANES, chunk), 1)
    rowc = lax.broadcasted_iota(jnp.int32, (chunk, F_C), 0)
    tril = jnp.where(lax.broadcasted_iota(jnp.int32, (chunk, chunk), 0)
                     >= lax.broadcasted_iota(jnp.int32, (chunk, chunk), 1), 1.0, 0.0).astype(BF16)

    qs, ks, bs, ivs = [], [], [], []
    for b in range(nb):
        fr = f_ref[b]
        ez = jnp.exp(-jnp.abs(fr))
        rz = 1.0 / (1.0 + ez)
        sig = jnp.where(fr >= 0.0, rz, ez * rz)
        nsig = jnp.where(fr >= 0.0, ez * rz, rz)
        logf = jnp.log(lbf + (1.0 - lbf) * sig)
        kfull = (1.0 - lbf) * nsig
        if valid < chunk:
            logf = jnp.where(rowc < valid, logf, 0.0)
            kfull = jnp.where(rowc < valid, kfull, 0.0)
        hi, lo = _split(logf)
        bc = jnp.dot(jnp.concatenate([tril, tril], axis=1), jnp.concatenate([hi, lo], axis=0),
                     preferred_element_type=F32)
        qs.append(jax.nn.silu(q_ref[b]))
        ks.append(kfull)
        bs.append(bc)
        ivs.append(i_ref[b].astype(BF16))

    chains = [(b, h) for b in range(nb) for h in range(H_C)]
    each = lambda f: [f(i) for i in range(len(chains))]
    hsl = lambda h: slice(h * DK_C, (h + 1) * DK_C)
    q = each(lambda i: qs[chains[i][0]][:, hsl(chains[i][1])])
    k = each(lambda i: ks[chains[i][0]][:, hsl(chains[i][1])])
    bcs = each(lambda i: bs[chains[i][0]][:, hsl(chains[i][1])])
    iv = each(lambda i: ivs[chains[i][0]][:, hsl(chains[i][1])])
    s_h = each(lambda i: s_scr[chains[i][0], chains[i][1]])

    def off_diag(i, tb):
        lo = tb * SUBLANES
        rs = slice(lo, lo + SUBLANES)
        edge = bcs[i][lo - 1:lo, :]
        qi = q[i][rs, :] * jnp.exp(bcs[i][rs, :] - edge)
        ki = k[i][0:lo, :] * jnp.exp(edge - bcs[i][0:lo, :])
        ki = jnp.concatenate([ki, jnp.zeros((chunk - lo, DK_C), F32)], axis=0)
        return lax.dot_general(qi.astype(BF16), ki.astype(BF16), (((1,), (1,)), ((), ())),
                               preferred_element_type=F32)

    inter = each(lambda i: jnp.dot((q[i] * jnp.exp(bcs[i])).astype(BF16), s_h[i].astype(BF16),
                                   preferred_element_type=F32))


    sub_c = lax.broadcasted_iota(jnp.int32, (SUBLANES, chunk), 0)

    def att_on_matrix_unit():
        def block(i, tb):
            lo = tb * SUBLANES
            rs = slice(lo, lo + SUBLANES)
            dloc = bcs[i][rs, :] if tb == 0 else bcs[i][rs, :] - bcs[i][lo - 1:lo, :]
            qi = q[i][rs, :] * jnp.exp(dloc)
            parts = [k[i][rs, :] * jnp.exp(-dloc)]
            if tb > 0:
                parts.insert(0, k[i][0:lo, :] * jnp.exp(bcs[i][lo - 1:lo, :] - bcs[i][0:lo, :]))
            if chunk - lo - SUBLANES > 0:
                parts.append(jnp.zeros((chunk - lo - SUBLANES, DK_C), F32))
            ki = jnp.concatenate(parts, axis=0) if len(parts) > 1 else parts[0]
            return lax.dot_general(qi.astype(BF16), ki.astype(BF16), (((1,), (1,)), ((), ())),
                                   preferred_element_type=F32)

        blks = each(lambda i: [block(i, tb) for tb in range(nblk)])
        for i in range(len(chains)):
            for tb in range(nblk):
                att_scr[i, tb * SUBLANES:(tb + 1) * SUBLANES, :] = jnp.where(
                    lane_c <= sub_c + tb * SUBLANES, blks[i][tb], 0.0)

    def att_pairwise_diagonal():
        offd = each(lambda i: [off_diag(i, tb) for tb in range(1, nblk)])
        for i in range(len(chains)):
            for tb in range(nblk):
                rs = slice(tb * SUBLANES, (tb + 1) * SUBLANES)
                blk = jnp.zeros((SUBLANES, chunk), F32) if tb == 0 else offd[i][tb - 1]
                for j in range(SUBLANES):
                    s = tb * SUBLANES + j
                    if s >= valid:
                        break
                    diff = jnp.where(sub >= j, bcs[i][rs, :] - bcs[i][s:s + 1, :], -jnp.inf)
                    ev = jnp.exp(diff) * (q[i][rs, :] * k[i][s:s + 1, :])
                    blk = jnp.where(lane_c == s, jnp.sum(ev, axis=1, keepdims=True), blk)
                att_scr[i, rs, :] = blk

    def new_state(i):
        bl = bcs[i][chunk - 1:chunk, :]
        kd = k[i] * jnp.exp(bl - bcs[i])
        pieces = [kd, jnp.broadcast_to(jnp.exp(bl), (SUBLANES, DK_C))]
        fill = DK_C - chunk - SUBLANES
        if fill > 0:
            pieces.append(jnp.zeros((fill, DK_C), F32))
        xt = jnp.transpose(jnp.concatenate(pieces, axis=0))
        return (xt[:, chunk:chunk + 1] * s_h[i]
                + jnp.dot(xt[:, 0:chunk].astype(BF16), iv[i], preferred_element_type=F32))

    if nblk > 1:
        att_on_matrix_unit()
        worst = jnp.float32(0.0)
        for b in range(nb):
            win = bs[b] - jnp.where(rowc >= SUBLANES, pltpu.roll(bs[b], SUBLANES, 0), 0.0)
            worst = jnp.minimum(worst, jnp.min(win))
        pl.when(worst < -HGRN_SAFE_LOG)(att_pairwise_diagonal)
    else:
        att_pairwise_diagonal()

    o_h = each(lambda i: jnp.dot(att_scr[i].astype(BF16), iv[i], preferred_element_type=F32)
               + inter[i])
    s_new = each(new_state)
    for i, (b, h) in enumerate(chains):
        s_scr[b, h] = s_new[i]
    for b in range(nb):
        o = jnp.concatenate([o_h[b * H_C + h] for h in range(H_C)], axis=1)
        o_ref[b] = _rms(o, gn_ref[...]) * jax.nn.silu(g_ref[b])
    sout_ref[...] = s_scr[...]


def _hgrn(pc3, s0, lbc, gn, *, layer, chunk, valid, nb):
    nseq, t, _ = pc3.shape
    assert nseq % nb == 0
    col = lambda j: pl.BlockSpec((nb, chunk, F_C), lambda b, c, j=j: (b, c, j))
    st = pl.BlockSpec((nb, H_C, DK_C, DK_C), lambda b, c: (b, 0, 0, 0))
    kern = functools.partial(_hgrn_kernel, layer=layer, chunk=chunk, valid=valid, nb=nb)
    return pl.pallas_call(
        kern,
        grid=(nseq // nb, t // chunk),
        in_specs=[col(0), col(1), col(2), col(3), st,
                  pl.BlockSpec(lbc.shape, lambda b, c: (0, 0)),
                  pl.BlockSpec((1, D_MODEL), lambda b, c: (0, 0))],
        out_specs=[pl.BlockSpec((nb, chunk, D_MODEL), lambda b, c: (b, c, 0)), st],
        out_shape=[jax.ShapeDtypeStruct((nseq, t, D_MODEL), F32),
                   jax.ShapeDtypeStruct(s0.shape, F32)],
        scratch_shapes=[pltpu.VMEM((nb, H_C, DK_C, DK_C), F32),
                        pltpu.VMEM((nb * H_C, chunk, chunk), F32)],
        compiler_params=_cparams(("parallel", "arbitrary")),
        name="hgrn",
    )(pc3, pc3, pc3, pc3, s0, lbc, gn.reshape(1, D_MODEL))


def _block_diag(w):
    h, n, _ = w.shape
    eye = jnp.eye(h, dtype=w.dtype)
    return (eye[:, None, :, None] * w[:, :, None, :]).reshape(h * n, h * n)


ROW_TILE = 512
HGRN_CHUNK = 64
HGRN_SEQS_PER_STEP = 4


def _trunk(x2d, conv_st, h_st, shift_st, rs_st, hs_st, P, *, nbatch, nsteps, time_major):
    n = x2d.shape[0]
    if time_major:
        nseq, step, rows = 1, nbatch, n
        tm = n
    else:
        nseq, step, rows = nbatch, 1, ROW_TILE
        tm = ROW_TILE
    nchain = nbatch * H_B

    out_a, conv_new, h_new, r, km, v, lw, kk, bb, g, shift_new = _front(
        x2d, P['ln_mix'][0], P['w_in_ab'],
        conv_st, h_st, P['conv_w'], P['conv_b'], P['wr'], P['wi'], P['gr_b'], P['gi_b'],
        P['lru_lambda'],
        shift_st, P['mu_b'], P['w0_b'], P['a0_b'], P['w2p'], P['a2p'], P['g2_b'],
        P['kk_b'], P['ka_b'], nseq=nseq, step=step, rows=rows)

    if time_major:
        def to_lanes(z):
            z = z.reshape(nsteps, nbatch, H_B, HD_B).transpose(0, 3, 1, 2)
            return z.reshape(nsteps, HD_B, nchain)

        s0 = rs_st.transpose(2, 3, 0, 1).reshape(HD_B, HD_B, nchain)
        o, s_new = _rwkv_scan(to_lanes(r), to_lanes(km), to_lanes(v), to_lanes(lw), to_lanes(kk),
                              to_lanes(bb), s0, tb=nsteps)
        o = o.reshape(nsteps, HD_B, nbatch, H_B).transpose(0, 2, 3, 1).reshape(n, W_B)
        rs_new = s_new.reshape(HD_B, HD_B, nbatch, H_B).transpose(2, 3, 0, 1)
    else:
        z = jnp.zeros_like(rs_st[:, 0::2])
        s0 = jnp.concatenate([jnp.concatenate([rs_st[:, 0::2], z], axis=-1),
                              jnp.concatenate([z, rs_st[:, 1::2]], axis=-1)], axis=-2)
        seq3 = lambda y: y.reshape(nbatch, nsteps, W_B)
        o, s_new = _rwkv_chunk(seq3(r), seq3(km), seq3(v), seq3(lw), seq3(kk), seq3(bb), s0,
                               chunk=HD_B, nb=RWKV_SEQS_PER_STEP)
        o = o.reshape(n, W_B)
        rs_new = jnp.stack([s_new[:, :, :HD_B, :HD_B], s_new[:, :, HD_B:, HD_B:]],
                           axis=2).reshape(nbatch, H_B, HD_B, HD_B)

    vec = lambda p: p.reshape(1, W_B)
    x2 = _res_ffn(_mix_residual, [x2d, out_a, o, r, km, v, g],
                  [vec(P['lnx_w']), vec(P['lnx_b']), vec(P['rk_b']), P['w_out_a'], P['w_out_b']],
                  P['ln_ffn'][0], P['ffn_gate'][0], P['ffn_up'][0], P['ffn_down'][0],
                  P['ln_final'], tm=tm, final_norm=False)

    pc = _norm_matmul(x2, P['ln_mix'][1], P['w_in_c'], tm)
    if time_major:
        pc3 = pc.reshape(nsteps, nbatch, 4 * F_C).transpose(1, 0, 2)
        chunk = SUBLANES
        pc3 = jnp.pad(pc3, ((0, 0), (0, chunk - nsteps), (0, 0)))
    else:
        pc3 = pc.reshape(nbatch, nsteps, 4 * F_C)
        chunk = HGRN_CHUNK
    o3, hs_new = _hgrn(pc3, hs_st, P['lb_c'], P['gn_c'], layer=1, chunk=chunk,
                       valid=min(chunk, nsteps),
                       nb=2 * HGRN_SEQS_PER_STEP if time_major else HGRN_SEQS_PER_STEP)
    if time_major:
        o = o3[:, :nsteps].transpose(1, 0, 2).reshape(n, D_MODEL)
    else:
        o = o3.reshape(n, D_MODEL)
    y = _res_ffn(_proj_residual, [x2, o], [P['w_out_c']],
                 P['ln_ffn'][1], P['ffn_gate'][1], P['ffn_up'][1], P['ffn_down'][1],
                 P['ln_final'], tm=tm, final_norm=True)
    return y, conv_new, h_new, shift_new, rs_new, hs_new


def kernel(x_prompt, x_sample, state_rglru_conv, state_rglru_h, state_rwkv_shift, state_rwkv_S,
           state_hgrn_S, ln_mix, ln_ffn, ln_final, w_in_ab, conv_w, conv_b, gr_w, gr_b, gi_w,
           gi_b, lru_lambda, mu_b, w0_b, w2_b, a0_b, a2_b, g2_b, kk_b, ka_b, rk_b, lnx_w, lnx_b,
           w_out_ab, w_in_c, lb_c, gn_c, w_out_c, ffn_gate, ffn_up, ffn_down):
    bp, tp, _ = x_prompt.shape
    bs, ts, _ = x_sample.shape
    zpad_w = jnp.zeros((LORA_A, W_B), F32)
    zpad_a = jnp.zeros((LORA_W, W_B), F32)
    P = dict(
        ln_mix=ln_mix, ln_ffn=ln_ffn, ln_final=ln_final,
        w_in_ab=w_in_ab[0], conv_w=conv_w[0], conv_b=conv_b[0],
        wr=_block_diag(gr_w[0]).astype(BF16), wi=_block_diag(gi_w[0]).astype(BF16),
        gr_b=gr_b[0], gi_b=gi_b[0], lru_lambda=lru_lambda[0], mu_b=mu_b[0], w0_b=w0_b[0],
        a0_b=a0_b[0],
        w2p=jnp.concatenate([w2_b[0], zpad_w], axis=0).astype(BF16),
        a2p=jnp.concatenate([zpad_a, a2_b[0]], axis=0).astype(BF16),
        g2_b=g2_b[0].astype(BF16), kk_b=kk_b[0], ka_b=ka_b[0], rk_b=rk_b[0],
        lnx_w=lnx_w[0], lnx_b=lnx_b[0],
        w_out_a=w_out_ab[0, :W_A].astype(BF16), w_out_b=w_out_ab[0, W_A:].astype(BF16),
        w_in_c=w_in_c[0], lb_c=lb_c, gn_c=gn_c[0],
        w_out_c=w_out_c[0].astype(BF16), ffn_gate=ffn_gate.astype(BF16),
        ffn_up=ffn_up.astype(BF16), ffn_down=ffn_down.astype(BF16))

    yp, p_conv, p_h, p_shift, p_rs, p_hs = _trunk(
        x_prompt.reshape(bp * tp, D_MODEL),
        jnp.zeros((bp, CONV_W - 1, W_A), F32), jnp.zeros((bp, 1, W_A), F32),
        jnp.zeros((bp, 1, P_B), F32), jnp.zeros((bp, H_B, HD_B, HD_B), F32),
        jnp.zeros((bp, H_C, DK_C, DK_C), F32), P, nbatch=bp, nsteps=tp, time_major=False)

    ys, s_conv, s_h, s_shift, s_rs, s_hs = _trunk(
        x_sample.transpose(1, 0, 2).reshape(ts * bs, D_MODEL),
        state_rglru_conv[0].transpose(1, 0, 2).reshape(1, (CONV_W - 1) * bs, W_A),
        state_rglru_h[0].reshape(1, bs, W_A), state_rwkv_shift[0].reshape(1, bs, P_B),
        state_rwkv_S[0], state_hgrn_S[0], P, nbatch=bs, nsteps=ts, time_major=True)

    return (yp.reshape(bp, tp, D_MODEL),
            ys.reshape(ts, bs, D_MODEL).transpose(1, 0, 2),
            p_conv[None], p_h.reshape(1, bp, W_A), p_shift.reshape(1, bp, P_B), p_rs[None],
            p_hs[None],
            s_conv.reshape(CONV_W - 1, bs, W_A).transpose(1, 0, 2)[None],
            s_h.reshape(1, bs, W_A), s_shift.reshape(1, bs, P_B), s_rs[None], s_hs[None])
```
